```python
import jax, jax.numpy as jnp
from jax import lax
import numpy as np

D_MODEL = 1024
BATCH = 8
SEQ = 2048
DEPTH = 2

HEAD_DIM = 64
ATTN_WIDTH = D_MODEL // 2
ATTN_HEADS = ATTN_WIDTH // HEAD_DIM
CONV_WIDTH = D_MODEL // 4
CONV_HEADS = CONV_WIDTH // HEAD_DIM
CONV_K = 3
POOL_WIDTH = D_MODEL // 4
POOL_WINDOWS = (2, 4, 8, 16)
POOL_GROUPS = len(POOL_WINDOWS)
POOL_GROUP_DIM = POOL_WIDTH // POOL_GROUPS
MIX_WIDTH = ATTN_WIDTH + CONV_WIDTH + POOL_WIDTH
IN_WIDTH = 3 * ATTN_WIDTH + 3 * CONV_WIDTH + POOL_WIDTH
MOBA_BLOCK = 256
MOBA_TOPK = 3
Q_CHUNK = 32
D_FF = -(-(8 * D_MODEL) // (3 * 256)) * 256
NORM_EPS = 1e-6

kernel_name = "hybrid_moba_conv_pool_block"


def rms_norm(x, g):
    xf = x.astype(jnp.float32)
    y = xf * lax.rsqrt(jnp.mean(xf * xf, axis=-1, keepdims=True) + NORM_EPS)
    return (y * g.astype(jnp.float32)).astype(x.dtype)


def alibi_slopes(n_heads):
    return jnp.asarray(2.0 ** (-8.0 * np.arange(1, n_heads + 1) / n_heads), dtype=jnp.float32)


def moba_attention(q, k, v):
    B, H, S, Dh = q.shape
    nb = -(-S // MOBA_BLOCK)
    s_pad = nb * MOBA_BLOCK
    topk = min(MOBA_TOPK, nb)
    pad = ((0, 0), (0, 0), (0, s_pad - S), (0, 0))
    kp = jnp.pad(k, pad)
    vp = jnp.pad(v, pad)
    k_blocks = kp.reshape(B, H, nb, MOBA_BLOCK, Dh)
    v_blocks = vp.reshape(B, H, nb, MOBA_BLOCK, Dh)
    k_mean = jnp.mean(k_blocks.astype(jnp.float32), axis=3)
    slopes = alibi_slopes(H)
    scale = Dh ** -0.5
    b_idx = jnp.arange(B)[:, None, None]
    h_idx = jnp.arange(H)[None, :, None]
    blk_pos = jnp.arange(MOBA_BLOCK)
    neg_inf = jnp.float32(-jnp.inf)

    def chunk(start):
        qc = lax.dynamic_slice_in_dim(q, start, Q_CHUNK, axis=2).astype(jnp.float32)
        t = start + jnp.arange(Q_CHUNK)
        cur = start // MOBA_BLOCK
        gate = jnp.einsum('bhqd,bhnd->bhqn', qc, k_mean)
        gate = jnp.where(jnp.arange(nb) < cur, gate, neg_inf)
        _, gidx = lax.top_k(gate, topk)
        valid = jnp.arange(topk) < cur
        own_start = cur * MOBA_BLOCK
        k_own = lax.dynamic_slice_in_dim(kp, own_start, MOBA_BLOCK, axis=2)
        v_own = lax.dynamic_slice_in_dim(vp, own_start, MOBA_BLOCK, axis=2)
        s_own = own_start + blk_pos
        dist_own = (t[:, None] - s_own[None, :]).astype(jnp.float32)
        logit_own = (jnp.einsum('bhqd,bhkd->bhqk', qc, k_own) * scale
                     - slopes[:, None, None] * dist_own)
        logit_own = jnp.where(dist_own >= 0, logit_own, neg_inf)
        flat_idx = gidx.reshape(B, H, Q_CHUNK * topk)
        k_sel = k_blocks[b_idx, h_idx, flat_idx].reshape(B, H, Q_CHUNK, topk, MOBA_BLOCK, Dh)
        v_sel = v_blocks[b_idx, h_idx, flat_idx].reshape(B, H, Q_CHUNK, topk, MOBA_BLOCK, Dh)
        s_sel = gidx[..., None] * MOBA_BLOCK + blk_pos
        dist_sel = (t[None, None, :, None, None] - s_sel).astype(jnp.float32)
        logit_sel = (jnp.einsum('bhqd,bhqnkd->bhqnk', qc, k_sel) * scale
                     - slopes[None, :, None, None, None] * dist_sel)
        logit_sel = jnp.where(valid[:, None], logit_sel, neg_inf)
        logits = jnp.concatenate(
            [logit_own, logit_sel.reshape(B, H, Q_CHUNK, topk * MOBA_BLOCK)], axis=-1)
        p = jax.nn.softmax(logits, axis=-1)
        p_own = p[..., :MOBA_BLOCK]
        p_sel = p[..., MOBA_BLOCK:].reshape(B, H, Q_CHUNK, topk, MOBA_BLOCK)
        o = (jnp.einsum('bhqk,bhkd->bhqd', p_own, v_own)
             + jnp.einsum('bhqnk,bhqnkd->bhqd', p_sel, v_sel))
        return o.astype(q.dtype)

    starts = jnp.arange(S // Q_CHUNK) * Q_CHUNK
    out = lax.map(chunk, starts)
    return out.transpose(1, 0, 3, 2, 4).reshape(B, S, H * Dh)


def short_conv_mixer(h, b_gate, c_gate, conv_w):
    S = h.shape[1]
    u = c_gate * h
    up = jnp.pad(u, ((0, 0), (CONV_K - 1, 0), (0, 0)))
    conv = up[:, 0:S] * conv_w[0]
    for j in range(1, CONV_K):
        conv = conv + up[:, j:j + S] * conv_w[j]
    return b_gate * conv


def pool_mixer(u, pool_w, pool_scale):
    B, S, _ = u.shape
    uf = u.astype(jnp.float32).reshape(B, S, POOL_GROUPS, POOL_GROUP_DIM)
    cs = jnp.cumsum(uf, axis=1)
    t = jnp.arange(1, S + 1, dtype=jnp.float32)
    outs = []
    for g, w in enumerate(POOL_WINDOWS):
        csg = cs[:, :, g]
        lag = jnp.pad(csg, ((0, 0), (w, 0), (0, 0)))[:, :S]
        cnt = jnp.minimum(t, jnp.float32(w))[None, :, None]
        outs.append((csg - lag) / cnt - uf[:, :, g])
    pooled = jnp.stack(outs, axis=2).astype(u.dtype)
    y = jnp.einsum('bsgc,gcd->bsgd', pooled, pool_w)
    return y.reshape(B, S, POOL_WIDTH) * pool_scale


def hybrid_layer(x, w_in, w_out, conv_w, pool_w, pool_scale,
                 g_pre_mix, g_post_mix, g_pre_ffn, g_post_ffn, w_gate, w_up, w_down):
    B, S, _ = x.shape
    h = rms_norm(x, g_pre_mix)
    proj = h @ w_in
    offs = np.cumsum([ATTN_WIDTH, ATTN_WIDTH, ATTN_WIDTH, CONV_WIDTH, CONV_WIDTH, CONV_WIDTH]).tolist()
    q, k, v, h_conv, b_gate, c_gate, u_pool = jnp.split(proj, offs, axis=-1)
    to_heads = lambda a: a.reshape(B, S, ATTN_HEADS, HEAD_DIM).transpose(0, 2, 1, 3)
    attn = moba_attention(to_heads(q), to_heads(k), to_heads(v))
    conv = short_conv_mixer(h_conv, b_gate, c_gate, conv_w)
    pool = pool_mixer(u_pool, pool_w, pool_scale)
    mixed = jnp.concatenate([attn, conv, pool], axis=-1) @ w_out
    x = x + rms_norm(mixed, g_post_mix)
    hf = rms_norm(x, g_pre_ffn)
    ff = (jax.nn.silu(hf @ w_gate) * (hf @ w_up)) @ w_down
    return x + rms_norm(ff, g_post_ffn)


def setup_inputs(seed: int = 0) -> dict:
    key = jax.random.key(seed)
    ks = jax.random.split(key, 14)
    f32 = jnp.float32
    nrm = lambda k, shape, s: jax.random.normal(k, shape, f32) * s
    gain = lambda k: 1.0 + 0.05 * jax.random.normal(k, (DEPTH, D_MODEL), f32)
    return {
        "x": jax.random.normal(ks[0], (BATCH, SEQ, D_MODEL), f32),
        "w_in": nrm(ks[1], (DEPTH, D_MODEL, IN_WIDTH), D_MODEL ** -0.5),
        "w_out": nrm(ks[2], (DEPTH, MIX_WIDTH, D_MODEL), MIX_WIDTH ** -0.5),
        "conv_w": nrm(ks[3], (DEPTH, CONV_K, CONV_WIDTH), CONV_K ** -0.5),
        "pool_w": nrm(ks[4], (DEPTH, POOL_GROUPS, POOL_GROUP_DIM, POOL_GROUP_DIM), POOL_GROUP_DIM ** -0.5),
        "pool_scale": 1.0 + 0.1 * jax.random.normal(ks[5], (DEPTH, POOL_WIDTH), f32),
        "g_pre_mix": gain(ks[6]),
        "g_post_mix": gain(ks[7]),
        "g_pre_ffn": gain(ks[8]),
        "g_post_ffn": gain(ks[9]),
        "w_gate": nrm(ks[10], (DEPTH, D_MODEL, D_FF), D_MODEL ** -0.5),
        "w_up": nrm(ks[11], (DEPTH, D_MODEL, D_FF), D_MODEL ** -0.5),
        "w_down": nrm(ks[12], (DEPTH, D_FF, D_MODEL), D_FF ** -0.5),
    }


def reference(x, w_in, w_out, conv_w, pool_w, pool_scale,
              g_pre_mix, g_post_mix, g_pre_ffn, g_post_ffn, w_gate, w_up, w_down):
    for l in range(DEPTH):
        x = hybrid_layer(x, w_in[l], w_out[l], conv_w[l], pool_w[l], pool_scale[l],
                         g_pre_mix[l], g_post_mix[l], g_pre_ffn[l], g_post_ffn[l],
                         w_gate[l], w_up[l], w_down[l])
    return x
```

```python
import functools

import jax
import jax.numpy as jnp
from jax import lax
from jax.experimental import pallas as pl
from jax.experimental.pallas import tpu as pltpu

D_MODEL = 1024
HEAD_DIM = 64
ATTN_WIDTH = 512
ATTN_HEADS = 8
CONV_WIDTH = 256
CONV_K = 3
POOL_WIDTH = 256
POOL_WINDOWS = (2, 4, 8, 16)
POOL_GROUP_DIM = 64
IN_WIDTH = 2560
MOBA_BLOCK = 256
MOBA_TOPK = 3
D_FF = 2816
NORM_EPS = 1e-6

LANES = 128
HEAD_PAIRS = ATTN_WIDTH // LANES
HALO = 16
GATE_ROWS = 16
QKV_WIDTH = 3 * ATTN_WIDTH
CONV_OFF = QKV_WIDTH
POOL_OFF = CONV_OFF + 3 * CONV_WIDTH

IN_TM = 512
FFN_TM = 512
FFN_CHUNK = 1408
VMEM_LIMIT = 56 * 1024 * 1024

BF16 = jnp.bfloat16
F32 = jnp.float32


def _rms(x, g):
    return x * lax.rsqrt(jnp.mean(x * x, axis=-1, keepdims=True) + NORM_EPS) * g


def _in_proj_kernel(x_ref, g_ref, w_ref, cw_ref, pw_ref, ps_ref, qkv_ref, cp_ref,
                    cbuf, pa, pb, pc, pd):
    j = pl.program_id(1)
    tm = IN_TM

    @pl.when(j == 0)
    def _():
        cbuf[0:HALO, :] = jnp.zeros((HALO, CONV_WIDTH), F32)
        pa[0:2 * HALO, :] = jnp.zeros((2 * HALO, POOL_WIDTH), F32)
        pb[0:HALO, :] = jnp.zeros((HALO, POOL_WIDTH), F32)
        pc[0:HALO, :] = jnp.zeros((HALO, POOL_WIDTH), F32)
        pd[0:HALO, :] = jnp.zeros((HALO, POOL_WIDTH), F32)

    h = _rms(x_ref[0], g_ref[...]).astype(BF16)
    qkv_ref[0] = jnp.dot(h, w_ref[:, 0:QKV_WIDTH], preferred_element_type=F32).astype(BF16)

    cv = jnp.dot(h, w_ref[:, CONV_OFF:POOL_OFF], preferred_element_type=F32)
    h_conv = cv[:, 0:CONV_WIDTH]
    b_gate = cv[:, CONV_WIDTH:2 * CONV_WIDTH]
    c_gate = cv[:, 2 * CONV_WIDTH:3 * CONV_WIDTH]
    u = c_gate * h_conv
    cbuf[HALO:HALO + tm, :] = u
    conv = (cbuf[HALO - 2:HALO - 2 + tm, :] * cw_ref[0:1, :]
            + cbuf[HALO - 1:HALO - 1 + tm, :] * cw_ref[1:2, :]
            + u * cw_ref[2:3, :])
    cp_ref[0, :, 0:CONV_WIDTH] = (b_gate * conv).astype(BF16)
    cbuf[0:HALO, :] = cbuf[tm:tm + HALO, :]

    up = jnp.dot(h, w_ref[:, POOL_OFF:IN_WIDTH], preferred_element_type=F32)
    pa[2 * HALO:2 * HALO + tm, :] = up
    n = tm + HALO
    pb[HALO:HALO + n, :] = pa[HALO:HALO + n, :] + pa[HALO - 1:HALO - 1 + n, :]
    pc[HALO:HALO + n, :] = pb[HALO:HALO + n, :] + pb[HALO - 2:HALO - 2 + n, :]
    pd[HALO:HALO + n, :] = pc[HALO:HALO + n, :] + pc[HALO - 4:HALO - 4 + n, :]
    s2 = pb[2 * HALO:2 * HALO + tm, :]
    s4 = pc[2 * HALO:2 * HALO + tm, :]
    s8 = pd[2 * HALO:2 * HALO + tm, :]
    s16 = s8 + pd[2 * HALO - 8:2 * HALO - 8 + tm, :]
    lane = lax.broadcasted_iota(jnp.int32, (1, POOL_WIDTH), 1)
    g0 = lane < POOL_GROUP_DIM
    g1 = lane < 2 * POOL_GROUP_DIM
    g2 = lane < 3 * POOL_GROUP_DIM
    wsum = jnp.where(g0, s2, jnp.where(g1, s4, jnp.where(g2, s8, s16)))
    w2, w4, w8, w16 = (F32(w) for w in POOL_WINDOWS)
    win = jnp.where(g0, w2, jnp.where(g1, w4, jnp.where(g2, w8, w16)))
    t1 = (j * tm + 1 + lax.broadcasted_iota(jnp.int32, (tm, 1), 0)).astype(F32)
    cnt = jnp.minimum(t1, win)
    pooled = (wsum / cnt - up).astype(BF16)
    y = jnp.dot(pooled, pw_ref[...], preferred_element_type=F32) * ps_ref[...]
    cp_ref[0, :, CONV_WIDTH:CONV_WIDTH + POOL_WIDTH] = y.astype(BF16)
    pa[HALO:2 * HALO, :] = pa[tm + HALO:tm + 2 * HALO, :]


def _in_proj(x, g, w_in, conv_w, pool_bd, pool_scale):
    B, S, _ = x.shape
    tm = IN_TM
    const = lambda b, j: (0, 0)
    return pl.pallas_call(
        _in_proj_kernel,
        out_shape=(jax.ShapeDtypeStruct((B, S, QKV_WIDTH), BF16),
                   jax.ShapeDtypeStruct((B, S, CONV_WIDTH + POOL_WIDTH), BF16)),
        grid=(B, S // tm),
        in_specs=[
            pl.BlockSpec((1, tm, D_MODEL), lambda b, j: (b, j, 0)),
            pl.BlockSpec((1, D_MODEL), const),
            pl.BlockSpec((D_MODEL, IN_WIDTH), const, pipeline_mode=pl.Buffered(1)),
            pl.BlockSpec((CONV_K, CONV_WIDTH), const),
            pl.BlockSpec((POOL_WIDTH, POOL_WIDTH), const),
            pl.BlockSpec((1, POOL_WIDTH), const),
        ],
        out_specs=(pl.BlockSpec((1, tm, QKV_WIDTH), lambda b, j: (b, j, 0)),
                   pl.BlockSpec((1, tm, CONV_WIDTH + POOL_WIDTH), lambda b, j: (b, j, 0))),
        scratch_shapes=[
            pltpu.VMEM((HALO + tm, CONV_WIDTH), F32),
            pltpu.VMEM((2 * HALO + tm, POOL_WIDTH), F32),
            pltpu.VMEM((2 * HALO + tm, POOL_WIDTH), F32),
            pltpu.VMEM((2 * HALO + tm, POOL_WIDTH), F32),
            pltpu.VMEM((2 * HALO + tm, POOL_WIDTH), F32),
        ],
        compiler_params=pltpu.CompilerParams(
            dimension_semantics=("arbitrary", "arbitrary"),
            vmem_limit_bytes=VMEM_LIMIT),
        name="in_proj",
    )(x, g, w_in, conv_w, pool_bd, pool_scale)


def _attn_kernel(q_ref, k_ref, v_ref, bias_ref, o_ref, kmh, kml, m_ref, l_ref, acc_ref):
    i = pl.program_id(2)
    nb = k_ref.shape[1] // MOBA_BLOCK
    blk = MOBA_BLOCK
    neg_inf = F32(-jnp.inf)

    @pl.when(i == 0)
    def _():
        km = jnp.mean(k_ref[0].astype(F32).reshape(nb, blk, LANES), axis=1)
        km = jnp.concatenate([km, jnp.zeros((GATE_ROWS - nb, LANES), F32)], axis=0)
        hi = km.astype(BF16)
        kmh[...] = hi
        kml[...] = (km - hi.astype(F32)).astype(BF16)

    q = q_ref[0]
    lane = lax.broadcasted_iota(jnp.int32, (1, LANES), 1)
    nt = (((1,), (1,)), ((), ()))
    row = lax.broadcasted_iota(jnp.int32, (blk, blk), 0)
    col = lax.broadcasted_iota(jnp.int32, (blk, blk), 1)
    causal = col <= row
    own = pl.multiple_of(i * blk, blk)
    k_own = k_ref[0, pl.ds(own, blk), :]
    v_own = v_ref[0, pl.ds(own, blk), :]
    blk_idx = lax.broadcasted_iota(jnp.int32, (nb, blk), 0)

    sels = []
    qss = []
    for h in range(2):
        in_head = (lane >= h * HEAD_DIM) & (lane < (h + 1) * HEAD_DIM)
        qh = jnp.where(in_head, q, jnp.zeros_like(q))
        qs = qh * BF16(HEAD_DIM ** -0.5)
        qss.append(qs)

        g = (lax.dot_general(kmh[...], qh, nt, preferred_element_type=F32)
             + lax.dot_general(kml[...], qh, nt, preferred_element_type=F32))[0:nb]
        elig = blk_idx < i
        rows = []
        for jb in range(nb):
            gj = g[jb:jb + 1, :]
            beats = ((g > gj) | ((g == gj) & (blk_idx < jb))) & elig
            rank = jnp.sum(beats.astype(F32), axis=0, keepdims=True)
            chosen = (rank < MOBA_TOPK) & (jnp.full((1, blk), jb, jnp.int32) < i)
            rows.append(jnp.where(chosen, F32(1.0), F32(0.0)))
        sel_t = jnp.concatenate(rows + [jnp.zeros((LANES - nb, blk), F32)], axis=0)
        sels.append(sel_t.T)

        s = lax.dot_general(qs, k_own, nt, preferred_element_type=F32)
        s = s + bias_ref[0, pl.ds(h * nb + i, 1), :]
        s = jnp.where(causal, s, neg_inf)
        m = jnp.max(s, axis=-1, keepdims=True)
        p = jnp.exp(s - m)
        m_ref[h] = m
        l_ref[h] = jnp.sum(p, axis=-1, keepdims=True)
        acc_ref[h] = jnp.dot(p.astype(BF16), v_own, preferred_element_type=F32)

    for jb in range(nb - 1):
        @pl.when(jb < i)
        def _(jb=jb):
            k_j = k_ref[0, jb * blk:(jb + 1) * blk, :]
            v_j = v_ref[0, jb * blk:(jb + 1) * blk, :]
            for h in range(2):
                sel = sels[h][:, jb:jb + 1] > 0.5
                s = lax.dot_general(qss[h], k_j, nt, preferred_element_type=F32)
                s = s + bias_ref[0, h * nb + jb:h * nb + jb + 1, :]
                m_prev = m_ref[h]
                m_cur = jnp.where(sel, jnp.max(s, axis=-1, keepdims=True), neg_inf)
                m_new = jnp.maximum(m_prev, m_cur)
                alpha = jnp.exp(m_prev - m_new)
                p = jnp.exp(s - jnp.where(sel, m_new, -neg_inf))
                m_ref[h] = m_new
                l_ref[h] = alpha * l_ref[h] + jnp.sum(p, axis=-1, keepdims=True)
                acc_ref[h] = alpha * acc_ref[h] + jnp.dot(p.astype(BF16), v_j,
                                                          preferred_element_type=F32)

    o0 = acc_ref[0] / l_ref[0]
    o1 = acc_ref[1] / l_ref[1]
    o_ref[0] = jnp.where((lane // HEAD_DIM) == 0, o0, o1).astype(BF16)


def _moba_attn(qkv, bias):
    B, S, _ = qkv.shape
    nb = S // MOBA_BLOCK
    blk = MOBA_BLOCK
    return pl.pallas_call(
        _attn_kernel,
        out_shape=jax.ShapeDtypeStruct((B, S, ATTN_WIDTH), BF16),
        grid=(B, HEAD_PAIRS, nb),
        in_specs=[
            pl.BlockSpec((1, blk, LANES), lambda b, hp, i: (b, i, hp)),
            pl.BlockSpec((1, S, LANES), lambda b, hp, i: (b, 0, HEAD_PAIRS + hp)),
            pl.BlockSpec((1, S, LANES), lambda b, hp, i: (b, 0, 2 * HEAD_PAIRS + hp)),
            pl.BlockSpec((1, 2 * nb, blk), lambda b, hp, i: (hp, 0, 0)),
        ],
        out_specs=pl.BlockSpec((1, blk, LANES), lambda b, hp, i: (b, i, hp)),
        scratch_shapes=[
            pltpu.VMEM((GATE_ROWS, LANES), BF16),
            pltpu.VMEM((GATE_ROWS, LANES), BF16),
            pltpu.VMEM((2, blk, 1), F32),
            pltpu.VMEM((2, blk, 1), F32),
            pltpu.VMEM((2, blk, LANES), F32),
        ],
        compiler_params=pltpu.CompilerParams(
            dimension_semantics=("arbitrary", "arbitrary", "arbitrary"),
            vmem_limit_bytes=VMEM_LIMIT),
        name="moba_attn",
    )(qkv, qkv, qkv, bias)


def _alibi_bias(S):
    nb = S // MOBA_BLOCK
    slopes = 2.0 ** (-8.0 * jnp.arange(1, ATTN_HEADS + 1, dtype=F32) / ATTN_HEADS)
    pos = jnp.arange(S, dtype=F32).reshape(nb, MOBA_BLOCK)
    bias = slopes[:, None, None] * pos[None]
    return bias.reshape(HEAD_PAIRS, 2 * nb, MOBA_BLOCK)


def _out_ffn_kernel(x_ref, a_ref, cp_ref, wo_ref, gpm_ref, gpf_ref, gqf_ref,
                    wg_ref, wu_ref, wd_ref, o_ref):
    mixed = (jnp.dot(a_ref[...], wo_ref[0:ATTN_WIDTH, :], preferred_element_type=F32)
             + jnp.dot(cp_ref[...], wo_ref[ATTN_WIDTH:D_MODEL, :], preferred_element_type=F32))
    x1 = x_ref[...] + _rms(mixed, gpm_ref[...])
    hf = _rms(x1, gpf_ref[...]).astype(BF16)
    ff = jnp.zeros(x1.shape, F32)
    for c in range(0, D_FF, FFN_CHUNK):
        gate = jnp.dot(hf, wg_ref[:, c:c + FFN_CHUNK], preferred_element_type=F32)
        up = jnp.dot(hf, wu_ref[:, c:c + FFN_CHUNK], preferred_element_type=F32)
        act = (gate * jax.nn.sigmoid(gate) * up).astype(BF16)
        ff = ff + jnp.dot(act, wd_ref[c:c + FFN_CHUNK, :], preferred_element_type=F32)
    o_ref[...] = x1 + _rms(ff, gqf_ref[...])


def _out_ffn(x, attn, cp, w_out, g_post_mix, g_pre_ffn, g_post_ffn, w_gate, w_up, w_down):
    T = x.shape[0]
    tm = FFN_TM
    const = lambda t: (0, 0)
    resident = functools.partial(pl.BlockSpec, index_map=const, pipeline_mode=pl.Buffered(1))
    return pl.pallas_call(
        _out_ffn_kernel,
        out_shape=jax.ShapeDtypeStruct((T, D_MODEL), F32),
        grid=(T // tm,),
        in_specs=[
            pl.BlockSpec((tm, D_MODEL), lambda t: (t, 0)),
            pl.BlockSpec((tm, ATTN_WIDTH), lambda t: (t, 0)),
            pl.BlockSpec((tm, CONV_WIDTH + POOL_WIDTH), lambda t: (t, 0)),
            resident((D_MODEL, D_MODEL)),
            pl.BlockSpec((1, D_MODEL), const),
            pl.BlockSpec((1, D_MODEL), const),
            pl.BlockSpec((1, D_MODEL), const),
            resident((D_MODEL, D_FF)),
            resident((D_MODEL, D_FF)),
            resident((D_FF, D_MODEL)),
        ],
        out_specs=pl.BlockSpec((tm, D_MODEL), lambda t: (t, 0)),
        compiler_params=pltpu.CompilerParams(
            dimension_semantics=("arbitrary",),
            vmem_limit_bytes=VMEM_LIMIT),
        name="out_ffn",
    )(x, attn, cp, w_out, g_post_mix, g_pre_ffn, g_post_ffn, w_gate, w_up, w_down)


def _block_diag(pool_w):
    G, C, _ = pool_w.shape
    eye = jnp.eye(G, dtype=pool_w.dtype)
    return (eye[:, None, :, None] * pool_w[:, :, None, :]).reshape(G * C, G * C)


def kernel(x, w_in, w_out, conv_w, pool_w, pool_scale, g_pre_mix, g_post_mix, g_pre_ffn,
           g_post_ffn, w_gate, w_up, w_down):
    B, S, D = x.shape
    depth = w_in.shape[0]
    assert D == D_MODEL and S % MOBA_BLOCK == 0 and S % IN_TM == 0 and (B * S) % FFN_TM == 0
    bias = _alibi_bias(S)
    row = lambda a: a.reshape(1, -1)
    for l in range(depth):
        qkv, cp = _in_proj(x, row(g_pre_mix[l]), w_in[l].astype(BF16), conv_w[l],
                           _block_diag(pool_w[l]).astype(BF16), row(pool_scale[l]))
        attn = _moba_attn(qkv, bias)
        x = _out_ffn(x.reshape(B * S, D), attn.reshape(B * S, ATTN_WIDTH),
                     cp.reshape(B * S, CONV_WIDTH + POOL_WIDTH), w_out[l].astype(BF16),
                     row(g_post_mix[l]), row(g_pre_ffn[l]), row(g_post_ffn[l]),
                     w_gate[l].astype(BF16), w_up[l].astype(BF16),
                     w_down[l].astype(BF16)).reshape(B, S, D)
    return x
```

```python
import functools

import jax
import jax.numpy as jnp
import numpy as np
from jax import lax
from jax.experimental import pallas as pl
from jax.experimental.pallas import tpu as pltpu

D_MODEL = 1024
HEAD_DIM = 64
ATTN_WIDTH = 512
ATTN_HEADS = 8
CONV_WIDTH = 256
CONV_K = 3
POOL_WIDTH = 256
POOL_WINDOWS = (2, 4, 8, 16)
POOL_GROUP_DIM = 64
IN_WIDTH = 2560
MOBA_BLOCK = 256
MOBA_TOPK = 3
D_FF = 2816
NORM_EPS = 1e-6

LANES = 128
HEAD_PAIRS = ATTN_WIDTH // LANES
HALO = 16
GATE_ROWS = 16
QKV_WIDTH = 3 * ATTN_WIDTH
CONV_OFF = QKV_WIDTH
POOL_OFF = CONV_OFF + 3 * CONV_WIDTH

IN_TM = 512
FFN_TM = 512
FFN_CHUNK = 1408
VMEM_LIMIT = 56 * 1024 * 1024

BF16 = jnp.bfloat16
F32 = jnp.float32


def _rms(x, g):
    return x * lax.rsqrt(jnp.mean(x * x, axis=-1, keepdims=True) + NORM_EPS) * g


def _in_proj_kernel(x_ref, g_ref, w_ref, cw_ref, pw_ref, ps_ref, qkv_ref, cp_ref,
                    cbuf, pa, pb, pc, pd):
    j = pl.program_id(1)
    tm = IN_TM

    @pl.when(j == 0)
    def _():
        cbuf[0:HALO, :] = jnp.zeros((HALO, CONV_WIDTH), F32)
        pa[0:2 * HALO, :] = jnp.zeros((2 * HALO, POOL_WIDTH), F32)
        pb[0:HALO, :] = jnp.zeros((HALO, POOL_WIDTH), F32)
        pc[0:HALO, :] = jnp.zeros((HALO, POOL_WIDTH), F32)
        pd[0:HALO, :] = jnp.zeros((HALO, POOL_WIDTH), F32)

    h = _rms(x_ref[0], g_ref[...]).astype(BF16)
    qkv_ref[0] = jnp.dot(h, w_ref[:, 0:QKV_WIDTH], preferred_element_type=F32).astype(BF16)

    cv = jnp.dot(h, w_ref[:, CONV_OFF:POOL_OFF], preferred_element_type=F32)
    h_conv = cv[:, 0:CONV_WIDTH]
    b_gate = cv[:, CONV_WIDTH:2 * CONV_WIDTH]
    c_gate = cv[:, 2 * CONV_WIDTH:3 * CONV_WIDTH]
    u = c_gate * h_conv
    cbuf[HALO:HALO + tm, :] = u
    conv = (cbuf[HALO - 2:HALO - 2 + tm, :] * cw_ref[0:1, :]
            + cbuf[HALO - 1:HALO - 1 + tm, :] * cw_ref[1:2, :]
            + u * cw_ref[2:3, :])
    cp_ref[0, :, 0:CONV_WIDTH] = (b_gate * conv).astype(BF16)
    cbuf[0:HALO, :] = cbuf[tm:tm + HALO, :]

    up = jnp.dot(h, w_ref[:, POOL_OFF:IN_WIDTH], preferred_element_type=F32)
    pa[2 * HALO:2 * HALO + tm, :] = up
    n = tm + HALO
    pb[HALO:HALO + n, :] = pa[HALO:HALO + n, :] + pa[HALO - 1:HALO - 1 + n, :]
    pc[HALO:HALO + n, :] = pb[HALO:HALO + n, :] + pb[HALO - 2:HALO - 2 + n, :]
    pd[HALO:HALO + n, :] = pc[HALO:HALO + n, :] + pc[HALO - 4:HALO - 4 + n, :]
    s2 = pb[2 * HALO:2 * HALO + tm, :]
    s4 = pc[2 * HALO:2 * HALO + tm, :]
    s8 = pd[2 * HALO:2 * HALO + tm, :]
    s16 = s8 + pd[2 * HALO - 8:2 * HALO - 8 + tm, :]
    lane = lax.broadcasted_iota(jnp.int32, (1, POOL_WIDTH), 1)
    g0 = lane < POOL_GROUP_DIM
    g1 = lane < 2 * POOL_GROUP_DIM
    g2 = lane < 3 * POOL_GROUP_DIM
    wsum = jnp.where(g0, s2, jnp.where(g1, s4, jnp.where(g2, s8, s16)))
    w2, w4, w8, w16 = (F32(w) for w in POOL_WINDOWS)
    win = jnp.where(g0, w2, jnp.where(g1, w4, jnp.where(g2, w8, w16)))
    t1 = (j * tm + 1 + lax.broadcasted_iota(jnp.int32, (tm, 1), 0)).astype(F32)
    cnt = jnp.minimum(t1, win)
    pooled = (wsum / cnt - up).astype(BF16)
    y = jnp.dot(pooled, pw_ref[...], preferred_element_type=F32) * ps_ref[...]
    cp_ref[0, :, CONV_WIDTH:CONV_WIDTH + POOL_WIDTH] = y.astype(BF16)
    pa[HALO:2 * HALO, :] = pa[tm + HALO:tm + 2 * HALO, :]


def _in_proj(x, g, w_in, conv_w, pool_bd, pool_scale):
    B, S, _ = x.shape
    tm = IN_TM
    const = lambda b, j: (0, 0)
    return pl.pallas_call(
        _in_proj_kernel,
        out_shape=(jax.ShapeDtypeStruct((B, S, QKV_WIDTH), BF16),
                   jax.ShapeDtypeStruct((B, S, CONV_WIDTH + POOL_WIDTH), BF16)),
        grid=(B, S // tm),
        in_specs=[
            pl.BlockSpec((1, tm, D_MODEL), lambda b, j: (b, j, 0)),
            pl.BlockSpec((1, D_MODEL), const),
            pl.BlockSpec((D_MODEL, IN_WIDTH), const, pipeline_mode=pl.Buffered(1)),
            pl.BlockSpec((CONV_K, CONV_WIDTH), const),
            pl.BlockSpec((POOL_WIDTH, POOL_WIDTH), const),
            pl.BlockSpec((1, POOL_WIDTH), const),
        ],
        out_specs=(pl.BlockSpec((1, tm, QKV_WIDTH), lambda b, j: (b, j, 0)),
                   pl.BlockSpec((1, tm, CONV_WIDTH + POOL_WIDTH), lambda b, j: (b, j, 0))),
        scratch_shapes=[
            pltpu.VMEM((HALO + tm, CONV_WIDTH), F32),
            pltpu.VMEM((2 * HALO + tm, POOL_WIDTH), F32),
            pltpu.VMEM((2 * HALO + tm, POOL_WIDTH), F32),
            pltpu.VMEM((2 * HALO + tm, POOL_WIDTH), F32),
            pltpu.VMEM((2 * HALO + tm, POOL_WIDTH), F32),
        ],
        compiler_params=pltpu.CompilerParams(
            dimension_semantics=("arbitrary", "arbitrary"),
            vmem_limit_bytes=VMEM_LIMIT),
        name="in_proj",
    )(x, g, w_in, conv_w, pool_bd, pool_scale)


NEG_BIG = -1e30
AUX_ALIBI_HI = 0
AUX_ALIBI_LO = 1
AUX_BLOCK0 = 8
POS_SPLIT = 16


def _attn_kernel(q_ref, k_ref, v_ref, qx_ref, kx_ref, o_ref, qa_ref, ka_ref):
    S = k_ref.shape[1]
    blk = MOBA_BLOCK
    nb = S // blk
    q = q_ref[0]
    k = k_ref[0]
    lane = lax.broadcasted_iota(jnp.int32, (1, LANES), 1)
    nt = (((1,), (1,)), ((), ()))

    km = jnp.mean(k.astype(F32).reshape(nb, blk, LANES), axis=1)
    km = jnp.concatenate([km, jnp.zeros((GATE_ROWS - nb, LANES), F32)], axis=0)
    km_hi = km.astype(BF16)
    km_lo = (km - km_hi.astype(F32)).astype(BF16)

    blk_idx = lax.broadcasted_iota(jnp.int32, (nb, S), 0)
    q_blk = lax.broadcasted_iota(jnp.int32, (nb, S), 1) // blk
    elig = blk_idx < q_blk
    own = blk_idx == q_blk

    for h in range(2):
        in_head = (lane >= h * HEAD_DIM) & (lane < (h + 1) * HEAD_DIM)
        qh = jnp.where(in_head, q, jnp.zeros_like(q))

        g = (lax.dot_general(km_hi, qh, nt, preferred_element_type=F32)
             + lax.dot_general(km_lo, qh, nt, preferred_element_type=F32))[0:nb]
        ranks = []
        for jb in range(nb):
            gj = g[jb:jb + 1, :]
            beats = ((g > gj) | ((g == gj) & (blk_idx < jb))) & elig
            ranks.append(jnp.sum(beats.astype(F32), axis=0, keepdims=True))
        rank = jnp.concatenate(ranks, axis=0)
        attend = ((rank < MOBA_TOPK) & elig) | own
        mask_t = jnp.where(attend, F32(0.0), F32(NEG_BIG))

        aux = (1 - h) * HEAD_DIM + AUX_BLOCK0
        mask_t = jnp.concatenate([jnp.zeros((aux, S), F32), mask_t,
                                  jnp.zeros((LANES - aux - nb, S), F32)], axis=0)
        extra = mask_t.T + qx_ref[0, h:h + 1, :]
        qa_ref[h] = jnp.where(in_head, qh * BF16(HEAD_DIM ** -0.5), extra.astype(BF16))
        ka_ref[h] = jnp.where(in_head, k, kx_ref[h])

    row = lax.broadcasted_iota(jnp.int32, (blk, blk), 0)
    col = lax.broadcasted_iota(jnp.int32, (blk, blk), 1)
    causal = col <= row
    for c in range(nb):
        lo, hi = c * blk, (c + 1) * blk
        outs = []
        for h in range(2):
            s = lax.dot_general(qa_ref[h, lo:hi, :], ka_ref[h, 0:hi, :], nt,
                                preferred_element_type=F32)
            s_own = jnp.where(causal, s[:, lo:hi], F32(NEG_BIG))
            m = jnp.max(s_own, axis=-1, keepdims=True)
            if c:
                s_past = s[:, 0:lo]
                m = jnp.maximum(m, jnp.max(s_past, axis=-1, keepdims=True))
                p_past = jnp.exp(s_past - m)
                p_own = jnp.exp(s_own - m)
                l = jnp.sum(p_past, axis=-1, keepdims=True) + jnp.sum(p_own, axis=-1, keepdims=True)
                p = jnp.concatenate([p_past, p_own], axis=1)
            else:
                p = jnp.exp(s_own - m)
                l = jnp.sum(p, axis=-1, keepdims=True)
            acc = jnp.dot(p.astype(BF16), v_ref[0, 0:hi, :], preferred_element_type=F32)
            outs.append(acc / l)
        o_ref[0, lo:hi, :] = jnp.where(lane < HEAD_DIM, outs[0], outs[1]).astype(BF16)


def _moba_attn(qkv, qx, kx):
    B, S, _ = qkv.shape
    return pl.pallas_call(
        _attn_kernel,
        out_shape=jax.ShapeDtypeStruct((B, S, ATTN_WIDTH), BF16),
        grid=(B, HEAD_PAIRS),
        in_specs=[
            pl.BlockSpec((1, S, LANES), lambda b, hp: (b, 0, hp)),
            pl.BlockSpec((1, S, LANES), lambda b, hp: (b, 0, HEAD_PAIRS + hp)),
            pl.BlockSpec((1, S, LANES), lambda b, hp: (b, 0, 2 * HEAD_PAIRS + hp)),
            pl.BlockSpec((1, 2, LANES), lambda b, hp: (hp, 0, 0)),
            pl.BlockSpec((2, S, LANES), lambda b, hp: (0, 0, 0)),
        ],
        out_specs=pl.BlockSpec((1, S, LANES), lambda b, hp: (b, 0, hp)),
        scratch_shapes=[
            pltpu.VMEM((2, S, LANES), BF16),
            pltpu.VMEM((2, S, LANES), BF16),
        ],
        compiler_params=pltpu.CompilerParams(
            dimension_semantics=("arbitrary", "arbitrary"),
            vmem_limit_bytes=VMEM_LIMIT),
        name="moba_attn",
    )(qkv, qkv, qkv, qx, kx)


def _attn_aux(S):
    nb = S // MOBA_BLOCK
    assert S // POS_SPLIT <= 128 and AUX_BLOCK0 + nb <= HEAD_DIM
    log2_slopes = -8.0 * np.arange(1, ATTN_HEADS + 1) / ATTN_HEADS
    assert np.all(log2_slopes == np.round(log2_slopes)), "ALiBi slopes must be powers of two (exact in bf16)"
    slopes = 2.0 ** log2_slopes
    pos = np.arange(S)
    qx = np.zeros((ATTN_HEADS, LANES), np.float32)
    kx = np.zeros((2, S, LANES), np.float32)
    for h in range(2):
        aux = (1 - h) * HEAD_DIM
        qx[h::2, aux + AUX_ALIBI_HI] = slopes[h::2] * POS_SPLIT
        qx[h::2, aux + AUX_ALIBI_LO] = slopes[h::2]
        kx[h, :, aux + AUX_ALIBI_HI] = pos // POS_SPLIT
        kx[h, :, aux + AUX_ALIBI_LO] = pos % POS_SPLIT
        kx[h, pos, aux + AUX_BLOCK0 + pos // MOBA_BLOCK] = 1.0
    return jnp.asarray(qx.reshape(HEAD_PAIRS, 2, LANES)), jnp.asarray(kx, dtype=BF16)


def _out_ffn_kernel(x_ref, a_ref, cp_ref, wo_ref, gpm_ref, gpf_ref, gqf_ref,
                    wg_ref, wu_ref, wd_ref, o_ref):
    mixed = (jnp.dot(a_ref[...], wo_ref[0:ATTN_WIDTH, :], preferred_element_type=F32)
             + jnp.dot(cp_ref[...], wo_ref[ATTN_WIDTH:D_MODEL, :], preferred_element_type=F32))
    x1 = x_ref[...] + _rms(mixed, gpm_ref[...])
    hf = _rms(x1, gpf_ref[...]).astype(BF16)
    ff = jnp.zeros(x1.shape, F32)
    for c in range(0, D_FF, FFN_CHUNK):
        gate = jnp.dot(hf, wg_ref[:, c:c + FFN_CHUNK], preferred_element_type=F32)
        up = jnp.dot(hf, wu_ref[:, c:c + FFN_CHUNK], preferred_element_type=F32)
        act = (gate * jax.nn.sigmoid(gate) * up).astype(BF16)
        ff = ff + jnp.dot(act, wd_ref[c:c + FFN_CHUNK, :], preferred_element_type=F32)
    o_ref[...] = x1 + _rms(ff, gqf_ref[...])


def _out_ffn(x, attn, cp, w_out, g_post_mix, g_pre_ffn, g_post_ffn, w_gate, w_up, w_down):
    T = x.shape[0]
    tm = FFN_TM
    const = lambda t: (0, 0)
    resident = functools.partial(pl.BlockSpec, index_map=const, pipeline_mode=pl.Buffered(1))
    return pl.pallas_call(
        _out_ffn_kernel,
        out_shape=jax.ShapeDtypeStruct((T, D_MODEL), F32),
        grid=(T // tm,),
        in_specs=[
            pl.BlockSpec((tm, D_MODEL), lambda t: (t, 0)),
            pl.BlockSpec((tm, ATTN_WIDTH), lambda t: (t, 0)),
            pl.BlockSpec((tm, CONV_WIDTH + POOL_WIDTH), lambda t: (t, 0)),
            resident((D_MODEL, D_MODEL)),
            pl.BlockSpec((1, D_MODEL), const),
            pl.BlockSpec((1, D_MODEL), const),
            pl.BlockSpec((1, D_MODEL), const),
            resident((D_MODEL, D_FF)),
            resident((D_MODEL, D_FF)),
            resident((D_FF, D_MODEL)),
        ],
        out_specs=pl.BlockSpec((tm, D_MODEL), lambda t: (t, 0)),
        compiler_params=pltpu.CompilerParams(
            dimension_semantics=("arbitrary",),
            vmem_limit_bytes=VMEM_LIMIT),
        name="out_ffn",
    )(x, attn, cp, w_out, g_post_mix, g_pre_ffn, g_post_ffn, w_gate, w_up, w_down)


def _block_diag(pool_w):
    G, C, _ = pool_w.shape
    eye = jnp.eye(G, dtype=pool_w.dtype)
    return (eye[:, None, :, None] * pool_w[:, :, None, :]).reshape(G * C, G * C)


def kernel(x, w_in, w_out, conv_w, pool_w, pool_scale, g_pre_mix, g_post_mix, g_pre_ffn,
           g_post_ffn, w_gate, w_up, w_down):
    B, S, D = x.shape
    depth = w_in.shape[0]
    assert D == D_MODEL and S % MOBA_BLOCK == 0 and S % IN_TM == 0 and (B * S) % FFN_TM == 0
    qx, kx = _attn_aux(S)
    row = lambda a: a.reshape(1, -1)
    for l in range(depth):
        qkv, cp = _in_proj(x, row(g_pre_mix[l]), w_in[l].astype(BF16), conv_w[l],
                           _block_diag(pool_w[l]).astype(BF16), row(pool_scale[l]))
        attn = _moba_attn(qkv, qx, kx)
        x = _out_ffn(x.reshape(B * S, D), attn.reshape(B * S, ATTN_WIDTH),
                     cp.reshape(B * S, CONV_WIDTH + POOL_WIDTH), w_out[l].astype(BF16),
                     row(g_post_mix[l]), row(g_pre_ffn[l]), row(g_post_ffn[l]),
                     w_gate[l].astype(BF16), w_up[l].astype(BF16),
                     w_down[l].astype(BF16)).reshape(B, S, D)
    return x
```

```python
import functools

import jax
import jax.numpy as jnp
import numpy as np
from jax import lax
from jax.experimental import pallas as pl
from jax.experimental.pallas import tpu as pltpu

D_MODEL = 1024
HEAD_DIM = 64
ATTN_WIDTH = 512
ATTN_HEADS = 8
CONV_WIDTH = 256
CONV_K = 3
POOL_WIDTH = 256
POOL_WINDOWS = (2, 4, 8, 16)
POOL_GROUP_DIM = 64
IN_WIDTH = 2560
MOBA_BLOCK = 256
MOBA_TOPK = 3
D_FF = 2816
NORM_EPS = 1e-6

LANES = 128
HEAD_PAIRS = ATTN_WIDTH // LANES
HALO = 16
GATE_ROWS = 16
QKV_WIDTH = 3 * ATTN_WIDTH
CONV_OFF = QKV_WIDTH
POOL_OFF = CONV_OFF + 3 * CONV_WIDTH

IN_TM = 512
FFN_TM = 512
FFN_CHUNK = 1408
VMEM_LIMIT = 56 * 1024 * 1024

BF16 = jnp.bfloat16
F32 = jnp.float32

LOG2E = float(np.log2(np.e))
Q_SCALE = HEAD_DIM ** -0.5 * LOG2E


def _rms(x, g):
    return x * lax.rsqrt(jnp.mean(x * x, axis=-1, keepdims=True) + NORM_EPS) * g


def _in_proj_kernel(x_ref, g_ref, w_ref, cw_ref, pw_ref, ps_ref, qkv_ref, cp_ref,
                    cbuf, pa, pb, pc, pd):
    j = pl.program_id(1)
    tm = IN_TM

    @pl.when(j == 0)
    def _():
        cbuf[0:HALO, :] = jnp.zeros((HALO, CONV_WIDTH), F32)
        pa[0:2 * HALO, :] = jnp.zeros((2 * HALO, POOL_WIDTH), F32)
        pb[0:HALO, :] = jnp.zeros((HALO, POOL_WIDTH), F32)
        pc[0:HALO, :] = jnp.zeros((HALO, POOL_WIDTH), F32)
        pd[0:HALO, :] = jnp.zeros((HALO, POOL_WIDTH), F32)

    h = _rms(x_ref[0], g_ref[...]).astype(BF16)
    q = jnp.dot(h, w_ref[:, 0:ATTN_WIDTH], preferred_element_type=F32) * F32(Q_SCALE)
    qkv_ref[0, :, 0:ATTN_WIDTH] = q.astype(BF16)
    qkv_ref[0, :, ATTN_WIDTH:QKV_WIDTH] = jnp.dot(
        h, w_ref[:, ATTN_WIDTH:QKV_WIDTH], preferred_element_type=F32).astype(BF16)

    cv = jnp.dot(h, w_ref[:, CONV_OFF:POOL_OFF], preferred_element_type=F32)
    h_conv = cv[:, 0:CONV_WIDTH]
    b_gate = cv[:, CONV_WIDTH:2 * CONV_WIDTH]
    c_gate = cv[:, 2 * CONV_WIDTH:3 * CONV_WIDTH]
    u = c_gate * h_conv
    cbuf[HALO:HALO + tm, :] = u
    conv = (cbuf[HALO - 2:HALO - 2 + tm, :] * cw_ref[0:1, :]
            + cbuf[HALO - 1:HALO - 1 + tm, :] * cw_ref[1:2, :]
            + u * cw_ref[2:3, :])
    cp_ref[0, :, 0:CONV_WIDTH] = (b_gate * conv).astype(BF16)
    cbuf[0:HALO, :] = cbuf[tm:tm + HALO, :]

    up = jnp.dot(h, w_ref[:, POOL_OFF:IN_WIDTH], preferred_element_type=F32)
    pa[2 * HALO:2 * HALO + tm, :] = up
    n = tm + HALO
    pb[HALO:HALO + n, :] = pa[HALO:HALO + n, :] + pa[HALO - 1:HALO - 1 + n, :]
    pc[HALO:HALO + n, :] = pb[HALO:HALO + n, :] + pb[HALO - 2:HALO - 2 + n, :]
    pd[HALO:HALO + n, :] = pc[HALO:HALO + n, :] + pc[HALO - 4:HALO - 4 + n, :]
    s2 = pb[2 * HALO:2 * HALO + tm, :]
    s4 = pc[2 * HALO:2 * HALO + tm, :]
    s8 = pd[2 * HALO:2 * HALO + tm, :]
    s16 = s8 + pd[2 * HALO - 8:2 * HALO - 8 + tm, :]
    lane = lax.broadcasted_iota(jnp.int32, (1, POOL_WIDTH), 1)
    g0 = lane < POOL_GROUP_DIM
    g1 = lane < 2 * POOL_GROUP_DIM
    g2 = lane < 3 * POOL_GROUP_DIM
    wsum = jnp.where(g0, s2, jnp.where(g1, s4, jnp.where(g2, s8, s16)))
    w2, w4, w8, w16 = (F32(w) for w in POOL_WINDOWS)
    win = jnp.where(g0, w2, jnp.where(g1, w4, jnp.where(g2, w8, w16)))
    t1 = (j * tm + 1 + lax.broadcasted_iota(jnp.int32, (tm, 1), 0)).astype(F32)
    cnt = jnp.minimum(t1, win)
    pooled = (wsum / cnt - up).astype(BF16)
    y = jnp.dot(pooled, pw_ref[...], preferred_element_type=F32) * ps_ref[...]
    cp_ref[0, :, CONV_WIDTH:CONV_WIDTH + POOL_WIDTH] = y.astype(BF16)
    pa[HALO:2 * HALO, :] = pa[tm + HALO:tm + 2 * HALO, :]


def _in_proj(x, g, w_in, conv_w, pool_bd, pool_scale):
    B, S, _ = x.shape
    tm = IN_TM
    const = lambda b, j: (0, 0)
    return pl.pallas_call(
        _in_proj_kernel,
        out_shape=(jax.ShapeDtypeStruct((B, S, QKV_WIDTH), BF16),
                   jax.ShapeDtypeStruct((B, S, CONV_WIDTH + POOL_WIDTH), BF16)),
        grid=(B, S // tm),
        in_specs=[
            pl.BlockSpec((1, tm, D_MODEL), lambda b, j: (b, j, 0)),
            pl.BlockSpec((1, D_MODEL), const),
            pl.BlockSpec((D_MODEL, IN_WIDTH), const, pipeline_mode=pl.Buffered(1)),
            pl.BlockSpec((CONV_K, CONV_WIDTH), const),
            pl.BlockSpec((POOL_WIDTH, POOL_WIDTH), const),
            pl.BlockSpec((1, POOL_WIDTH), const),
        ],
        out_specs=(pl.BlockSpec((1, tm, QKV_WIDTH), lambda b, j: (b, j, 0)),
                   pl.BlockSpec((1, tm, CONV_WIDTH + POOL_WIDTH), lambda b, j: (b, j, 0))),
        scratch_shapes=[
            pltpu.VMEM((HALO + tm, CONV_WIDTH), F32),
            pltpu.VMEM((2 * HALO + tm, POOL_WIDTH), F32),
            pltpu.VMEM((2 * HALO + tm, POOL_WIDTH), F32),
            pltpu.VMEM((2 * HALO + tm, POOL_WIDTH), F32),
            pltpu.VMEM((2 * HALO + tm, POOL_WIDTH), F32),
        ],
        compiler_params=pltpu.CompilerParams(
            dimension_semantics=("arbitrary", "arbitrary"),
            vmem_limit_bytes=VMEM_LIMIT),
        name="in_proj",
    )(x, g, w_in, conv_w, pool_bd, pool_scale)


NEG_BIG = -1e30
ALIBI_PARTS = 3
AUX_ALIBI_HI = 0
AUX_ALIBI_LO = ALIBI_PARTS
AUX_BLOCK0 = 8
AUX_ONES = 0
POS_SPLIT = 16


def _block_mask_t(g, q_blk):
    nb = g.shape[0]
    blk_idx = lax.broadcasted_iota(jnp.int32, g.shape, 0)
    elig = blk_idx < q_blk
    ranks = []
    for jb in range(nb):
        gj = g[jb:jb + 1, :]
        beats = ((g > gj) | ((g == gj) & (blk_idx < jb))) & elig
        ranks.append(jnp.sum(beats.astype(F32), axis=0, keepdims=True))
    rank = jnp.concatenate(ranks, axis=0)
    attend = ((rank < MOBA_TOPK) & elig) | (blk_idx == q_blk)
    return jnp.where(attend, F32(0.0), F32(NEG_BIG))


def _attn_kernel(q_ref, k_ref, v_ref, qx_ref, kx_ref, o_ref, qa_ref, ka_ref, va_ref):
    S = k_ref.shape[1]
    blk = MOBA_BLOCK
    nb = S // blk
    k = k_ref[0]
    v = v_ref[0]
    lane = lax.broadcasted_iota(jnp.int32, (1, LANES), 1)
    nt = (((1,), (1,)), ((), ()))

    km = jnp.mean(k.astype(F32).reshape(nb, blk, LANES), axis=1)
    km = jnp.concatenate([km, jnp.zeros((GATE_ROWS - nb, LANES), F32)], axis=0)
    km_hi = km.astype(BF16)
    km_lo = (km - km_hi.astype(F32)).astype(BF16)

    gated = min((MOBA_TOPK + 1) * blk, S)
    q_free = q_ref[0, 0:gated, :]
    q_gate = q_ref[0, gated:S, :]
    q_blk = (gated + lax.broadcasted_iota(jnp.int32, (nb, S - gated), 1)) // blk
    for h in range(2):
        in_head = (lane >= h * HEAD_DIM) & (lane < (h + 1) * HEAD_DIM)
        alibi = qx_ref[0, h:h + 1, :]
        qa_ref[h, 0:gated, :] = jnp.where(
            in_head, q_free, jnp.broadcast_to(alibi.astype(BF16), q_free.shape))
        if gated < S:
            qh = jnp.where(in_head, q_gate, jnp.zeros_like(q_gate))
            g = (lax.dot_general(km_hi, qh, nt, preferred_element_type=F32)
                 + lax.dot_general(km_lo, qh, nt, preferred_element_type=F32))[0:nb]
            aux = (1 - h) * HEAD_DIM + AUX_BLOCK0
            mask_t = jnp.concatenate([jnp.zeros((aux, S - gated), F32), _block_mask_t(g, q_blk),
                                      jnp.zeros((LANES - aux - nb, S - gated), F32)], axis=0)
            extra = mask_t.T + alibi
            qa_ref[h, gated:S, :] = jnp.where(in_head, q_gate, extra.astype(BF16))
        ka_ref[h] = jnp.where(in_head, k, kx_ref[h])
        ones_lane = jnp.where(lane == (1 - h) * HEAD_DIM + AUX_ONES, F32(1.0), F32(0.0))
        va_ref[h] = jnp.where(in_head, v, jnp.broadcast_to(ones_lane.astype(BF16), v.shape))

    row = lax.broadcasted_iota(jnp.int32, (blk, blk), 0)
    col = lax.broadcasted_iota(jnp.int32, (blk, blk), 1)
    causal = col <= row
    for c in range(nb):
        lo, hi = c * blk, (c + 1) * blk
        outs = []
        for h in range(2):
            s = lax.dot_general(qa_ref[h, lo:hi, :], ka_ref[h, 0:hi, :], nt,
                                preferred_element_type=F32)
            s_own = jnp.where(causal, s[:, lo:hi], F32(NEG_BIG))
            m = jnp.max(s_own, axis=-1, keepdims=True)
            if c:
                s_past = s[:, 0:lo]
                m = jnp.maximum(m, jnp.max(s_past, axis=-1, keepdims=True))
                p = jnp.concatenate([jnp.exp2(s_past - m), jnp.exp2(s_own - m)], axis=1)
            else:
                p = jnp.exp2(s_own - m)
            acc = jnp.dot(p.astype(BF16), va_ref[h, 0:hi, :], preferred_element_type=F32)
            ones = (1 - h) * HEAD_DIM + AUX_ONES
            outs.append(acc * (1.0 / acc[:, ones:ones + 1]))
        o_ref[0, lo:hi, :] = jnp.where(lane < HEAD_DIM, outs[0], outs[1]).astype(BF16)


def _moba_attn(qkv, qx, kx):
    B, S, _ = qkv.shape
    return pl.pallas_call(
        _attn_kernel,
        out_shape=jax.ShapeDtypeStruct((B, S, ATTN_WIDTH), BF16),
        grid=(B, HEAD_PAIRS),
        in_specs=[
            pl.BlockSpec((1, S, LANES), lambda b, hp: (b, 0, hp)),
            pl.BlockSpec((1, S, LANES), lambda b, hp: (b, 0, HEAD_PAIRS + hp)),
            pl.BlockSpec((1, S, LANES), lambda b, hp: (b, 0, 2 * HEAD_PAIRS + hp)),
            pl.BlockSpec((1, 2, LANES), lambda b, hp: (hp, 0, 0)),
            pl.BlockSpec((2, S, LANES), lambda b, hp: (0, 0, 0)),
        ],
        out_specs=pl.BlockSpec((1, S, LANES), lambda b, hp: (b, 0, hp)),
        scratch_shapes=[
            pltpu.VMEM((2, S, LANES), BF16),
            pltpu.VMEM((2, S, LANES), BF16),
            pltpu.VMEM((2, S, LANES), BF16),
        ],
        compiler_params=pltpu.CompilerParams(
            dimension_semantics=("arbitrary", "arbitrary"),
            vmem_limit_bytes=VMEM_LIMIT),
        name="moba_attn",
    )(qkv, qkv, qkv, qx, kx)


def _attn_aux(S):
    nb = S // MOBA_BLOCK
    assert S // POS_SPLIT <= 128 and 2 * ALIBI_PARTS <= AUX_BLOCK0 and AUX_BLOCK0 + nb <= HEAD_DIM
    slopes = 2.0 ** (-8.0 * np.arange(1, ATTN_HEADS + 1) / ATTN_HEADS)
    rest = (slopes * LOG2E).astype(np.float32)
    pos = np.arange(S)
    qx = np.zeros((ATTN_HEADS, LANES), np.float32)
    kx = np.zeros((2, S, LANES), np.float32)
    for i in range(ALIBI_PARTS):
        part = rest.astype(BF16).astype(np.float32)
        rest = rest - part
        for h in range(2):
            aux = (1 - h) * HEAD_DIM
            qx[h::2, aux + AUX_ALIBI_HI + i] = part[h::2] * POS_SPLIT
            qx[h::2, aux + AUX_ALIBI_LO + i] = part[h::2]
            kx[h, :, aux + AUX_ALIBI_HI + i] = pos // POS_SPLIT
            kx[h, :, aux + AUX_ALIBI_LO + i] = pos % POS_SPLIT
    for h in range(2):
        aux = (1 - h) * HEAD_DIM
        kx[h, pos, aux + AUX_BLOCK0 + pos // MOBA_BLOCK] = 1.0
    return jnp.asarray(qx.reshape(HEAD_PAIRS, 2, LANES)), jnp.asarray(kx, dtype=BF16)


def _out_ffn_kernel(x_ref, a_ref, cp_ref, wo_ref, gpm_ref, gpf_ref, gqf_ref,
                    wg_ref, wu_ref, wd_ref, o_ref):
    mixed = (jnp.dot(a_ref[...], wo_ref[0:ATTN_WIDTH, :], preferred_element_type=F32)
             + jnp.dot(cp_ref[...], wo_ref[ATTN_WIDTH:D_MODEL, :], preferred_element_type=F32))
    x1 = x_ref[...] + _rms(mixed, gpm_ref[...])
    hf = _rms(x1, gpf_ref[...]).astype(BF16)
    ff = jnp.zeros(x1.shape, F32)
    for c in range(0, D_FF, FFN_CHUNK):
        gate = jnp.dot(hf, wg_ref[:, c:c + FFN_CHUNK], preferred_element_type=F32)
        up = jnp.dot(hf, wu_ref[:, c:c + FFN_CHUNK], preferred_element_type=F32)
        act = (gate * jax.nn.sigmoid(gate) * up).astype(BF16)
        ff = ff + jnp.dot(act, wd_ref[c:c + FFN_CHUNK, :], preferred_element_type=F32)
    o_ref[...] = x1 + _rms(ff, gqf_ref[...])


def _out_ffn(x, attn, cp, w_out, g_post_mix, g_pre_ffn, g_post_ffn, w_gate, w_up, w_down):
    T = x.shape[0]
    tm = FFN_TM
    const = lambda t: (0, 0)
    resident = functools.partial(pl.BlockSpec, index_map=const, pipeline_mode=pl.Buffered(1))
    return pl.pallas_call(
        _out_ffn_kernel,
        out_shape=jax.ShapeDtypeStruct((T, D_MODEL), F32),
        grid=(T // tm,),
        in_specs=[
            pl.BlockSpec((tm, D_MODEL), lambda t: (t, 0)),
            pl.BlockSpec((tm, ATTN_WIDTH), lambda t: (t, 0)),
            pl.BlockSpec((tm, CONV_WIDTH + POOL_WIDTH), lambda t: (t, 0)),
            resident((D_MODEL, D_MODEL)),
            pl.BlockSpec((1, D_MODEL), const),
            pl.BlockSpec((1, D_MODEL), const),
            pl.BlockSpec((1, D_MODEL), const),
            resident((D_MODEL, D_FF)),
            resident((D_MODEL, D_FF)),
            resident((D_FF, D_MODEL)),
        ],
        out_specs=pl.BlockSpec((tm, D_MODEL), lambda t: (t, 0)),
        compiler_params=pltpu.CompilerParams(
            dimension_semantics=("arbitrary",),
            vmem_limit_bytes=VMEM_LIMIT),
        name="out_ffn",
    )(x, attn, cp, w_out, g_post_mix, g_pre_ffn, g_post_ffn, w_gate, w_up, w_down)


def _block_diag(pool_w):
    G, C, _ = pool_w.shape
    eye = jnp.eye(G, dtype=pool_w.dtype)
    return (eye[:, None, :, None] * pool_w[:, :, None, :]).reshape(G * C, G * C)


def kernel(x, w_in, w_out, conv_w, pool_w, pool_scale, g_pre_mix, g_post_mix, g_pre_ffn,
           g_post_ffn, w_gate, w_up, w_down):
    B, S, D = x.shape
    depth = w_in.shape[0]
    assert D == D_MODEL and S % MOBA_BLOCK == 0 and S % IN_TM == 0 and (B * S) % FFN_TM == 0
    qx, kx = _attn_aux(S)
    row = lambda a: a.reshape(1, -1)
    for l in range(depth):
        qkv, cp = _in_proj(x, row(g_pre_mix[l]), w_in[l].astype(BF16), conv_w[l],
                           _block_diag(pool_w[l]).astype(BF16), row(pool_scale[l]))
        attn = _moba_attn(qkv, qx, kx)
        x = _out_ffn(x.reshape(B * S, D), attn.reshape(B * S, ATTN_WIDTH),
                     cp.reshape(B * S, CONV_WIDTH + POOL_WIDTH), w_out[l].astype(BF16),
                     row(g_post_mix[l]), row(g_pre_ffn[l]), row(g_post_ffn[l]),
                     w_gate[l].astype(BF16), w_up[l].astype(BF16),
                     w_down[l].astype(BF16)).reshape(B, S, D)
    return x
```

```python
import functools

import jax
import jax.numpy as jnp
import numpy as np
from jax import lax
from jax.experimental import pallas as pl
from jax.experimental.pallas import tpu as pltpu

D_MODEL = 1024
HEAD_DIM = 64
ATTN_WIDTH = 512
ATTN_HEADS = 8
CONV_WIDTH = 256
CONV_K = 3
POOL_WIDTH = 256
POOL_WINDOWS = (2, 4, 8, 16)
POOL_GROUP_DIM = 64
IN_WIDTH = 2560
MOBA_BLOCK = 256
MOBA_TOPK = 3
D_FF = 2816
NORM_EPS = 1e-6

LANES = 128
HEAD_PAIRS = ATTN_WIDTH // LANES
HALO = 16
GATE_ROWS = 16
QKV_WIDTH = 3 * ATTN_WIDTH
CONV_OFF = QKV_WIDTH
POOL_OFF = CONV_OFF + 3 * CONV_WIDTH

IN_TM = 512
FFN_TM = 512
FFN_CHUNK = 256
VMEM_LIMIT = 56 * 1024 * 1024

BF16 = jnp.bfloat16
F32 = jnp.float32

LOG2E = float(np.log2(np.e))
Q_SCALE = HEAD_DIM ** -0.5 * LOG2E


def _rms(x, g):
    return x * lax.rsqrt(jnp.mean(x * x, axis=-1, keepdims=True) + NORM_EPS) * g


def _in_proj_kernel(x_ref, g_ref, w_ref, cw_ref, pw_ref, ps_ref, qkv_ref, cp_ref,
                    cbuf, pa, pb, pc, pd):
    j = pl.program_id(1)
    tm = IN_TM

    @pl.when(j == 0)
    def _():
        cbuf[0:HALO, :] = jnp.zeros((HALO, CONV_WIDTH), F32)
        pa[0:2 * HALO, :] = jnp.zeros((2 * HALO, POOL_WIDTH), F32)
        pb[0:HALO, :] = jnp.zeros((HALO, POOL_WIDTH), F32)
        pc[0:HALO, :] = jnp.zeros((HALO, POOL_WIDTH), F32)
        pd[0:HALO, :] = jnp.zeros((HALO, POOL_WIDTH), F32)

    h = _rms(x_ref[0], g_ref[...]).astype(BF16)
    cv = jnp.dot(h, w_ref[:, CONV_OFF:POOL_OFF], preferred_element_type=F32)
    up = jnp.dot(h, w_ref[:, POOL_OFF:IN_WIDTH], preferred_element_type=F32)

    q = jnp.dot(h, w_ref[:, 0:ATTN_WIDTH], preferred_element_type=F32) * F32(Q_SCALE)
    qkv_ref[0, :, 0:ATTN_WIDTH] = q.astype(BF16)

    h_conv = cv[:, 0:CONV_WIDTH]
    b_gate = cv[:, CONV_WIDTH:2 * CONV_WIDTH]
    c_gate = cv[:, 2 * CONV_WIDTH:3 * CONV_WIDTH]
    u = c_gate * h_conv
    cbuf[HALO:HALO + tm, :] = u
    conv = (cbuf[HALO - 2:HALO - 2 + tm, :] * cw_ref[0:1, :]
            + cbuf[HALO - 1:HALO - 1 + tm, :] * cw_ref[1:2, :]
            + u * cw_ref[2:3, :])
    cp_ref[0, :, 0:CONV_WIDTH] = (b_gate * conv).astype(BF16)
    cbuf[0:HALO, :] = cbuf[tm:tm + HALO, :]

    qkv_ref[0, :, ATTN_WIDTH:QKV_WIDTH] = jnp.dot(
        h, w_ref[:, ATTN_WIDTH:QKV_WIDTH], preferred_element_type=F32).astype(BF16)

    pa[2 * HALO:2 * HALO + tm, :] = up
    n = tm + HALO
    pb[HALO:HALO + n, :] = pa[HALO:HALO + n, :] + pa[HALO - 1:HALO - 1 + n, :]
    pc[HALO:HALO + n, :] = pb[HALO:HALO + n, :] + pb[HALO - 2:HALO - 2 + n, :]
    pd[HALO:HALO + n, :] = pc[HALO:HALO + n, :] + pc[HALO - 4:HALO - 4 + n, :]
    s2 = pb[2 * HALO:2 * HALO + tm, :]
    s4 = pc[2 * HALO:2 * HALO + tm, :]
    s8 = pd[2 * HALO:2 * HALO + tm, :]
    s16 = s8 + pd[2 * HALO - 8:2 * HALO - 8 + tm, :]
    lane = lax.broadcasted_iota(jnp.int32, (1, POOL_WIDTH), 1)
    g0 = lane < POOL_GROUP_DIM
    g1 = lane < 2 * POOL_GROUP_DIM
    g2 = lane < 3 * POOL_GROUP_DIM
    wsum = jnp.where(g0, s2, jnp.where(g1, s4, jnp.where(g2, s8, s16)))
    w2, w4, w8, w16 = (F32(w) for w in POOL_WINDOWS)
    win = jnp.where(g0, w2, jnp.where(g1, w4, jnp.where(g2, w8, w16)))
    t1 = (j * tm + 1 + lax.broadcasted_iota(jnp.int32, (tm, 1), 0)).astype(F32)
    cnt = jnp.minimum(t1, win)
    pooled = (wsum / cnt - up).astype(BF16)
    y = jnp.dot(pooled, pw_ref[...], preferred_element_type=F32) * ps_ref[...]
    cp_ref[0, :, CONV_WIDTH:CONV_WIDTH + POOL_WIDTH] = y.astype(BF16)
    pa[HALO:2 * HALO, :] = pa[tm + HALO:tm + 2 * HALO, :]


def _in_proj(layer, x, g, w_in, conv_w, pool_bd, pool_scale):
    B, S, _ = x.shape
    tm = IN_TM
    const = lambda b, j: (layer, 0, 0)
    return pl.pallas_call(
        _in_proj_kernel,
        out_shape=(jax.ShapeDtypeStruct((B, S, QKV_WIDTH), BF16),
                   jax.ShapeDtypeStruct((B, S, CONV_WIDTH + POOL_WIDTH), BF16)),
        grid=(B, S // tm),
        in_specs=[
            pl.BlockSpec((1, tm, D_MODEL), lambda b, j: (b, j, 0)),
            pl.BlockSpec((None, 1, D_MODEL), const),
            pl.BlockSpec((None, D_MODEL, IN_WIDTH), const, pipeline_mode=pl.Buffered(1)),
            pl.BlockSpec((None, CONV_K, CONV_WIDTH), const),
            pl.BlockSpec((None, POOL_WIDTH, POOL_WIDTH), const),
            pl.BlockSpec((None, 1, POOL_WIDTH), const),
        ],
        out_specs=(pl.BlockSpec((1, tm, QKV_WIDTH), lambda b, j: (b, j, 0)),
                   pl.BlockSpec((1, tm, CONV_WIDTH + POOL_WIDTH), lambda b, j: (b, j, 0))),
        scratch_shapes=[
            pltpu.VMEM((HALO + tm, CONV_WIDTH), F32),
            pltpu.VMEM((2 * HALO + tm, POOL_WIDTH), F32),
            pltpu.VMEM((2 * HALO + tm, POOL_WIDTH), F32),
            pltpu.VMEM((2 * HALO + tm, POOL_WIDTH), F32),
            pltpu.VMEM((2 * HALO + tm, POOL_WIDTH), F32),
        ],
        compiler_params=pltpu.CompilerParams(
            dimension_semantics=("arbitrary", "arbitrary"),
            vmem_limit_bytes=VMEM_LIMIT),
        name="in_proj",
    )(x, g, w_in, conv_w, pool_bd, pool_scale)


NEG_BIG = -1e30
ALIBI_PARTS = 3
AUX_ALIBI_HI = 0
AUX_ALIBI_LO = ALIBI_PARTS
AUX_BLOCK0 = 8
AUX_ONES = 0
POS_SPLIT = 16


def _block_mask_t(g, q_blk):
    nb = g.shape[0]
    blk_idx = lax.broadcasted_iota(jnp.int32, g.shape, 0)
    elig = blk_idx < q_blk
    ranks = []
    for jb in range(nb):
        gj = g[jb:jb + 1, :]
        beats = ((g > gj) | ((g == gj) & (blk_idx < jb))) & elig
        ranks.append(jnp.sum(beats.astype(F32), axis=0, keepdims=True))
    rank = jnp.concatenate(ranks, axis=0)
    attend = ((rank < MOBA_TOPK) & elig) | (blk_idx == q_blk)
    return jnp.where(attend, F32(0.0), F32(NEG_BIG))


def _attn_kernel(q_ref, k_ref, v_ref, qx_ref, kx_ref, o_ref, qa_ref, ka_ref, va_ref):
    S = k_ref.shape[1]
    blk = MOBA_BLOCK
    nb = S // blk
    k = k_ref[0]
    v = v_ref[0]
    lane = lax.broadcasted_iota(jnp.int32, (1, LANES), 1)
    nt = (((1,), (1,)), ((), ()))

    km = jnp.mean(k.astype(F32).reshape(nb, blk, LANES), axis=1)
    km = jnp.concatenate([km, jnp.zeros((GATE_ROWS - nb, LANES), F32)], axis=0)
    km_hi = km.astype(BF16)
    km_lo = (km - km_hi.astype(F32)).astype(BF16)

    gated = min((MOBA_TOPK + 1) * blk, S)
    q_free = q_ref[0, 0:gated, :]
    q_gate = q_ref[0, gated:S, :]
    q_blk = (gated + lax.broadcasted_iota(jnp.int32, (nb, S - gated), 1)) // blk
    for h in range(2):
        in_head = (lane >= h * HEAD_DIM) & (lane < (h + 1) * HEAD_DIM)
        alibi = qx_ref[0, h:h + 1, :]
        qa_ref[h, 0:gated, :] = jnp.where(
            in_head, q_free, jnp.broadcast_to(alibi.astype(BF16), q_free.shape))
        if gated < S:
            qh = jnp.where(in_head, q_gate, jnp.zeros_like(q_gate))
            g = (lax.dot_general(km_hi, qh, nt, preferred_element_type=F32)
                 + lax.dot_general(km_lo, qh, nt, preferred_element_type=F32))[0:nb]
            aux = (1 - h) * HEAD_DIM + AUX_BLOCK0
            mask_t = jnp.concatenate([jnp.zeros((aux, S - gated), F32), _block_mask_t(g, q_blk),
                                      jnp.zeros((LANES - aux - nb, S - gated), F32)], axis=0)
            extra = mask_t.T + alibi
            qa_ref[h, gated:S, :] = jnp.where(in_head, q_gate, extra.astype(BF16))
        ka_ref[h] = jnp.where(in_head, k, kx_ref[h])
        ones_lane = jnp.where(lane == (1 - h) * HEAD_DIM + AUX_ONES, F32(1.0), F32(0.0))
        va_ref[h] = jnp.where(in_head, v, jnp.broadcast_to(ones_lane.astype(BF16), v.shape))

    row = lax.broadcasted_iota(jnp.int32, (blk, blk), 0)
    col = lax.broadcasted_iota(jnp.int32, (blk, blk), 1)
    causal = col <= row
    for c in range(nb):
        lo, hi = c * blk, (c + 1) * blk
        outs = []
        for h in range(2):
            s = lax.dot_general(qa_ref[h, lo:hi, :], ka_ref[h, 0:hi, :], nt,
                                preferred_element_type=F32)
            s_own = jnp.where(causal, s[:, lo:hi], F32(NEG_BIG))
            m = jnp.max(s_own, axis=-1, keepdims=True)
            if c:
                s_past = s[:, 0:lo]
                m = jnp.maximum(m, jnp.max(s_past, axis=-1, keepdims=True))
                p = jnp.concatenate([jnp.exp2(s_past - m), jnp.exp2(s_own - m)], axis=1)
            else:
                p = jnp.exp2(s_own - m)
            acc = jnp.dot(p.astype(BF16), va_ref[h, 0:hi, :], preferred_element_type=F32)
            ones = (1 - h) * HEAD_DIM + AUX_ONES
            outs.append(acc * (1.0 / acc[:, ones:ones + 1]))
        o_ref[0, lo:hi, :] = jnp.where(lane < HEAD_DIM, outs[0], outs[1]).astype(BF16)


def _moba_attn(qkv, qx, kx):
    B, S, _ = qkv.shape
    return pl.pallas_call(
        _attn_kernel,
        out_shape=jax.ShapeDtypeStruct((B, S, ATTN_WIDTH), BF16),
        grid=(B, HEAD_PAIRS),
        in_specs=[
            pl.BlockSpec((1, S, LANES), lambda b, hp: (b, 0, hp)),
            pl.BlockSpec((1, S, LANES), lambda b, hp: (b, 0, HEAD_PAIRS + hp)),
            pl.BlockSpec((1, S, LANES), lambda b, hp: (b, 0, 2 * HEAD_PAIRS + hp)),
            pl.BlockSpec((1, 2, LANES), lambda b, hp: (hp, 0, 0)),
            pl.BlockSpec((2, S, LANES), lambda b, hp: (0, 0, 0)),
        ],
        out_specs=pl.BlockSpec((1, S, LANES), lambda b, hp: (b, 0, hp)),
        scratch_shapes=[
            pltpu.VMEM((2, S, LANES), BF16),
            pltpu.VMEM((2, S, LANES), BF16),
            pltpu.VMEM((2, S, LANES), BF16),
        ],
        compiler_params=pltpu.CompilerParams(
            dimension_semantics=("arbitrary", "arbitrary"),
            vmem_limit_bytes=VMEM_LIMIT),
        name="moba_attn",
    )(qkv, qkv, qkv, qx, kx)


def _attn_aux(S):
    nb = S // MOBA_BLOCK
    assert S // POS_SPLIT <= 128 and 2 * ALIBI_PARTS <= AUX_BLOCK0 and AUX_BLOCK0 + nb <= HEAD_DIM
    slopes = 2.0 ** (-8.0 * np.arange(1, ATTN_HEADS + 1) / ATTN_HEADS)
    rest = (slopes * LOG2E).astype(np.float32)
    pos = np.arange(S)
    qx = np.zeros((ATTN_HEADS, LANES), np.float32)
    kx = np.zeros((2, S, LANES), np.float32)
    for i in range(ALIBI_PARTS):
        part = rest.astype(BF16).astype(np.float32)
        rest = rest - part
        for h in range(2):
            aux = (1 - h) * HEAD_DIM
            qx[h::2, aux + AUX_ALIBI_HI + i] = part[h::2] * POS_SPLIT
            qx[h::2, aux + AUX_ALIBI_LO + i] = part[h::2]
            kx[h, :, aux + AUX_ALIBI_HI + i] = pos // POS_SPLIT
            kx[h, :, aux + AUX_ALIBI_LO + i] = pos % POS_SPLIT
    for h in range(2):
        aux = (1 - h) * HEAD_DIM
        kx[h, pos, aux + AUX_BLOCK0 + pos // MOBA_BLOCK] = 1.0
    return jnp.asarray(qx.reshape(HEAD_PAIRS, 2, LANES)), jnp.asarray(kx, dtype=BF16)


def _out_ffn_kernel(x_ref, a_ref, cp_ref, wo_ref, gpm_ref, gpf_ref, gqf_ref,
                    wg_ref, wu_ref, wd_ref, o_ref):
    mixed = (jnp.dot(a_ref[...], wo_ref[0:ATTN_WIDTH, :], preferred_element_type=F32)
             + jnp.dot(cp_ref[...], wo_ref[ATTN_WIDTH:D_MODEL, :], preferred_element_type=F32))
    x1 = x_ref[...] + _rms(mixed, gpm_ref[...])
    hf = _rms(x1, gpf_ref[...]).astype(BF16)
    def gate_up(c):
        cols = slice(c * FFN_CHUNK, (c + 1) * FFN_CHUNK)
        return (jnp.dot(hf, wg_ref[:, cols], preferred_element_type=F32),
                jnp.dot(hf, wu_ref[:, cols], preferred_element_type=F32))

    n_chunks = D_FF // FFN_CHUNK
    ff = jnp.zeros(x1.shape, F32)
    nxt = gate_up(0)
    for c in range(n_chunks):
        gate, up = nxt
        if c + 1 < n_chunks:
            nxt = gate_up(c + 1)
        act = (gate * jax.nn.sigmoid(gate) * up).astype(BF16)
        ff = ff + jnp.dot(act, wd_ref[c * FFN_CHUNK:(c + 1) * FFN_CHUNK, :],
                          preferred_element_type=F32)
    o_ref[...] = x1 + _rms(ff, gqf_ref[...])


def _out_ffn(layer, x, attn, cp, w_out, g_post_mix, g_pre_ffn, g_post_ffn, w_gate, w_up, w_down):
    T = x.shape[0]
    tm = FFN_TM
    const = lambda t: (layer, 0, 0)
    resident = functools.partial(pl.BlockSpec, index_map=const, pipeline_mode=pl.Buffered(1))
    return pl.pallas_call(
        _out_ffn_kernel,
        out_shape=jax.ShapeDtypeStruct((T, D_MODEL), F32),
        grid=(T // tm,),
        in_specs=[
            pl.BlockSpec((tm, D_MODEL), lambda t: (t, 0)),
            pl.BlockSpec((tm, ATTN_WIDTH), lambda t: (t, 0)),
            pl.BlockSpec((tm, CONV_WIDTH + POOL_WIDTH), lambda t: (t, 0)),
            resident((None, D_MODEL, D_MODEL)),
            pl.BlockSpec((None, 1, D_MODEL), const),
            pl.BlockSpec((None, 1, D_MODEL), const),
            pl.BlockSpec((None, 1, D_MODEL), const),
            resident((None, D_MODEL, D_FF)),
            resident((None, D_MODEL, D_FF)),
            resident((None, D_FF, D_MODEL)),
        ],
        out_specs=pl.BlockSpec((tm, D_MODEL), lambda t: (t, 0)),
        compiler_params=pltpu.CompilerParams(
            dimension_semantics=("arbitrary",),
            vmem_limit_bytes=VMEM_LIMIT),
        name="out_ffn",
    )(x, attn, cp, w_out, g_post_mix, g_pre_ffn, g_post_ffn, w_gate, w_up, w_down)


def _block_diag(pool_w):
    L, G, C, _ = pool_w.shape
    eye = jnp.eye(G, dtype=pool_w.dtype)
    return (eye[None, :, None, :, None] * pool_w[:, :, :, None, :]).reshape(L, G * C, G * C)


def kernel(x, w_in, w_out, conv_w, pool_w, pool_scale, g_pre_mix, g_post_mix, g_pre_ffn,
           g_post_ffn, w_gate, w_up, w_down):
    B, S, D = x.shape
    depth = w_in.shape[0]
    assert D == D_MODEL and S % MOBA_BLOCK == 0 and S % IN_TM == 0 and (B * S) % FFN_TM == 0
    qx, kx = _attn_aux(S)
    rows = lambda a: a.reshape(depth, 1, -1)
    w_in, w_out, w_gate, w_up, w_down = (w.astype(BF16) for w in (w_in, w_out, w_gate, w_up, w_down))
    pool_bd = _block_diag(pool_w).astype(BF16)
    g_pre_mix, g_post_mix, g_pre_ffn, g_post_ffn, pool_scale = (
        rows(a) for a in (g_pre_mix, g_post_mix, g_pre_ffn, g_post_ffn, pool_scale))
    for l in range(depth):
        qkv, cp = _in_proj(l, x, g_pre_mix, w_in, conv_w, pool_bd, pool_scale)
        attn = _moba_attn(qkv, qx, kx)
        x = _out_ffn(l, x.reshape(B * S, D), attn.reshape(B * S, ATTN_WIDTH),
                     cp.reshape(B * S, CONV_WIDTH + POOL_WIDTH), w_out,
                     g_post_mix, g_pre_ffn, g_post_ffn, w_gate, w_up, w_down).reshape(B, S, D)
    return x
```

```python
import functools

import jax
import jax.numpy as jnp
import numpy as np
from jax import lax
from jax.experimental import pallas as pl
from jax.experimental.pallas import tpu as pltpu

D_MODEL = 1024
HEAD_DIM = 64
ATTN_WIDTH = 512
ATTN_HEADS = 8
CONV_WIDTH = 256
CONV_K = 3
POOL_WIDTH = 256
POOL_WINDOWS = (2, 4, 8, 16)
POOL_GROUP_DIM = 64
IN_WIDTH = 2560
MOBA_BLOCK = 256
MOBA_TOPK = 3
D_FF = 2816
NORM_EPS = 1e-6

LANES = 128
HEAD_PAIRS = ATTN_WIDTH // LANES
HALO = 16
GATE_ROWS = 16
QKV_WIDTH = 3 * ATTN_WIDTH
CONV_OFF = QKV_WIDTH
POOL_OFF = CONV_OFF + 3 * CONV_WIDTH

IN_TM = 512
FFN_TM = 512
FFN_CHUNK = 256
VMEM_LIMIT = 56 * 1024 * 1024

BF16 = jnp.bfloat16
F32 = jnp.float32

LOG2E = float(np.log2(np.e))
Q_SCALE = HEAD_DIM ** -0.5 * LOG2E


def _rms(x, g):
    return x * lax.rsqrt(jnp.mean(x * x, axis=-1, keepdims=True) + NORM_EPS) * g


def _in_proj_kernel(x_ref, g_ref, w_ref, cw_ref, pw_ref, ps_ref, qkv_ref, cp_ref,
                    cbuf, pa, pb, pc, pd):
    j = pl.program_id(1)
    tm = IN_TM

    @pl.when(j == 0)
    def _():
        cbuf[0:HALO, :] = jnp.zeros((HALO, CONV_WIDTH), F32)
        pa[0:2 * HALO, :] = jnp.zeros((2 * HALO, POOL_WIDTH), F32)
        pb[0:HALO, :] = jnp.zeros((HALO, POOL_WIDTH), F32)
        pc[0:HALO, :] = jnp.zeros((HALO, POOL_WIDTH), F32)
        pd[0:HALO, :] = jnp.zeros((HALO, POOL_WIDTH), F32)

    h = _rms(x_ref[0], g_ref[...]).astype(BF16)
    cv = jnp.dot(h, w_ref[:, CONV_OFF:POOL_OFF], preferred_element_type=F32)
    up = jnp.dot(h, w_ref[:, POOL_OFF:IN_WIDTH], preferred_element_type=F32)

    q = jnp.dot(h, w_ref[:, 0:ATTN_WIDTH], preferred_element_type=F32) * F32(Q_SCALE)
    qkv_ref[0, :, 0:ATTN_WIDTH] = q.astype(BF16)

    h_conv = cv[:, 0:CONV_WIDTH]
    b_gate = cv[:, CONV_WIDTH:2 * CONV_WIDTH]
    c_gate = cv[:, 2 * CONV_WIDTH:3 * CONV_WIDTH]
    u = c_gate * h_conv
    cbuf[HALO:HALO + tm, :] = u
    conv = (cbuf[HALO - 2:HALO - 2 + tm, :] * cw_ref[0:1, :]
            + cbuf[HALO - 1:HALO - 1 + tm, :] * cw_ref[1:2, :]
            + u * cw_ref[2:3, :])
    cp_ref[0, :, 0:CONV_WIDTH] = (b_gate * conv).astype(BF16)
    cbuf[0:HALO, :] = cbuf[tm:tm + HALO, :]

    qkv_ref[0, :, ATTN_WIDTH:QKV_WIDTH] = jnp.dot(
        h, w_ref[:, ATTN_WIDTH:QKV_WIDTH], preferred_element_type=F32).astype(BF16)

    pa[2 * HALO:2 * HALO + tm, :] = up
    n = tm + HALO
    pb[HALO:HALO + n, :] = pa[HALO:HALO + n, :] + pa[HALO - 1:HALO - 1 + n, :]
    pc[HALO:HALO + n, :] = pb[HALO:HALO + n, :] + pb[HALO - 2:HALO - 2 + n, :]
    pd[HALO:HALO + n, :] = pc[HALO:HALO + n, :] + pc[HALO - 4:HALO - 4 + n, :]
    s2 = pb[2 * HALO:2 * HALO + tm, :]
    s4 = pc[2 * HALO:2 * HALO + tm, :]
    s8 = pd[2 * HALO:2 * HALO + tm, :]
    s16 = s8 + pd[2 * HALO - 8:2 * HALO - 8 + tm, :]
    lane = lax.broadcasted_iota(jnp.int32, (1, POOL_WIDTH), 1)
    g0 = lane < POOL_GROUP_DIM
    g1 = lane < 2 * POOL_GROUP_DIM
    g2 = lane < 3 * POOL_GROUP_DIM
    wsum = jnp.where(g0, s2, jnp.where(g1, s4, jnp.where(g2, s8, s16)))
    w2, w4, w8, w16 = (F32(w) for w in POOL_WINDOWS)
    win = jnp.where(g0, w2, jnp.where(g1, w4, jnp.where(g2, w8, w16)))
    t1 = (j * tm + 1 + lax.broadcasted_iota(jnp.int32, (tm, 1), 0)).astype(F32)
    cnt = jnp.minimum(t1, win)
    pooled = (wsum / cnt - up).astype(BF16)
    y = jnp.dot(pooled, pw_ref[...], preferred_element_type=F32) * ps_ref[...]
    cp_ref[0, :, CONV_WIDTH:CONV_WIDTH + POOL_WIDTH] = y.astype(BF16)
    pa[HALO:2 * HALO, :] = pa[tm + HALO:tm + 2 * HALO, :]


def _in_proj(layer, x, g, w_in, conv_w, pool_bd, pool_scale):
    B, S, _ = x.shape
    tm = IN_TM
    const = lambda b, j: (layer, 0, 0)
    return pl.pallas_call(
        _in_proj_kernel,
        out_shape=(jax.ShapeDtypeStruct((B, S, QKV_WIDTH), BF16),
                   jax.ShapeDtypeStruct((B, S, CONV_WIDTH + POOL_WIDTH), BF16)),
        grid=(B, S // tm),
        in_specs=[
            pl.BlockSpec((1, tm, D_MODEL), lambda b, j: (b, j, 0)),
            pl.BlockSpec((None, 1, D_MODEL), const),
            pl.BlockSpec((None, D_MODEL, IN_WIDTH), const, pipeline_mode=pl.Buffered(1)),
            pl.BlockSpec((None, CONV_K, CONV_WIDTH), const),
            pl.BlockSpec((None, POOL_WIDTH, POOL_WIDTH), const),
            pl.BlockSpec((None, 1, POOL_WIDTH), const),
        ],
        out_specs=(pl.BlockSpec((1, tm, QKV_WIDTH), lambda b, j: (b, j, 0)),
                   pl.BlockSpec((1, tm, CONV_WIDTH + POOL_WIDTH), lambda b, j: (b, j, 0))),
        scratch_shapes=[
            pltpu.VMEM((HALO + tm, CONV_WIDTH), F32),
            pltpu.VMEM((2 * HALO + tm, POOL_WIDTH), F32),
            pltpu.VMEM((2 * HALO + tm, POOL_WIDTH), F32),
            pltpu.VMEM((2 * HALO + tm, POOL_WIDTH), F32),
            pltpu.VMEM((2 * HALO + tm, POOL_WIDTH), F32),
        ],
        compiler_params=pltpu.CompilerParams(
            dimension_semantics=("arbitrary", "arbitrary"),
            vmem_limit_bytes=VMEM_LIMIT),
        name="in_proj",
    )(x, g, w_in, conv_w, pool_bd, pool_scale)


NEG_BIG = -1e30
ALIBI_PARTS = 3
AUX_ALIBI_HI = 0
AUX_ALIBI_LO = ALIBI_PARTS
AUX_BLOCK0 = 8
AUX_ONES = 0
POS_SPLIT = 16


def _block_mask_t(g, q_blk):
    nb = g.shape[0]
    blk_idx = lax.broadcasted_iota(jnp.int32, g.shape, 0)
    elig = blk_idx < q_blk
    ranks = []
    for jb in range(nb):
        gj = g[jb:jb + 1, :]
        beats = ((g > gj) | ((g == gj) & (blk_idx < jb))) & elig
        ranks.append(jnp.sum(beats.astype(F32), axis=0, keepdims=True))
    rank = jnp.concatenate(ranks, axis=0)
    attend = ((rank < MOBA_TOPK) & elig) | (blk_idx == q_blk)
    return jnp.where(attend, F32(0.0), F32(NEG_BIG))


def _attn_kernel(q_ref, k_ref, v_ref, qx_ref, kx_ref, o_ref, qa_ref, ka_ref, va_ref):
    S = k_ref.shape[1]
    blk = MOBA_BLOCK
    nb = S // blk
    k = k_ref[0]
    v = v_ref[0]
    lane = lax.broadcasted_iota(jnp.int32, (1, LANES), 1)
    nt = (((1,), (1,)), ((), ()))

    km = jnp.mean(k.astype(F32).reshape(nb, blk, LANES), axis=1)
    km = jnp.concatenate([km, jnp.zeros((GATE_ROWS - nb, LANES), F32)], axis=0)
    km_hi = km.astype(BF16)
    km_lo = (km - km_hi.astype(F32)).astype(BF16)

    gated = min((MOBA_TOPK + 1) * blk, S)
    q_free = q_ref[0, 0:gated, :]
    q_gate = q_ref[0, gated:S, :]
    q_blk = (gated + lax.broadcasted_iota(jnp.int32, (nb, S - gated), 1)) // blk
    for h in range(2):
        in_head = (lane >= h * HEAD_DIM) & (lane < (h + 1) * HEAD_DIM)
        alibi = qx_ref[0, h:h + 1, :]
        qa_ref[h, 0:gated, :] = jnp.where(
            in_head, q_free, jnp.broadcast_to(alibi.astype(BF16), q_free.shape))
        if gated < S:
            qh = jnp.where(in_head, q_gate, jnp.zeros_like(q_gate))
            g = (lax.dot_general(km_hi, qh, nt, preferred_element_type=F32)
                 + lax.dot_general(km_lo, qh, nt, preferred_element_type=F32))[0:nb]
            aux = (1 - h) * HEAD_DIM + AUX_BLOCK0
            mask_t = jnp.concatenate([jnp.zeros((aux, S - gated), F32), _block_mask_t(g, q_blk),
                                      jnp.zeros((LANES - aux - nb, S - gated), F32)], axis=0)
            extra = mask_t.T + alibi
            qa_ref[h, gated:S, :] = jnp.where(in_head, q_gate, extra.astype(BF16))
        ka_ref[h] = jnp.where(in_head, k, kx_ref[h])
        ones_lane = jnp.where(lane == (1 - h) * HEAD_DIM + AUX_ONES, F32(1.0), F32(0.0))
        va_ref[h] = jnp.where(in_head, v, jnp.broadcast_to(ones_lane.astype(BF16), v.shape))

    row = lax.broadcasted_iota(jnp.int32, (blk, blk), 0)
    col = lax.broadcasted_iota(jnp.int32, (blk, blk), 1)
    causal = col <= row

    def scores(c, h):
        lo, hi = c * blk, (c + 1) * blk
        s = lax.dot_general(qa_ref[h, lo:hi, :], ka_ref[h, 0:hi, :], nt,
                            preferred_element_type=F32)
        s_own = jnp.where(causal, s[:, lo:hi], F32(NEG_BIG))
        m = jnp.max(s_own, axis=-1, keepdims=True)
        if not c:
            return [s_own], m
        s_past = s[:, 0:lo]
        return [s_past, s_own], jnp.maximum(m, jnp.max(s_past, axis=-1, keepdims=True))

    def probs(parts, m):
        p = [jnp.exp2(s - m).astype(BF16) for s in parts]
        return p[0] if len(p) == 1 else jnp.concatenate(p, axis=1)

    def weighted_values(c, h, p):
        acc = jnp.dot(p, va_ref[h, 0:(c + 1) * blk, :], preferred_element_type=F32)
        ones = (1 - h) * HEAD_DIM + AUX_ONES
        return acc * (1.0 / acc[:, ones:ones + 1])

    chains = [(c, h) for c in range(nb) for h in range(2)]
    staged_s, staged_p, outs = {}, {}, {}
    for t in range(len(chains) + 2):
        if t < len(chains):
            staged_s[t] = scores(*chains[t])
        if 0 <= t - 1 < len(chains):
            staged_p[t - 1] = probs(*staged_s.pop(t - 1))
        if 0 <= t - 2 < len(chains):
            c, h = chains[t - 2]
            outs[h] = weighted_values(c, h, staged_p.pop(t - 2))
            if h == 1:
                o_ref[0, c * blk:(c + 1) * blk, :] = jnp.where(
                    lane < HEAD_DIM, outs[0], outs[1]).astype(BF16)


def _moba_attn(qkv, qx, kx):
    B, S, _ = qkv.shape
    return pl.pallas_call(
        _attn_kernel,
        out_shape=jax.ShapeDtypeStruct((B, S, ATTN_WIDTH), BF16),
        grid=(B, HEAD_PAIRS),
        in_specs=[
            pl.BlockSpec((1, S, LANES), lambda b, hp: (b, 0, hp)),
            pl.BlockSpec((1, S, LANES), lambda b, hp: (b, 0, HEAD_PAIRS + hp)),
            pl.BlockSpec((1, S, LANES), lambda b, hp: (b, 0, 2 * HEAD_PAIRS + hp)),
            pl.BlockSpec((1, 2, LANES), lambda b, hp: (hp, 0, 0)),
            pl.BlockSpec((2, S, LANES), lambda b, hp: (0, 0, 0)),
        ],
        out_specs=pl.BlockSpec((1, S, LANES), lambda b, hp: (b, 0, hp)),
        scratch_shapes=[
            pltpu.VMEM((2, S, LANES), BF16),
            pltpu.VMEM((2, S, LANES), BF16),
            pltpu.VMEM((2, S, LANES), BF16),
        ],
        compiler_params=pltpu.CompilerParams(
            dimension_semantics=("arbitrary", "arbitrary"),
            vmem_limit_bytes=VMEM_LIMIT),
        name="moba_attn",
    )(qkv, qkv, qkv, qx, kx)


def _attn_aux(S):
    nb = S // MOBA_BLOCK
    assert S // POS_SPLIT <= 128 and 2 * ALIBI_PARTS <= AUX_BLOCK0 and AUX_BLOCK0 + nb <= HEAD_DIM
    slopes = 2.0 ** (-8.0 * np.arange(1, ATTN_HEADS + 1) / ATTN_HEADS)
    rest = (slopes * LOG2E).astype(np.float32)
    pos = np.arange(S)
    qx = np.zeros((ATTN_HEADS, LANES), np.float32)
    kx = np.zeros((2, S, LANES), np.float32)
    for i in range(ALIBI_PARTS):
        part = rest.astype(BF16).astype(np.float32)
        rest = rest - part
        for h in range(2):
            aux = (1 - h) * HEAD_DIM
            qx[h::2, aux + AUX_ALIBI_HI + i] = part[h::2] * POS_SPLIT
            qx[h::2, aux + AUX_ALIBI_LO + i] = part[h::2]
            kx[h, :, aux + AUX_ALIBI_HI + i] = pos // POS_SPLIT
            kx[h, :, aux + AUX_ALIBI_LO + i] = pos % POS_SPLIT
    for h in range(2):
        aux = (1 - h) * HEAD_DIM
        kx[h, pos, aux + AUX_BLOCK0 + pos // MOBA_BLOCK] = 1.0
    return jnp.asarray(qx.reshape(HEAD_PAIRS, 2, LANES)), jnp.asarray(kx, dtype=BF16)


def _out_ffn_kernel(x_ref, a_ref, cp_ref, wo_ref, gpm_ref, gpf_ref, gqf_ref,
                    wg_ref, wu_ref, wd_ref, o_ref):
    mixed = (jnp.dot(a_ref[...], wo_ref[0:ATTN_WIDTH, :], preferred_element_type=F32)
             + jnp.dot(cp_ref[...], wo_ref[ATTN_WIDTH:D_MODEL, :], preferred_element_type=F32))
    x1 = x_ref[...] + _rms(mixed, gpm_ref[...])
    hf = _rms(x1, gpf_ref[...]).astype(BF16)
    def gate_up(c):
        cols = slice(c * FFN_CHUNK, (c + 1) * FFN_CHUNK)
        return (jnp.dot(hf, wg_ref[:, cols], preferred_element_type=F32),
                jnp.dot(hf, wu_ref[:, cols], preferred_element_type=F32))

    n_chunks = D_FF // FFN_CHUNK
    ff = jnp.zeros(x1.shape, F32)
    nxt = gate_up(0)
    for c in range(n_chunks):
        gate, up = nxt
        if c + 1 < n_chunks:
            nxt = gate_up(c + 1)
        act = (gate * jax.nn.sigmoid(gate) * up).astype(BF16)
        ff = ff + jnp.dot(act, wd_ref[c * FFN_CHUNK:(c + 1) * FFN_CHUNK, :],
                          preferred_element_type=F32)
    o_ref[...] = x1 + _rms(ff, gqf_ref[...])


def _out_ffn(layer, x, attn, cp, w_out, g_post_mix, g_pre_ffn, g_post_ffn, w_gate, w_up, w_down):
    T = x.shape[0]
    tm = FFN_TM
    const = lambda t: (layer, 0, 0)
    resident = functools.partial(pl.BlockSpec, index_map=const, pipeline_mode=pl.Buffered(1))
    return pl.pallas_call(
        _out_ffn_kernel,
        out_shape=jax.ShapeDtypeStruct((T, D_MODEL), F32),
        grid=(T // tm,),
        in_specs=[
            pl.BlockSpec((tm, D_MODEL), lambda t: (t, 0)),
            pl.BlockSpec((tm, ATTN_WIDTH), lambda t: (t, 0)),
            pl.BlockSpec((tm, CONV_WIDTH + POOL_WIDTH), lambda t: (t, 0)),
            resident((None, D_MODEL, D_MODEL)),
            pl.BlockSpec((None, 1, D_MODEL), const),
            pl.BlockSpec((None, 1, D_MODEL), const),
            pl.BlockSpec((None, 1, D_MODEL), const),
            resident((None, D_MODEL, D_FF)),
            resident((None, D_MODEL, D_FF)),
            resident((None, D_FF, D_MODEL)),
        ],
        out_specs=pl.BlockSpec((tm, D_MODEL), lambda t: (t, 0)),
        compiler_params=pltpu.CompilerParams(
            dimension_semantics=("arbitrary",),
            vmem_limit_bytes=VMEM_LIMIT),
        name="out_ffn",
    )(x, attn, cp, w_out, g_post_mix, g_pre_ffn, g_post_ffn, w_gate, w_up, w_down)


def _block_diag(pool_w):
    L, G, C, _ = pool_w.shape
    eye = jnp.eye(G, dtype=pool_w.dtype)
    return (eye[None, :, None, :, None] * pool_w[:, :, :, None, :]).reshape(L, G * C, G * C)


def kernel(x, w_in, w_out, conv_w, pool_w, pool_scale, g_pre_mix, g_post_mix, g_pre_ffn,
           g_post_ffn, w_gate, w_up, w_down):
    B, S, D = x.shape
    depth = w_in.shape[0]
    assert D == D_MODEL and S % MOBA_BLOCK == 0 and S % IN_TM == 0 and (B * S) % FFN_TM == 0
    qx, kx = _attn_aux(S)
    rows = lambda a: a.reshape(depth, 1, -1)
    w_in, w_out, w_gate, w_up, w_down = (w.astype(BF16) for w in (w_in, w_out, w_gate, w_up, w_down))
    pool_bd = _block_diag(pool_w).astype(BF16)
    g_pre_mix, g_post_mix, g_pre_ffn, g_post_ffn, pool_scale = (
        rows(a) for a in (g_pre_mix, g_post_mix, g_pre_ffn, g_post_ffn, pool_scale))
    for l in range(depth):
        qkv, cp = _in_proj(l, x, g_pre_mix, w_in, conv_w, pool_bd, pool_scale)
        attn = _moba_attn(qkv, qx, kx)
        x = _out_ffn(l, x.reshape(B * S, D), attn.reshape(B * S, ATTN_WIDTH),
                     cp.reshape(B * S, CONV_WIDTH + POOL_WIDTH), w_out,
                     g_post_mix, g_pre_ffn, g_post_ffn, w_gate, w_up, w_down).reshape(B, S, D)
    return x
```

```python
import functools

import jax
import jax.numpy as jnp
import numpy as np
from jax import lax
from jax.experimental import pallas as pl
from jax.experimental.pallas import tpu as pltpu

D_MODEL = 1024
HEAD_DIM = 64
ATTN_WIDTH = 512
ATTN_HEADS = 8
CONV_WIDTH = 256
CONV_K = 3
POOL_WIDTH = 256
POOL_WINDOWS = (2, 4, 8, 16)
POOL_GROUP_DIM = 64
IN_WIDTH = 2560
MOBA_BLOCK = 256
MOBA_TOPK = 3
D_FF = 2816
NORM_EPS = 1e-6

LANES = 128
HEAD_PAIRS = ATTN_WIDTH // LANES
HALO = 16
GATE_ROWS = 16
QKV_WIDTH = 3 * ATTN_WIDTH
CONV_OFF = QKV_WIDTH
POOL_OFF = CONV_OFF + 3 * CONV_WIDTH

IN_TM = 512
FFN_TM = 512
FFN_CHUNK = 256
VMEM_LIMIT = 56 * 1024 * 1024

BF16 = jnp.bfloat16
F32 = jnp.float32

LOG2E = float(np.log2(np.e))
Q_SCALE = HEAD_DIM ** -0.5 * LOG2E


def _rms(x, g):
    return x * lax.rsqrt(jnp.mean(x * x, axis=-1, keepdims=True) + NORM_EPS) * g


def _in_proj_kernel(x_ref, g_ref, w_ref, cw_ref, pw_ref, ps_ref, qkv_ref, cp_ref,
                    cbuf, pa, pb, pc, pd):
    j = pl.program_id(1)
    tm = IN_TM

    @pl.when(j == 0)
    def _():
        cbuf[0:HALO, :] = jnp.zeros((HALO, CONV_WIDTH), F32)
        pa[0:2 * HALO, :] = jnp.zeros((2 * HALO, POOL_WIDTH), F32)
        pb[0:HALO, :] = jnp.zeros((HALO, POOL_WIDTH), F32)
        pc[0:HALO, :] = jnp.zeros((HALO, POOL_WIDTH), F32)
        pd[0:HALO, :] = jnp.zeros((HALO, POOL_WIDTH), F32)

    h = _rms(x_ref[0], g_ref[...]).astype(BF16)
    cv = jnp.dot(h, w_ref[:, CONV_OFF:POOL_OFF], preferred_element_type=F32)
    up = jnp.dot(h, w_ref[:, POOL_OFF:IN_WIDTH], preferred_element_type=F32)

    q = jnp.dot(h, w_ref[:, 0:ATTN_WIDTH], preferred_element_type=F32) * F32(Q_SCALE)
    qkv_ref[0, :, 0:ATTN_WIDTH] = q.astype(BF16)

    h_conv = cv[:, 0:CONV_WIDTH]
    b_gate = cv[:, CONV_WIDTH:2 * CONV_WIDTH]
    c_gate = cv[:, 2 * CONV_WIDTH:3 * CONV_WIDTH]
    u = c_gate * h_conv
    cbuf[HALO:HALO + tm, :] = u
    conv = (cbuf[HALO - 2:HALO - 2 + tm, :] * cw_ref[0:1, :]
            + cbuf[HALO - 1:HALO - 1 + tm, :] * cw_ref[1:2, :]
            + u * cw_ref[2:3, :])
    cp_ref[0, :, 0:CONV_WIDTH] = (b_gate * conv).astype(BF16)
    cbuf[0:HALO, :] = cbuf[tm:tm + HALO, :]

    qkv_ref[0, :, ATTN_WIDTH:QKV_WIDTH] = jnp.dot(
        h, w_ref[:, ATTN_WIDTH:QKV_WIDTH], preferred_element_type=F32).astype(BF16)

    pa[2 * HALO:2 * HALO + tm, :] = up
    n = tm + HALO
    pb[HALO:HALO + n, :] = pa[HALO:HALO + n, :] + pa[HALO - 1:HALO - 1 + n, :]
    pc[HALO:HALO + n, :] = pb[HALO:HALO + n, :] + pb[HALO - 2:HALO - 2 + n, :]
    pd[HALO:HALO + n, :] = pc[HALO:HALO + n, :] + pc[HALO - 4:HALO - 4 + n, :]
    s2 = pb[2 * HALO:2 * HALO + tm, :]
    s4 = pc[2 * HALO:2 * HALO + tm, :]
    s8 = pd[2 * HALO:2 * HALO + tm, :]
    s16 = s8 + pd[2 * HALO - 8:2 * HALO - 8 + tm, :]
    lane = lax.broadcasted_iota(jnp.int32, (1, POOL_WIDTH), 1)
    g0 = lane < POOL_GROUP_DIM
    g1 = lane < 2 * POOL_GROUP_DIM
    g2 = lane < 3 * POOL_GROUP_DIM
    wsum = jnp.where(g0, s2, jnp.where(g1, s4, jnp.where(g2, s8, s16)))
    w2, w4, w8, w16 = (F32(w) for w in POOL_WINDOWS)
    win = jnp.where(g0, w2, jnp.where(g1, w4, jnp.where(g2, w8, w16)))
    t1 = (j * tm + 1 + lax.broadcasted_iota(jnp.int32, (tm, 1), 0)).astype(F32)
    cnt = jnp.minimum(t1, win)
    pooled = (wsum / cnt - up).astype(BF16)
    y = jnp.dot(pooled, pw_ref[...], preferred_element_type=F32) * ps_ref[...]
    cp_ref[0, :, CONV_WIDTH:CONV_WIDTH + POOL_WIDTH] = y.astype(BF16)
    pa[HALO:2 * HALO, :] = pa[tm + HALO:tm + 2 * HALO, :]


def _in_proj(layer, x, g, w_in, conv_w, pool_bd, pool_scale):
    B, S, _ = x.shape
    tm = IN_TM
    const = lambda b, j: (layer, 0, 0)
    return pl.pallas_call(
        _in_proj_kernel,
        out_shape=(jax.ShapeDtypeStruct((B, S, QKV_WIDTH), BF16),
                   jax.ShapeDtypeStruct((B, S, CONV_WIDTH + POOL_WIDTH), BF16)),
        grid=(B, S // tm),
        in_specs=[
            pl.BlockSpec((1, tm, D_MODEL), lambda b, j: (b, j, 0)),
            pl.BlockSpec((None, 1, D_MODEL), const),
            pl.BlockSpec((None, D_MODEL, IN_WIDTH), const, pipeline_mode=pl.Buffered(1)),
            pl.BlockSpec((None, CONV_K, CONV_WIDTH), const),
            pl.BlockSpec((None, POOL_WIDTH, POOL_WIDTH), const),
            pl.BlockSpec((None, 1, POOL_WIDTH), const),
        ],
        out_specs=(pl.BlockSpec((1, tm, QKV_WIDTH), lambda b, j: (b, j, 0)),
                   pl.BlockSpec((1, tm, CONV_WIDTH + POOL_WIDTH), lambda b, j: (b, j, 0))),
        scratch_shapes=[
            pltpu.VMEM((HALO + tm, CONV_WIDTH), F32),
            pltpu.VMEM((2 * HALO + tm, POOL_WIDTH), F32),
            pltpu.VMEM((2 * HALO + tm, POOL_WIDTH), F32),
            pltpu.VMEM((2 * HALO + tm, POOL_WIDTH), F32),
            pltpu.VMEM((2 * HALO + tm, POOL_WIDTH), F32),
        ],
        compiler_params=pltpu.CompilerParams(
            dimension_semantics=("arbitrary", "arbitrary"),
            vmem_limit_bytes=VMEM_LIMIT),
        name="in_proj",
    )(x, g, w_in, conv_w, pool_bd, pool_scale)


NEG_BIG = -1e30
ALIBI_PARTS = 3
AUX_ALIBI_HI = 0
AUX_ALIBI_LO = ALIBI_PARTS
AUX_BLOCK0 = 8
VT_ROWS = HEAD_DIM + 16
EXP_LAG = 2
PV_LAG = 4
POS_SPLIT = 16


def _block_mask_t(g, q_blk):
    nb = g.shape[0]
    blk_idx = lax.broadcasted_iota(jnp.int32, g.shape, 0)
    elig = blk_idx < q_blk
    ranks = []
    for jb in range(nb):
        gj = g[jb:jb + 1, :]
        beats = ((g > gj) | ((g == gj) & (blk_idx < jb))) & elig
        ranks.append(jnp.sum(beats.astype(F32), axis=0, keepdims=True))
    rank = jnp.concatenate(ranks, axis=0)
    attend = ((rank < MOBA_TOPK) & elig) | (blk_idx == q_blk)
    return jnp.where(attend, F32(0.0), F32(NEG_BIG))


def _attn_kernel(q_ref, k_ref, v_ref, qx_ref, kx_ref, o_ref, qa_ref, ka_ref, vt_ref):
    S = k_ref.shape[1]
    blk = MOBA_BLOCK
    nb = S // blk
    k = k_ref[0]
    v = v_ref[0]
    lane = lax.broadcasted_iota(jnp.int32, (1, LANES), 1)
    nt = (((1,), (1,)), ((), ()))

    km = jnp.mean(k.astype(F32).reshape(nb, blk, LANES), axis=1)
    km = jnp.concatenate([km, jnp.zeros((GATE_ROWS - nb, LANES), F32)], axis=0)
    km_hi = km.astype(BF16)
    km_lo = (km - km_hi.astype(F32)).astype(BF16)

    gated = min((MOBA_TOPK + 1) * blk, S)
    q_free = q_ref[0, 0:gated, :]
    q_gate = q_ref[0, gated:S, :]
    q_blk = (gated + lax.broadcasted_iota(jnp.int32, (nb, S - gated), 1)) // blk
    in_heads = [(lane >= h * HEAD_DIM) & (lane < (h + 1) * HEAD_DIM) for h in range(2)]
    v_t = v.T
    ones_tile = jnp.where(lax.broadcasted_iota(jnp.int32, (VT_ROWS - HEAD_DIM, S), 0) == 0,
                          F32(1.0), F32(0.0)).astype(BF16)
    for h in range(2):
        alibi = qx_ref[0, h:h + 1, :]
        qa_ref[h, 0:gated, :] = jnp.where(
            in_heads[h], q_free, jnp.broadcast_to(alibi.astype(BF16), q_free.shape))
        ka_ref[h] = jnp.where(in_heads[h], k, kx_ref[h])
        vt_ref[h] = jnp.concatenate([v_t[h * HEAD_DIM:(h + 1) * HEAD_DIM], ones_tile], axis=0)

    def gate_queries(h):
        qh = jnp.where(in_heads[h], q_gate, jnp.zeros_like(q_gate))
        g = (lax.dot_general(km_hi, qh, nt, preferred_element_type=F32)
             + lax.dot_general(km_lo, qh, nt, preferred_element_type=F32))[0:nb]
        aux = (1 - h) * HEAD_DIM + AUX_BLOCK0
        mask_t = jnp.concatenate([jnp.zeros((aux, S - gated), F32), _block_mask_t(g, q_blk),
                                  jnp.zeros((LANES - aux - nb, S - gated), F32)], axis=0)
        extra = mask_t.T + qx_ref[0, h:h + 1, :]
        qa_ref[h, gated:S, :] = jnp.where(in_heads[h], q_gate, extra.astype(BF16))

    key = lax.broadcasted_iota(jnp.int32, (blk, blk), 0)
    qry = lax.broadcasted_iota(jnp.int32, (blk, blk), 1)
    causal = key <= qry

    def scores(c, h):
        lo, hi = c * blk, (c + 1) * blk
        s = lax.dot_general(ka_ref[h, 0:hi, :], qa_ref[h, lo:hi, :], nt,
                            preferred_element_type=F32)
        s_own = jnp.where(causal, s[lo:hi], F32(NEG_BIG))
        m = jnp.max(s_own, axis=0, keepdims=True)
        if not c:
            return [s_own], m
        s_past = s[0:lo]
        return [s_past, s_own], jnp.maximum(m, jnp.max(s_past, axis=0, keepdims=True))

    def probs(parts, m):
        p = [jnp.exp2(s - m).astype(BF16) for s in parts]
        return p[0] if len(p) == 1 else jnp.concatenate(p, axis=0)

    def weighted_values(c, h, p):
        acc = jnp.dot(vt_ref[h, :, 0:(c + 1) * blk], p, preferred_element_type=F32)
        return acc[0:HEAD_DIM] * (1.0 / acc[HEAD_DIM:HEAD_DIM + 1])

    chains = [(c, h) for c in range(nb) for h in range(2)]
    n_free = 2 * (gated // blk)
    gate_after = {n_free // 4: 0, n_free * 5 // 8: 1} if gated < S else {}
    staged_s, staged_p, outs = {}, {}, {}
    for t in range(len(chains) + PV_LAG):
        if t < len(chains):
            staged_s[t] = scores(*chains[t])
        if t in gate_after:
            gate_queries(gate_after[t])
        if 0 <= t - EXP_LAG < len(chains):
            staged_p[t - EXP_LAG] = probs(*staged_s.pop(t - EXP_LAG))
        if 0 <= t - PV_LAG < len(chains):
            c, h = chains[t - PV_LAG]
            outs[h] = weighted_values(c, h, staged_p.pop(t - PV_LAG))
            if h == 1:
                o_ref[0, c * blk:(c + 1) * blk, :] = jnp.concatenate(
                    [outs[0], outs[1]], axis=0).T.astype(BF16)


def _moba_attn(qkv, qx, kx):
    B, S, _ = qkv.shape
    return pl.pallas_call(
        _attn_kernel,
        out_shape=jax.ShapeDtypeStruct((B, S, ATTN_WIDTH), BF16),
        grid=(B, HEAD_PAIRS),
        in_specs=[
            pl.BlockSpec((1, S, LANES), lambda b, hp: (b, 0, hp)),
            pl.BlockSpec((1, S, LANES), lambda b, hp: (b, 0, HEAD_PAIRS + hp)),
            pl.BlockSpec((1, S, LANES), lambda b, hp: (b, 0, 2 * HEAD_PAIRS + hp)),
            pl.BlockSpec((1, 2, LANES), lambda b, hp: (hp, 0, 0)),
            pl.BlockSpec((2, S, LANES), lambda b, hp: (0, 0, 0)),
        ],
        out_specs=pl.BlockSpec((1, S, LANES), lambda b, hp: (b, 0, hp)),
        scratch_shapes=[
            pltpu.VMEM((2, S, LANES), BF16),
            pltpu.VMEM((2, S, LANES), BF16),
            pltpu.VMEM((2, VT_ROWS, S), BF16),
        ],
        compiler_params=pltpu.CompilerParams(
            dimension_semantics=("arbitrary", "arbitrary"),
            vmem_limit_bytes=VMEM_LIMIT),
        name="moba_attn",
    )(qkv, qkv, qkv, qx, kx)


def _attn_aux(S):
    nb = S // MOBA_BLOCK
    assert S // POS_SPLIT <= 128 and 2 * ALIBI_PARTS <= AUX_BLOCK0 and AUX_BLOCK0 + nb <= HEAD_DIM
    slopes = 2.0 ** (-8.0 * np.arange(1, ATTN_HEADS + 1) / ATTN_HEADS)
    rest = (slopes * LOG2E).astype(np.float32)
    pos = np.arange(S)
    qx = np.zeros((ATTN_HEADS, LANES), np.float32)
    kx = np.zeros((2, S, LANES), np.float32)
    for i in range(ALIBI_PARTS):
        part = rest.astype(BF16).astype(np.float32)
        rest = rest - part
        for h in range(2):
            aux = (1 - h) * HEAD_DIM
            qx[h::2, aux + AUX_ALIBI_HI + i] = part[h::2] * POS_SPLIT
            qx[h::2, aux + AUX_ALIBI_LO + i] = part[h::2]
            kx[h, :, aux + AUX_ALIBI_HI + i] = pos // POS_SPLIT
            kx[h, :, aux + AUX_ALIBI_LO + i] = pos % POS_SPLIT
    for h in range(2):
        aux = (1 - h) * HEAD_DIM
        kx[h, pos, aux + AUX_BLOCK0 + pos // MOBA_BLOCK] = 1.0
    return jnp.asarray(qx.reshape(HEAD_PAIRS, 2, LANES)), jnp.asarray(kx, dtype=BF16)


def _out_ffn_kernel(x_ref, a_ref, cp_ref, wo_ref, gpm_ref, gpf_ref, gqf_ref,
                    wg_ref, wu_ref, wd_ref, o_ref):
    mixed = (jnp.dot(a_ref[...], wo_ref[0:ATTN_WIDTH, :], preferred_element_type=F32)
             + jnp.dot(cp_ref[...], wo_ref[ATTN_WIDTH:D_MODEL, :], preferred_element_type=F32))
    x1 = x_ref[...] + _rms(mixed, gpm_ref[...])
    hf = _rms(x1, gpf_ref[...]).astype(BF16)
    def gate_up(c):
        cols = slice(c * FFN_CHUNK, (c + 1) * FFN_CHUNK)
        return (jnp.dot(hf, wg_ref[:, cols], preferred_element_type=F32),
                jnp.dot(hf, wu_ref[:, cols], preferred_element_type=F32))

    n_chunks = D_FF // FFN_CHUNK
    ff = jnp.zeros(x1.shape, F32)
    nxt = gate_up(0)
    for c in range(n_chunks):
        gate, up = nxt
        if c + 1 < n_chunks:
            nxt = gate_up(c + 1)
        act = (gate * jax.nn.sigmoid(gate) * up).astype(BF16)
        ff = ff + jnp.dot(act, wd_ref[c * FFN_CHUNK:(c + 1) * FFN_CHUNK, :],
                          preferred_element_type=F32)
    o_ref[...] = x1 + _rms(ff, gqf_ref[...])


def _out_ffn(layer, x, attn, cp, w_out, g_post_mix, g_pre_ffn, g_post_ffn, w_gate, w_up, w_down):
    T = x.shape[0]
    tm = FFN_TM
    const = lambda t: (layer, 0, 0)
    resident = functools.partial(pl.BlockSpec, index_map=const, pipeline_mode=pl.Buffered(1))
    return pl.pallas_call(
        _out_ffn_kernel,
        out_shape=jax.ShapeDtypeStruct((T, D_MODEL), F32),
        grid=(T // tm,),
        in_specs=[
            pl.BlockSpec((tm, D_MODEL), lambda t: (t, 0)),
            pl.BlockSpec((tm, ATTN_WIDTH), lambda t: (t, 0)),
            pl.BlockSpec((tm, CONV_WIDTH + POOL_WIDTH), lambda t: (t, 0)),
            resident((None, D_MODEL, D_MODEL)),
            pl.BlockSpec((None, 1, D_MODEL), const),
            pl.BlockSpec((None, 1, D_MODEL), const),
            pl.BlockSpec((None, 1, D_MODEL), const),
            resident((None, D_MODEL, D_FF)),
            resident((None, D_MODEL, D_FF)),
            resident((None, D_FF, D_MODEL)),
        ],
        out_specs=pl.BlockSpec((tm, D_MODEL), lambda t: (t, 0)),
        compiler_params=pltpu.CompilerParams(
            dimension_semantics=("arbitrary",),
            vmem_limit_bytes=VMEM_LIMIT),
        name="out_ffn",
    )(x, attn, cp, w_out, g_post_mix, g_pre_ffn, g_post_ffn, w_gate, w_up, w_down)


def _block_diag(pool_w):
    L, G, C, _ = pool_w.shape
    eye = jnp.eye(G, dtype=pool_w.dtype)
    return (eye[None, :, None, :, None] * pool_w[:, :, :, None, :]).reshape(L, G * C, G * C)


def kernel(x, w_in, w_out, conv_w, pool_w, pool_scale, g_pre_mix, g_post_mix, g_pre_ffn,
           g_post_ffn, w_gate, w_up, w_down):
    B, S, D = x.shape
    depth = w_in.shape[0]
    assert D == D_MODEL and S % MOBA_BLOCK == 0 and S % IN_TM == 0 and (B * S) % FFN_TM == 0
    qx, kx = _attn_aux(S)
    rows = lambda a: a.reshape(depth, 1, -1)
    w_in, w_out, w_gate, w_up, w_down = (w.astype(BF16) for w in (w_in, w_out, w_gate, w_up, w_down))
    pool_bd = _block_diag(pool_w).astype(BF16)
    g_pre_mix, g_post_mix, g_pre_ffn, g_post_ffn, pool_scale = (
        rows(a) for a in (g_pre_mix, g_post_mix, g_pre_ffn, g_post_ffn, pool_scale))
    for l in range(depth):
        qkv, cp = _in_proj(l, x, g_pre_mix, w_in, conv_w, pool_bd, pool_scale)
        attn = _moba_attn(qkv, qx, kx)
        x = _out_ffn(l, x.reshape(B * S, D), attn.reshape(B * S, ATTN_WIDTH),
                     cp.reshape(B * S, CONV_WIDTH + POOL_WIDTH), w_out,
                     g_post_mix, g_pre_ffn, g_post_ffn, w_gate, w_up, w_down).reshape(B, S, D)
    return x
```

```python
import functools

import jax
import jax.numpy as jnp
import numpy as np
from jax import lax
from jax.experimental import pallas as pl
from jax.experimental.pallas import tpu as pltpu

D_MODEL = 1024
HEAD_DIM = 64
ATTN_WIDTH = 512
ATTN_HEADS = 8
CONV_WIDTH = 256
CONV_K = 3
POOL_WIDTH = 256
POOL_WINDOWS = (2, 4, 8, 16)
POOL_GROUP_DIM = 64
IN_WIDTH = 2560
MOBA_BLOCK = 256
MOBA_TOPK = 3
D_FF = 2816
NORM_EPS = 1e-6

LANES = 128
HEAD_PAIRS = ATTN_WIDTH // LANES
HALO = 16
GATE_ROWS = 16
QKV_WIDTH = 3 * ATTN_WIDTH
CONV_OFF = QKV_WIDTH
POOL_OFF = CONV_OFF + 3 * CONV_WIDTH

IN_TM = 1024
IN_ROWS = 256
FFN_TM = 1024
FFN_ROWS = 256
FFN_CHUNK = 256
VMEM_LIMIT = 56 * 1024 * 1024

BF16 = jnp.bfloat16
F32 = jnp.float32

LOG2E = float(np.log2(np.e))
Q_SCALE = HEAD_DIM ** -0.5 * LOG2E


def _rms(x, g):
    return x * lax.rsqrt(jnp.mean(x * x, axis=-1, keepdims=True) + NORM_EPS) * g


def _in_proj_kernel(x_ref, g_ref, w_ref, cw_ref, pw_ref, ps_ref, qkv_ref, cp_ref,
                    cbuf, pa, pb, pc, pd):
    j = pl.program_id(1)
    tm = IN_ROWS

    @pl.when(j == 0)
    def _():
        cbuf[0:HALO, :] = jnp.zeros((HALO, CONV_WIDTH), F32)
        pa[0:2 * HALO, :] = jnp.zeros((2 * HALO, POOL_WIDTH), F32)
        pb[0:HALO, :] = jnp.zeros((HALO, POOL_WIDTH), F32)
        pc[0:HALO, :] = jnp.zeros((HALO, POOL_WIDTH), F32)
        pd[0:HALO, :] = jnp.zeros((HALO, POOL_WIDTH), F32)

    lane = lax.broadcasted_iota(jnp.int32, (1, POOL_WIDTH), 1)
    g0 = lane < POOL_GROUP_DIM
    g1 = lane < 2 * POOL_GROUP_DIM
    g2 = lane < 3 * POOL_GROUP_DIM
    w2, w4, w8, w16 = (F32(w) for w in POOL_WINDOWS)
    win = jnp.where(g0, w2, jnp.where(g1, w4, jnp.where(g2, w8, w16)))

    def normed(r):
        return _rms(x_ref[0, r, :], g_ref[...]).astype(BF16)

    groups = [slice(r, r + tm) for r in range(0, x_ref.shape[1], tm)]
    h = normed(groups[0])
    for i, r in enumerate(groups):
        cv = jnp.dot(h, w_ref[:, CONV_OFF:POOL_OFF], preferred_element_type=F32)
        up = jnp.dot(h, w_ref[:, POOL_OFF:IN_WIDTH], preferred_element_type=F32)

        q = jnp.dot(h, w_ref[:, 0:ATTN_WIDTH], preferred_element_type=F32) * F32(Q_SCALE)
        qkv_ref[0, r, 0:ATTN_WIDTH] = q.astype(BF16)
        h_next = normed(groups[i + 1]) if i + 1 < len(groups) else None

        h_conv = cv[:, 0:CONV_WIDTH]
        b_gate = cv[:, CONV_WIDTH:2 * CONV_WIDTH]
        c_gate = cv[:, 2 * CONV_WIDTH:3 * CONV_WIDTH]
        u = c_gate * h_conv
        cbuf[HALO:HALO + tm, :] = u
        conv = (cbuf[HALO - 2:HALO - 2 + tm, :] * cw_ref[0:1, :]
                + cbuf[HALO - 1:HALO - 1 + tm, :] * cw_ref[1:2, :]
                + u * cw_ref[2:3, :])
        cp_ref[0, r, 0:CONV_WIDTH] = (b_gate * conv).astype(BF16)
        cbuf[0:HALO, :] = cbuf[tm:tm + HALO, :]

        qkv_ref[0, r, ATTN_WIDTH:QKV_WIDTH] = jnp.dot(
            h, w_ref[:, ATTN_WIDTH:QKV_WIDTH], preferred_element_type=F32).astype(BF16)

        pa[2 * HALO:2 * HALO + tm, :] = up
        n = tm + HALO
        pb[HALO:HALO + n, :] = pa[HALO:HALO + n, :] + pa[HALO - 1:HALO - 1 + n, :]
        pc[HALO:HALO + n, :] = pb[HALO:HALO + n, :] + pb[HALO - 2:HALO - 2 + n, :]
        pd[HALO:HALO + n, :] = pc[HALO:HALO + n, :] + pc[HALO - 4:HALO - 4 + n, :]
        s2 = pb[2 * HALO:2 * HALO + tm, :]
        s4 = pc[2 * HALO:2 * HALO + tm, :]
        s8 = pd[2 * HALO:2 * HALO + tm, :]
        s16 = s8 + pd[2 * HALO - 8:2 * HALO - 8 + tm, :]
        wsum = jnp.where(g0, s2, jnp.where(g1, s4, jnp.where(g2, s8, s16)))
        t1 = (j * x_ref.shape[1] + r.start + 1
              + lax.broadcasted_iota(jnp.int32, (tm, 1), 0)).astype(F32)
        cnt = jnp.minimum(t1, win)
        pooled = (wsum / cnt - up).astype(BF16)
        y = jnp.dot(pooled, pw_ref[...], preferred_element_type=F32) * ps_ref[...]
        cp_ref[0, r, CONV_WIDTH:CONV_WIDTH + POOL_WIDTH] = y.astype(BF16)
        pa[HALO:2 * HALO, :] = pa[tm + HALO:tm + 2 * HALO, :]
        h = h_next


def _in_proj(layer, x, g, w_in, conv_w, pool_bd, pool_scale):
    B, S, _ = x.shape
    tm = IN_TM
    const = lambda b, j: (layer, 0, 0)
    return pl.pallas_call(
        _in_proj_kernel,
        out_shape=(jax.ShapeDtypeStruct((B, S, QKV_WIDTH), BF16),
                   jax.ShapeDtypeStruct((B, S, CONV_WIDTH + POOL_WIDTH), BF16)),
        grid=(B, S // tm),
        in_specs=[
            pl.BlockSpec((1, tm, D_MODEL), lambda b, j: (b, j, 0)),
            pl.BlockSpec((None, 1, D_MODEL), const),
            pl.BlockSpec((None, D_MODEL, IN_WIDTH), const, pipeline_mode=pl.Buffered(1)),
            pl.BlockSpec((None, CONV_K, CONV_WIDTH), const),
            pl.BlockSpec((None, POOL_WIDTH, POOL_WIDTH), const),
            pl.BlockSpec((None, 1, POOL_WIDTH), const),
        ],
        out_specs=(pl.BlockSpec((1, tm, QKV_WIDTH), lambda b, j: (b, j, 0)),
                   pl.BlockSpec((1, tm, CONV_WIDTH + POOL_WIDTH), lambda b, j: (b, j, 0))),
        scratch_shapes=[
            pltpu.VMEM((HALO + IN_ROWS, CONV_WIDTH), F32),
            pltpu.VMEM((2 * HALO + IN_ROWS, POOL_WIDTH), F32),
            pltpu.VMEM((2 * HALO + IN_ROWS, POOL_WIDTH), F32),
            pltpu.VMEM((2 * HALO + IN_ROWS, POOL_WIDTH), F32),
            pltpu.VMEM((2 * HALO + IN_ROWS, POOL_WIDTH), F32),
        ],
        compiler_params=pltpu.CompilerParams(
            dimension_semantics=("arbitrary", "arbitrary"),
            vmem_limit_bytes=VMEM_LIMIT),
        name="in_proj",
    )(x, g, w_in, conv_w, pool_bd, pool_scale)


NEG_BIG = -1e30
ALIBI_PARTS = 3
AUX_ALIBI_HI = 0
AUX_ALIBI_LO = ALIBI_PARTS
AUX_BLOCK0 = 8
VT_ROWS = HEAD_DIM + 16
EXP_LAG = 2
PV_LAG = 4
POS_SPLIT = 16


def _block_mask_t(g, q_blk):
    nb = g.shape[0]
    blk_idx = lax.broadcasted_iota(jnp.int32, g.shape, 0)
    elig = blk_idx < q_blk
    ranks = []
    for jb in range(nb):
        gj = g[jb:jb + 1, :]
        beats = ((g > gj) | ((g == gj) & (blk_idx < jb))) & elig
        ranks.append(jnp.sum(beats.astype(F32), axis=0, keepdims=True))
    rank = jnp.concatenate(ranks, axis=0)
    attend = ((rank < MOBA_TOPK) & elig) | (blk_idx == q_blk)
    return jnp.where(attend, F32(0.0), F32(NEG_BIG))


def _attn_kernel(q_ref, k_ref, v_ref, qx_ref, kx_ref, o_ref, qa_ref, ka_ref, vt_ref):
    S = k_ref.shape[1]
    blk = MOBA_BLOCK
    nb = S // blk
    k = k_ref[0]
    v = v_ref[0]
    lane = lax.broadcasted_iota(jnp.int32, (1, LANES), 1)
    nt = (((1,), (1,)), ((), ()))

    km = jnp.mean(k.astype(F32).reshape(nb, blk, LANES), axis=1)
    km = jnp.concatenate([km, jnp.zeros((GATE_ROWS - nb, LANES), F32)], axis=0)
    km_hi = km.astype(BF16)
    km_lo = (km - km_hi.astype(F32)).astype(BF16)

    gated = min((MOBA_TOPK + 1) * blk, S)
    q_free = q_ref[0, 0:gated, :]
    q_gate = q_ref[0, gated:S, :]
    q_blk = (gated + lax.broadcasted_iota(jnp.int32, (nb, S - gated), 1)) // blk
    in_heads = [(lane >= h * HEAD_DIM) & (lane < (h + 1) * HEAD_DIM) for h in range(2)]
    v_t = v.T
    ones_tile = jnp.where(lax.broadcasted_iota(jnp.int32, (VT_ROWS - HEAD_DIM, S), 0) == 0,
                          F32(1.0), F32(0.0)).astype(BF16)
    for h in range(2):
        alibi = qx_ref[0, h:h + 1, :]
        qa_ref[h, 0:gated, :] = jnp.where(
            in_heads[h], q_free, jnp.broadcast_to(alibi.astype(BF16), q_free.shape))
        ka_ref[h] = jnp.where(in_heads[h], k, kx_ref[h])
        vt_ref[h] = jnp.concatenate([v_t[h * HEAD_DIM:(h + 1) * HEAD_DIM], ones_tile], axis=0)

    def gate_queries(h):
        qh = jnp.where(in_heads[h], q_gate, jnp.zeros_like(q_gate))
        g = (lax.dot_general(km_hi, qh, nt, preferred_element_type=F32)
             + lax.dot_general(km_lo, qh, nt, preferred_element_type=F32))[0:nb]
        aux = (1 - h) * HEAD_DIM + AUX_BLOCK0
        mask_t = jnp.concatenate([jnp.zeros((aux, S - gated), F32), _block_mask_t(g, q_blk),
                                  jnp.zeros((LANES - aux - nb, S - gated), F32)], axis=0)
        extra = mask_t.T + qx_ref[0, h:h + 1, :]
        qa_ref[h, gated:S, :] = jnp.where(in_heads[h], q_gate, extra.astype(BF16))

    key = lax.broadcasted_iota(jnp.int32, (blk, blk), 0)
    qry = lax.broadcasted_iota(jnp.int32, (blk, blk), 1)
    causal = key <= qry

    def scores(c, h):
        lo, hi = c * blk, (c + 1) * blk
        s = lax.dot_general(ka_ref[h, 0:hi, :], qa_ref[h, lo:hi, :], nt,
                            preferred_element_type=F32)
        s_own = jnp.where(causal, s[lo:hi], F32(NEG_BIG))
        m = jnp.max(s_own, axis=0, keepdims=True)
        if not c:
            return [s_own], m
        s_past = s[0:lo]
        return [s_past, s_own], jnp.maximum(m, jnp.max(s_past, axis=0, keepdims=True))

    def probs(parts, m):
        p = [jnp.exp2(s - m).astype(BF16) for s in parts]
        return p[0] if len(p) == 1 else jnp.concatenate(p, axis=0)

    def weighted_values(c, h, p):
        acc = jnp.dot(vt_ref[h, :, 0:(c + 1) * blk], p, preferred_element_type=F32)
        return acc[0:HEAD_DIM] * (1.0 / acc[HEAD_DIM:HEAD_DIM + 1])

    chains = [(c, h) for c in range(nb) for h in range(2)]
    n_free = 2 * (gated // blk)
    gate_after = {n_free // 4: 0, n_free * 5 // 8: 1} if gated < S else {}
    staged_s, staged_p, outs = {}, {}, {}
    for t in range(len(chains) + PV_LAG):
        if t < len(chains):
            staged_s[t] = scores(*chains[t])
        if t in gate_after:
            gate_queries(gate_after[t])
        if 0 <= t - EXP_LAG < len(chains):
            staged_p[t - EXP_LAG] = probs(*staged_s.pop(t - EXP_LAG))
        if 0 <= t - PV_LAG < len(chains):
            c, h = chains[t - PV_LAG]
            outs[h] = weighted_values(c, h, staged_p.pop(t - PV_LAG))
            if h == 1:
                o_ref[0, c * blk:(c + 1) * blk, :] = jnp.concatenate(
                    [outs[0], outs[1]], axis=0).T.astype(BF16)


def _moba_attn(qkv, qx, kx):
    B, S, _ = qkv.shape
    return pl.pallas_call(
        _attn_kernel,
        out_shape=jax.ShapeDtypeStruct((B, S, ATTN_WIDTH), BF16),
        grid=(B, HEAD_PAIRS),
        in_specs=[
            pl.BlockSpec((1, S, LANES), lambda b, hp: (b, 0, hp)),
            pl.BlockSpec((1, S, LANES), lambda b, hp: (b, 0, HEAD_PAIRS + hp)),
            pl.BlockSpec((1, S, LANES), lambda b, hp: (b, 0, 2 * HEAD_PAIRS + hp)),
            pl.BlockSpec((1, 2, LANES), lambda b, hp: (hp, 0, 0)),
            pl.BlockSpec((2, S, LANES), lambda b, hp: (0, 0, 0)),
        ],
        out_specs=pl.BlockSpec((1, S, LANES), lambda b, hp: (b, 0, hp)),
        scratch_shapes=[
            pltpu.VMEM((2, S, LANES), BF16),
            pltpu.VMEM((2, S, LANES), BF16),
            pltpu.VMEM((2, VT_ROWS, S), BF16),
        ],
        compiler_params=pltpu.CompilerParams(
            dimension_semantics=("arbitrary", "arbitrary"),
            vmem_limit_bytes=VMEM_LIMIT),
        name="moba_attn",
    )(qkv, qkv, qkv, qx, kx)


def _attn_aux(S):
    nb = S // MOBA_BLOCK
    assert S // POS_SPLIT <= 128 and 2 * ALIBI_PARTS <= AUX_BLOCK0 and AUX_BLOCK0 + nb <= HEAD_DIM
    slopes = 2.0 ** (-8.0 * np.arange(1, ATTN_HEADS + 1) / ATTN_HEADS)
    rest = (slopes * LOG2E).astype(np.float32)
    pos = np.arange(S)
    qx = np.zeros((ATTN_HEADS, LANES), np.float32)
    kx = np.zeros((2, S, LANES), np.float32)
    for i in range(ALIBI_PARTS):
        part = rest.astype(BF16).astype(np.float32)
        rest = rest - part
        for h in range(2):
            aux = (1 - h) * HEAD_DIM
            qx[h::2, aux + AUX_ALIBI_HI + i] = part[h::2] * POS_SPLIT
            qx[h::2, aux + AUX_ALIBI_LO + i] = part[h::2]
            kx[h, :, aux + AUX_ALIBI_HI + i] = pos // POS_SPLIT
            kx[h, :, aux + AUX_ALIBI_LO + i] = pos % POS_SPLIT
    for h in range(2):
        aux = (1 - h) * HEAD_DIM
        kx[h, pos, aux + AUX_BLOCK0 + pos // MOBA_BLOCK] = 1.0
    return jnp.asarray(qx.reshape(HEAD_PAIRS, 2, LANES)), jnp.asarray(kx, dtype=BF16)


def _out_ffn_kernel(x_ref, a_ref, cp_ref, wo_ref, gpm_ref, gpf_ref, gqf_ref,
                    wg_ref, wu_ref, wd_ref, o_ref):
    groups = [slice(r, r + FFN_ROWS) for r in range(0, x_ref.shape[0], FFN_ROWS)]
    n_chunks = D_FF // FFN_CHUNK

    def mix(r):
        return (jnp.dot(a_ref[r, :], wo_ref[0:ATTN_WIDTH, :], preferred_element_type=F32)
                + jnp.dot(cp_ref[r, :], wo_ref[ATTN_WIDTH:D_MODEL, :], preferred_element_type=F32))

    def norms(r, mixed):
        x1 = x_ref[r, :] + _rms(mixed, gpm_ref[...])
        return x1, _rms(x1, gpf_ref[...]).astype(BF16)

    def gate_up(hf, c):
        cols = slice(c * FFN_CHUNK, (c + 1) * FFN_CHUNK)
        return (jnp.dot(hf, wg_ref[:, cols], preferred_element_type=F32),
                jnp.dot(hf, wu_ref[:, cols], preferred_element_type=F32))

    def down(gu, c):
        gate, up = gu
        act = (gate * jax.nn.sigmoid(gate) * up).astype(BF16)
        return jnp.dot(act, wd_ref[c * FFN_CHUNK:(c + 1) * FFN_CHUNK, :],
                       preferred_element_type=F32)

    def finish(r, x1, ff):
        o_ref[r, :] = x1 + _rms(ff, gqf_ref[...])

    mixed = [mix(r) for r in groups]
    x1, hf, gu = [], [], []
    for r, mx in zip(groups, mixed):
        x1_r, hf_r = norms(r, mx)
        x1.append(x1_r)
        hf.append(hf_r)
        gu.append(gate_up(hf_r, 0))
    ff = [jnp.zeros((FFN_ROWS, D_MODEL), F32) for _ in groups]
    for c in range(n_chunks):
        for i, r in enumerate(groups):
            nxt = gate_up(hf[i], c + 1) if c + 1 < n_chunks else None
            ff[i] = ff[i] + down(gu[i], c)
            gu[i] = nxt
            if c + 1 == n_chunks:
                finish(r, x1[i], ff[i])


def _out_ffn(layer, x, attn, cp, w_out, g_post_mix, g_pre_ffn, g_post_ffn, w_gate, w_up, w_down):
    T = x.shape[0]
    tm = FFN_TM
    const = lambda t: (layer, 0, 0)
    resident = functools.partial(pl.BlockSpec, index_map=const, pipeline_mode=pl.Buffered(1))
    return pl.pallas_call(
        _out_ffn_kernel,
        out_shape=jax.ShapeDtypeStruct((T, D_MODEL), F32),
        grid=(T // tm,),
        in_specs=[
            pl.BlockSpec((tm, D_MODEL), lambda t: (t, 0)),
            pl.BlockSpec((tm, ATTN_WIDTH), lambda t: (t, 0)),
            pl.BlockSpec((tm, CONV_WIDTH + POOL_WIDTH), lambda t: (t, 0)),
            resident((None, D_MODEL, D_MODEL)),
            pl.BlockSpec((None, 1, D_MODEL), const),
            pl.BlockSpec((None, 1, D_MODEL), const),
            pl.BlockSpec((None, 1, D_MODEL), const),
            resident((None, D_MODEL, D_FF)),
            resident((None, D_MODEL, D_FF)),
            resident((None, D_FF, D_MODEL)),
        ],
        out_specs=pl.BlockSpec((tm, D_MODEL), lambda t: (t, 0)),
        compiler_params=pltpu.CompilerParams(
            dimension_semantics=("arbitrary",),
            vmem_limit_bytes=VMEM_LIMIT),
        name="out_ffn",
    )(x, attn, cp, w_out, g_post_mix, g_pre_ffn, g_post_ffn, w_gate, w_up, w_down)


def _block_diag(pool_w):
    L, G, C, _ = pool_w.shape
    eye = jnp.eye(G, dtype=pool_w.dtype)
    return (eye[None, :, None, :, None] * pool_w[:, :, :, None, :]).reshape(L, G * C, G * C)


def kernel(x, w_in, w_out, conv_w, pool_w, pool_scale, g_pre_mix, g_post_mix, g_pre_ffn,
           g_post_ffn, w_gate, w_up, w_down):
    B, S, D = x.shape
    depth = w_in.shape[0]
    assert D == D_MODEL and S % MOBA_BLOCK == 0 and S % IN_TM == 0 and (B * S) % FFN_TM == 0
    qx, kx = _attn_aux(S)
    rows = lambda a: a.reshape(depth, 1, -1)
    w_in, w_out, w_gate, w_up, w_down = (w.astype(BF16) for w in (w_in, w_out, w_gate, w_up, w_down))
    pool_bd = _block_diag(pool_w).astype(BF16)
    g_pre_mix, g_post_mix, g_pre_ffn, g_post_ffn, pool_scale = (
        rows(a) for a in (g_pre_mix, g_post_mix, g_pre_ffn, g_post_ffn, pool_scale))
    for l in range(depth):
        qkv, cp = _in_proj(l, x, g_pre_mix, w_in, conv_w, pool_bd, pool_scale)
        attn = _moba_attn(qkv, qx, kx)
        x = _out_ffn(l, x.reshape(B * S, D), attn.reshape(B * S, ATTN_WIDTH),
                     cp.reshape(B * S, CONV_WIDTH + POOL_WIDTH), w_out,
                     g_post_mix, g_pre_ffn, g_post_ffn, w_gate, w_up, w_down).reshape(B, S, D)
    return x
```

```python
import functools

import jax
import jax.numpy as jnp
import numpy as np
from jax import lax
from jax.experimental import pallas as pl
from jax.experimental.pallas import tpu as pltpu

D_MODEL = 1024
HEAD_DIM = 64
ATTN_WIDTH = 512
ATTN_HEADS = 8
CONV_WIDTH = 256
CONV_K = 3
POOL_WIDTH = 256
POOL_WINDOWS = (2, 4, 8, 16)
POOL_GROUP_DIM = 64
IN_WIDTH = 2560
MOBA_BLOCK = 256
MOBA_TOPK = 3
D_FF = 2816
NORM_EPS = 1e-6

LANES = 128
HEAD_PAIRS = ATTN_WIDTH // LANES
HALO = 16
GATE_ROWS = 16
QKV_WIDTH = 3 * ATTN_WIDTH
CONV_OFF = QKV_WIDTH
POOL_OFF = CONV_OFF + 3 * CONV_WIDTH

IN_TM = 1024
IN_ROWS = 256
FFN_TM = 1024
FFN_ROWS = 256
FFN_CHUNK = 256
VMEM_LIMIT = 56 * 1024 * 1024

BF16 = jnp.bfloat16
F32 = jnp.float32

LOG2E = float(np.log2(np.e))
Q_SCALE = HEAD_DIM ** -0.5 * LOG2E


def _rms(x, g):
    return x * lax.rsqrt(jnp.mean(x * x, axis=-1, keepdims=True) + NORM_EPS) * g


def _store_lane_tiles(ref, first, rows, val):
    for p in range(val.shape[1] // LANES):
        ref[0, first + p, rows, :] = val[:, p * LANES:(p + 1) * LANES]


def _in_proj_kernel(x_ref, g_ref, w_ref, cw_ref, pw_ref, ps_ref, qkv_ref, cp_ref,
                    cbuf, pa, pb, pc, pd):
    j = pl.program_id(1)
    tm = IN_ROWS

    @pl.when(j == 0)
    def _():
        cbuf[0:HALO, :] = jnp.zeros((HALO, CONV_WIDTH), F32)
        pa[0:2 * HALO, :] = jnp.zeros((2 * HALO, POOL_WIDTH), F32)
        pb[0:HALO, :] = jnp.zeros((HALO, POOL_WIDTH), F32)
        pc[0:HALO, :] = jnp.zeros((HALO, POOL_WIDTH), F32)
        pd[0:HALO, :] = jnp.zeros((HALO, POOL_WIDTH), F32)

    lane = lax.broadcasted_iota(jnp.int32, (1, POOL_WIDTH), 1)
    g0 = lane < POOL_GROUP_DIM
    g1 = lane < 2 * POOL_GROUP_DIM
    g2 = lane < 3 * POOL_GROUP_DIM
    w2, w4, w8, w16 = (F32(w) for w in POOL_WINDOWS)
    win = jnp.where(g0, w2, jnp.where(g1, w4, jnp.where(g2, w8, w16)))

    def normed(r):
        return _rms(x_ref[0, r, :], g_ref[...]).astype(BF16)

    groups = [slice(r, r + tm) for r in range(0, x_ref.shape[1], tm)]
    h = normed(groups[0])
    for i, r in enumerate(groups):
        cv = jnp.dot(h, w_ref[:, CONV_OFF:POOL_OFF], preferred_element_type=F32)
        up = jnp.dot(h, w_ref[:, POOL_OFF:IN_WIDTH], preferred_element_type=F32)

        q = jnp.dot(h, w_ref[:, 0:ATTN_WIDTH], preferred_element_type=F32) * F32(Q_SCALE)
        _store_lane_tiles(qkv_ref, 0, r, q.astype(BF16))
        h_next = normed(groups[i + 1]) if i + 1 < len(groups) else None

        h_conv = cv[:, 0:CONV_WIDTH]
        b_gate = cv[:, CONV_WIDTH:2 * CONV_WIDTH]
        c_gate = cv[:, 2 * CONV_WIDTH:3 * CONV_WIDTH]
        u = c_gate * h_conv
        cbuf[HALO:HALO + tm, :] = u
        conv = (cbuf[HALO - 2:HALO - 2 + tm, :] * cw_ref[0:1, :]
                + cbuf[HALO - 1:HALO - 1 + tm, :] * cw_ref[1:2, :]
                + u * cw_ref[2:3, :])
        cp_ref[0, r, 0:CONV_WIDTH] = (b_gate * conv).astype(BF16)
        cbuf[0:HALO, :] = cbuf[tm:tm + HALO, :]

        _store_lane_tiles(qkv_ref, HEAD_PAIRS, r, jnp.dot(
            h, w_ref[:, ATTN_WIDTH:QKV_WIDTH], preferred_element_type=F32).astype(BF16))

        pa[2 * HALO:2 * HALO + tm, :] = up
        n = tm + HALO
        pb[HALO:HALO + n, :] = pa[HALO:HALO + n, :] + pa[HALO - 1:HALO - 1 + n, :]
        pc[HALO:HALO + n, :] = pb[HALO:HALO + n, :] + pb[HALO - 2:HALO - 2 + n, :]
        pd[HALO:HALO + n, :] = pc[HALO:HALO + n, :] + pc[HALO - 4:HALO - 4 + n, :]
        s2 = pb[2 * HALO:2 * HALO + tm, :]
        s4 = pc[2 * HALO:2 * HALO + tm, :]
        s8 = pd[2 * HALO:2 * HALO + tm, :]
        s16 = s8 + pd[2 * HALO - 8:2 * HALO - 8 + tm, :]
        wsum = jnp.where(g0, s2, jnp.where(g1, s4, jnp.where(g2, s8, s16)))
        t1 = (j * x_ref.shape[1] + r.start + 1
              + lax.broadcasted_iota(jnp.int32, (tm, 1), 0)).astype(F32)
        cnt = jnp.minimum(t1, win)
        pooled = (wsum / cnt - up).astype(BF16)
        y = jnp.dot(pooled, pw_ref[...], preferred_element_type=F32) * ps_ref[...]
        cp_ref[0, r, CONV_WIDTH:CONV_WIDTH + POOL_WIDTH] = y.astype(BF16)
        pa[HALO:2 * HALO, :] = pa[tm + HALO:tm + 2 * HALO, :]
        h = h_next


def _in_proj(layer, x, g, w_in, conv_w, pool_bd, pool_scale):
    B, S, _ = x.shape
    tm = IN_TM
    const = lambda b, j: (layer, 0, 0)
    return pl.pallas_call(
        _in_proj_kernel,
        out_shape=(jax.ShapeDtypeStruct((B, 3 * HEAD_PAIRS, S, LANES), BF16),
                   jax.ShapeDtypeStruct((B, S, CONV_WIDTH + POOL_WIDTH), BF16)),
        grid=(B, S // tm),
        in_specs=[
            pl.BlockSpec((1, tm, D_MODEL), lambda b, j: (b, j, 0)),
            pl.BlockSpec((None, 1, D_MODEL), const),
            pl.BlockSpec((None, D_MODEL, IN_WIDTH), const, pipeline_mode=pl.Buffered(1)),
            pl.BlockSpec((None, CONV_K, CONV_WIDTH), const),
            pl.BlockSpec((None, POOL_WIDTH, POOL_WIDTH), const),
            pl.BlockSpec((None, 1, POOL_WIDTH), const),
        ],
        out_specs=(pl.BlockSpec((1, 3 * HEAD_PAIRS, tm, LANES), lambda b, j: (b, 0, j, 0)),
                   pl.BlockSpec((1, tm, CONV_WIDTH + POOL_WIDTH), lambda b, j: (b, j, 0))),
        scratch_shapes=[
            pltpu.VMEM((HALO + IN_ROWS, CONV_WIDTH), F32),
            pltpu.VMEM((2 * HALO + IN_ROWS, POOL_WIDTH), F32),
            pltpu.VMEM((2 * HALO + IN_ROWS, POOL_WIDTH), F32),
            pltpu.VMEM((2 * HALO + IN_ROWS, POOL_WIDTH), F32),
            pltpu.VMEM((2 * HALO + IN_ROWS, POOL_WIDTH), F32),
        ],
        compiler_params=pltpu.CompilerParams(
            dimension_semantics=("arbitrary", "arbitrary"),
            vmem_limit_bytes=VMEM_LIMIT),
        name="in_proj",
    )(x, g, w_in, conv_w, pool_bd, pool_scale)


NEG_BIG = -1e30
ALIBI_PARTS = 3
AUX_ALIBI_HI = 0
AUX_ALIBI_LO = ALIBI_PARTS
AUX_BLOCK0 = 8
VT_ROWS = HEAD_DIM + 16
EXP_LAG = 2
PV_LAG = 4
POS_SPLIT = 16


def _block_mask_t(g, q_blk):
    nb = g.shape[0]
    blk_idx = lax.broadcasted_iota(jnp.int32, g.shape, 0)
    elig = blk_idx < q_blk
    ranks = []
    for jb in range(nb):
        gj = g[jb:jb + 1, :]
        beats = ((g > gj) | ((g == gj) & (blk_idx < jb))) & elig
        ranks.append(jnp.sum(beats.astype(F32), axis=0, keepdims=True))
    rank = jnp.concatenate(ranks, axis=0)
    attend = ((rank < MOBA_TOPK) & elig) | (blk_idx == q_blk)
    return jnp.where(attend, F32(0.0), F32(NEG_BIG))


def _attn_kernel(q_ref, k_ref, v_ref, qx_ref, kx_ref, o_ref, qa_ref, ka_ref, vt_ref):
    S = k_ref.shape[1]
    blk = MOBA_BLOCK
    nb = S // blk
    k = k_ref[0]
    v = v_ref[0]
    lane = lax.broadcasted_iota(jnp.int32, (1, LANES), 1)
    nt = (((1,), (1,)), ((), ()))

    km = jnp.mean(k.astype(F32).reshape(nb, blk, LANES), axis=1)
    km = jnp.concatenate([km, jnp.zeros((GATE_ROWS - nb, LANES), F32)], axis=0)
    km_hi = km.astype(BF16)
    km_lo = (km - km_hi.astype(F32)).astype(BF16)

    gated = min((MOBA_TOPK + 1) * blk, S)
    q_free = q_ref[0, 0:gated, :]
    q_gate = q_ref[0, gated:S, :]
    q_blk = (gated + lax.broadcasted_iota(jnp.int32, (nb, S - gated), 1)) // blk
    in_heads = [(lane >= h * HEAD_DIM) & (lane < (h + 1) * HEAD_DIM) for h in range(2)]
    v_t = v.T
    ones_tile = jnp.where(lax.broadcasted_iota(jnp.int32, (VT_ROWS - HEAD_DIM, S), 0) == 0,
                          F32(1.0), F32(0.0)).astype(BF16)
    for h in range(2):
        alibi = qx_ref[0, h:h + 1, :]
        qa_ref[h, 0:gated, :] = jnp.where(
            in_heads[h], q_free, jnp.broadcast_to(alibi.astype(BF16), q_free.shape))
        ka_ref[h] = jnp.where(in_heads[h], k, kx_ref[h])
        vt_ref[h] = jnp.concatenate([v_t[h * HEAD_DIM:(h + 1) * HEAD_DIM], ones_tile], axis=0)

    def gate_queries(h):
        qh = jnp.where(in_heads[h], q_gate, jnp.zeros_like(q_gate))
        g = (lax.dot_general(km_hi, qh, nt, preferred_element_type=F32)
             + lax.dot_general(km_lo, qh, nt, preferred_element_type=F32))[0:nb]
        aux = (1 - h) * HEAD_DIM + AUX_BLOCK0
        mask_t = jnp.concatenate([jnp.zeros((aux, S - gated), F32), _block_mask_t(g, q_blk),
                                  jnp.zeros((LANES - aux - nb, S - gated), F32)], axis=0)
        extra = mask_t.T + qx_ref[0, h:h + 1, :]
        qa_ref[h, gated:S, :] = jnp.where(in_heads[h], q_gate, extra.astype(BF16))

    key = lax.broadcasted_iota(jnp.int32, (blk, blk), 0)
    qry = lax.broadcasted_iota(jnp.int32, (blk, blk), 1)
    causal = key <= qry

    def scores(c, h):
        lo, hi = c * blk, (c + 1) * blk
        s = lax.dot_general(ka_ref[h, 0:hi, :], qa_ref[h, lo:hi, :], nt,
                            preferred_element_type=F32)
        s_own = jnp.where(causal, s[lo:hi], F32(NEG_BIG))
        m = jnp.max(s_own, axis=0, keepdims=True)
        if not c:
            return [s_own], m
        s_past = s[0:lo]
        return [s_past, s_own], jnp.maximum(m, jnp.max(s_past, axis=0, keepdims=True))

    def probs(parts, m):
        p = [jnp.exp2(s - m).astype(BF16) for s in parts]
        return p[0] if len(p) == 1 else jnp.concatenate(p, axis=0)

    def weighted_values(c, h, p):
        acc = jnp.dot(vt_ref[h, :, 0:(c + 1) * blk], p, preferred_element_type=F32)
        return acc[0:HEAD_DIM] * (1.0 / acc[HEAD_DIM:HEAD_DIM + 1])

    chains = [(c, h) for c in range(nb) for h in range(2)]
    n_free = 2 * (gated // blk)
    gate_after = {n_free // 4: 0, n_free * 5 // 8: 1} if gated < S else {}
    staged_s, staged_p, outs = {}, {}, {}
    for t in range(len(chains) + PV_LAG):
        if t < len(chains):
            staged_s[t] = scores(*chains[t])
        if t in gate_after:
            gate_queries(gate_after[t])
        if 0 <= t - EXP_LAG < len(chains):
            staged_p[t - EXP_LAG] = probs(*staged_s.pop(t - EXP_LAG))
        if 0 <= t - PV_LAG < len(chains):
            c, h = chains[t - PV_LAG]
            outs[h] = weighted_values(c, h, staged_p.pop(t - PV_LAG))
            if h == 1:
                o_ref[0, c * blk:(c + 1) * blk, :] = jnp.concatenate(
                    [outs[0], outs[1]], axis=0).T.astype(BF16)


def _moba_attn(qkv, qx, kx):
    B, _, S, _ = qkv.shape
    return pl.pallas_call(
        _attn_kernel,
        out_shape=jax.ShapeDtypeStruct((B, HEAD_PAIRS, S, LANES), BF16),
        grid=(B, HEAD_PAIRS),
        in_specs=[
            pl.BlockSpec((None, 1, S, LANES), lambda b, hp: (b, hp, 0, 0)),
            pl.BlockSpec((None, 1, S, LANES), lambda b, hp: (b, HEAD_PAIRS + hp, 0, 0)),
            pl.BlockSpec((None, 1, S, LANES), lambda b, hp: (b, 2 * HEAD_PAIRS + hp, 0, 0)),
            pl.BlockSpec((1, 2, LANES), lambda b, hp: (hp, 0, 0)),
            pl.BlockSpec((2, S, LANES), lambda b, hp: (0, 0, 0)),
        ],
        out_specs=pl.BlockSpec((None, 1, S, LANES), lambda b, hp: (b, hp, 0, 0)),
        scratch_shapes=[
            pltpu.VMEM((2, S, LANES), BF16),
            pltpu.VMEM((2, S, LANES), BF16),
            pltpu.VMEM((2, VT_ROWS, S), BF16),
        ],
        compiler_params=pltpu.CompilerParams(
            dimension_semantics=("arbitrary", "arbitrary"),
            vmem_limit_bytes=VMEM_LIMIT),
        name="moba_attn",
    )(qkv, qkv, qkv, qx, kx)


def _attn_aux(S):
    nb = S // MOBA_BLOCK
    assert S // POS_SPLIT <= 128 and 2 * ALIBI_PARTS <= AUX_BLOCK0 and AUX_BLOCK0 + nb <= HEAD_DIM
    slopes = 2.0 ** (-8.0 * np.arange(1, ATTN_HEADS + 1) / ATTN_HEADS)
    rest = (slopes * LOG2E).astype(np.float32)
    pos = np.arange(S)
    qx = np.zeros((ATTN_HEADS, LANES), np.float32)
    kx = np.zeros((2, S, LANES), np.float32)
    for i in range(ALIBI_PARTS):
        part = rest.astype(BF16).astype(np.float32)
        rest = rest - part
        for h in range(2):
            aux = (1 - h) * HEAD_DIM
            qx[h::2, aux + AUX_ALIBI_HI + i] = part[h::2] * POS_SPLIT
            qx[h::2, aux + AUX_ALIBI_LO + i] = part[h::2]
            kx[h, :, aux + AUX_ALIBI_HI + i] = pos // POS_SPLIT
            kx[h, :, aux + AUX_ALIBI_LO + i] = pos % POS_SPLIT
    for h in range(2):
        aux = (1 - h) * HEAD_DIM
        kx[h, pos, aux + AUX_BLOCK0 + pos // MOBA_BLOCK] = 1.0
    return jnp.asarray(qx.reshape(HEAD_PAIRS, 2, LANES)), jnp.asarray(kx, dtype=BF16)


def _out_ffn_kernel(x_ref, a_ref, cp_ref, wo_ref, gpm_ref, gpf_ref, gqf_ref,
                    wg_ref, wu_ref, wd_ref, o_ref):
    groups = [slice(r, r + FFN_ROWS) for r in range(0, x_ref.shape[0], FFN_ROWS)]
    n_chunks = D_FF // FFN_CHUNK

    def mix(r):
        attn = jnp.concatenate([a_ref[p, r, :] for p in range(HEAD_PAIRS)], axis=1)
        return (jnp.dot(attn, wo_ref[0:ATTN_WIDTH, :], preferred_element_type=F32)
                + jnp.dot(cp_ref[r, :], wo_ref[ATTN_WIDTH:D_MODEL, :], preferred_element_type=F32))

    def norms(r, mixed):
        x1 = x_ref[r, :] + _rms(mixed, gpm_ref[...])
        return x1, _rms(x1, gpf_ref[...]).astype(BF16)

    def gate_up(hf, c):
        cols = slice(c * FFN_CHUNK, (c + 1) * FFN_CHUNK)
        return (jnp.dot(hf, wg_ref[:, cols], preferred_element_type=F32),
                jnp.dot(hf, wu_ref[:, cols], preferred_element_type=F32))

    def down(gu, c):
        gate, up = gu
        act = (gate * jax.nn.sigmoid(gate) * up).astype(BF16)
        return jnp.dot(act, wd_ref[c * FFN_CHUNK:(c + 1) * FFN_CHUNK, :],
                       preferred_element_type=F32)

    def finish(r, x1, ff):
        o_ref[r, :] = x1 + _rms(ff, gqf_ref[...])

    mixed = [mix(r) for r in groups]
    x1, hf, gu = [], [], []
    for r, mx in zip(groups, mixed):
        x1_r, hf_r = norms(r, mx)
        x1.append(x1_r)
        hf.append(hf_r)
        gu.append(gate_up(hf_r, 0))
    ff = [jnp.zeros((FFN_ROWS, D_MODEL), F32) for _ in groups]
    for c in range(n_chunks):
        for i, r in enumerate(groups):
            nxt = gate_up(hf[i], c + 1) if c + 1 < n_chunks else None
            ff[i] = ff[i] + down(gu[i], c)
            gu[i] = nxt
            if c + 1 == n_chunks:
                finish(r, x1[i], ff[i])


def _out_ffn(layer, x, attn, cp, w_out, g_post_mix, g_pre_ffn, g_post_ffn, w_gate, w_up, w_down):
    T = x.shape[0]
    tm = FFN_TM
    tiles_per_seq = attn.shape[2] // tm
    const = lambda t: (layer, 0, 0)
    resident = functools.partial(pl.BlockSpec, index_map=const, pipeline_mode=pl.Buffered(1))
    return pl.pallas_call(
        _out_ffn_kernel,
        out_shape=jax.ShapeDtypeStruct((T, D_MODEL), F32),
        grid=(T // tm,),
        in_specs=[
            pl.BlockSpec((tm, D_MODEL), lambda t: (t, 0)),
            pl.BlockSpec((None, HEAD_PAIRS, tm, LANES),
                         lambda t: (t // tiles_per_seq, 0, t % tiles_per_seq, 0)),
            pl.BlockSpec((tm, CONV_WIDTH + POOL_WIDTH), lambda t: (t, 0)),
            resident((None, D_MODEL, D_MODEL)),
            pl.BlockSpec((None, 1, D_MODEL), const),
            pl.BlockSpec((None, 1, D_MODEL), const),
            pl.BlockSpec((None, 1, D_MODEL), const),
            resident((None, D_MODEL, D_FF)),
            resident((None, D_MODEL, D_FF)),
            resident((None, D_FF, D_MODEL)),
        ],
        out_specs=pl.BlockSpec((tm, D_MODEL), lambda t: (t, 0)),
        compiler_params=pltpu.CompilerParams(
            dimension_semantics=("arbitrary",),
            vmem_limit_bytes=VMEM_LIMIT),
        name="out_ffn",
    )(x, attn, cp, w_out, g_post_mix, g_pre_ffn, g_post_ffn, w_gate, w_up, w_down)


def _block_diag(pool_w):
    L, G, C, _ = pool_w.shape
    eye = jnp.eye(G, dtype=pool_w.dtype)
    return (eye[None, :, None, :, None] * pool_w[:, :, :, None, :]).reshape(L, G * C, G * C)


def kernel(x, w_in, w_out, conv_w, pool_w, pool_scale, g_pre_mix, g_post_mix, g_pre_ffn,
           g_post_ffn, w_gate, w_up, w_down):
    B, S, D = x.shape
    depth = w_in.shape[0]
    assert D == D_MODEL and S % MOBA_BLOCK == 0 and S % IN_TM == 0 and S % FFN_TM == 0
    qx, kx = _attn_aux(S)
    rows = lambda a: a.reshape(depth, 1, -1)
    w_in, w_out, w_gate, w_up, w_down = (w.astype(BF16) for w in (w_in, w_out, w_gate, w_up, w_down))
    pool_bd = _block_diag(pool_w).astype(BF16)
    g_pre_mix, g_post_mix, g_pre_ffn, g_post_ffn, pool_scale = (
        rows(a) for a in (g_pre_mix, g_post_mix, g_pre_ffn, g_post_ffn, pool_scale))
    for l in range(depth):
        qkv, cp = _in_proj(l, x, g_pre_mix, w_in, conv_w, pool_bd, pool_scale)
        attn = _moba_attn(qkv, qx, kx)
        x = _out_ffn(l, x.reshape(B * S, D), attn,
                     cp.reshape(B * S, CONV_WIDTH + POOL_WIDTH), w_out,
                     g_post_mix, g_pre_ffn, g_post_ffn, w_gate, w_up, w_down).reshape(B, S, D)
    return x
```

```python
import functools

import jax
import jax.numpy as jnp
import numpy as np
from jax import lax
from jax.experimental import pallas as pl
from jax.experimental.pallas import tpu as pltpu

D_MODEL = 1024
HEAD_DIM = 64
ATTN_WIDTH = 512
ATTN_HEADS = 8
CONV_WIDTH = 256
CONV_K = 3
POOL_WIDTH = 256
POOL_WINDOWS = (2, 4, 8, 16)
POOL_GROUP_DIM = 64
IN_WIDTH = 2560
MOBA_BLOCK = 256
MOBA_TOPK = 3
D_FF = 2816
NORM_EPS = 1e-6

LANES = 128
HEAD_PAIRS = ATTN_WIDTH // LANES
HALO = 16
GATE_ROWS = 16
QKV_WIDTH = 3 * ATTN_WIDTH
CONV_OFF = QKV_WIDTH
POOL_OFF = CONV_OFF + 3 * CONV_WIDTH

IN_TM = 1024
IN_ROWS = 256
FFN_TM = 1024
FFN_ROWS = 256
FFN_CHUNK = 256
FFN_STAGE_ROWS = 128
VMEM_LIMIT = 56 * 1024 * 1024

BF16 = jnp.bfloat16
F32 = jnp.float32

LOG2E = float(np.log2(np.e))
Q_SCALE = HEAD_DIM ** -0.5 * LOG2E


def _rms(x, g):
    return x * lax.rsqrt(jnp.mean(x * x, axis=-1, keepdims=True) + NORM_EPS) * g


def _in_proj_kernel(x_ref, g_ref, w_ref, cw_ref, pw_ref, ps_ref, qkv_ref, cp_ref,
                    cbuf, pa, pb, pc, pd):
    j = pl.program_id(1)
    tm = IN_ROWS

    @pl.when(j == 0)
    def _():
        cbuf[0:HALO, :] = jnp.zeros((HALO, CONV_WIDTH), F32)
        pa[0:2 * HALO, :] = jnp.zeros((2 * HALO, POOL_WIDTH), F32)
        pb[0:HALO, :] = jnp.zeros((HALO, POOL_WIDTH), F32)
        pc[0:HALO, :] = jnp.zeros((HALO, POOL_WIDTH), F32)
        pd[0:HALO, :] = jnp.zeros((HALO, POOL_WIDTH), F32)

    lane = lax.broadcasted_iota(jnp.int32, (1, POOL_WIDTH), 1)
    g0 = lane < POOL_GROUP_DIM
    g1 = lane < 2 * POOL_GROUP_DIM
    g2 = lane < 3 * POOL_GROUP_DIM
    w2, w4, w8, w16 = (F32(w) for w in POOL_WINDOWS)
    win = jnp.where(g0, w2, jnp.where(g1, w4, jnp.where(g2, w8, w16)))

    def normed(r):
        return _rms(x_ref[0, r, :], g_ref[...]).astype(BF16)

    groups = [slice(r, r + tm) for r in range(0, x_ref.shape[1], tm)]
    h = normed(groups[0])
    for i, r in enumerate(groups):
        cv = jnp.dot(h, w_ref[:, CONV_OFF:POOL_OFF], preferred_element_type=F32)
        up = jnp.dot(h, w_ref[:, POOL_OFF:IN_WIDTH], preferred_element_type=F32)

        q = jnp.dot(h, w_ref[:, 0:ATTN_WIDTH], preferred_element_type=F32) * F32(Q_SCALE)
        qkv_ref[0, r, 0:ATTN_WIDTH] = q.astype(BF16)
        h_next = normed(groups[i + 1]) if i + 1 < len(groups) else None

        h_conv = cv[:, 0:CONV_WIDTH]
        b_gate = cv[:, CONV_WIDTH:2 * CONV_WIDTH]
        c_gate = cv[:, 2 * CONV_WIDTH:3 * CONV_WIDTH]
        u = c_gate * h_conv
        cbuf[HALO:HALO + tm, :] = u
        conv = (cbuf[HALO - 2:HALO - 2 + tm, :] * cw_ref[0:1, :]
                + cbuf[HALO - 1:HALO - 1 + tm, :] * cw_ref[1:2, :]
                + u * cw_ref[2:3, :])
        cp_ref[0, r, 0:CONV_WIDTH] = (b_gate * conv).astype(BF16)
        cbuf[0:HALO, :] = cbuf[tm:tm + HALO, :]

        qkv_ref[0, r, ATTN_WIDTH:QKV_WIDTH] = jnp.dot(
            h, w_ref[:, ATTN_WIDTH:QKV_WIDTH], preferred_element_type=F32).astype(BF16)

        pa[2 * HALO:2 * HALO + tm, :] = up
        n = tm + HALO
        pb[HALO:HALO + n, :] = pa[HALO:HALO + n, :] + pa[HALO - 1:HALO - 1 + n, :]
        pc[HALO:HALO + n, :] = pb[HALO:HALO + n, :] + pb[HALO - 2:HALO - 2 + n, :]
        pd[HALO:HALO + n, :] = pc[HALO:HALO + n, :] + pc[HALO - 4:HALO - 4 + n, :]
        s2 = pb[2 * HALO:2 * HALO + tm, :]
        s4 = pc[2 * HALO:2 * HALO + tm, :]
        s8 = pd[2 * HALO:2 * HALO + tm, :]
        s16 = s8 + pd[2 * HALO - 8:2 * HALO - 8 + tm, :]
        wsum = jnp.where(g0, s2, jnp.where(g1, s4, jnp.where(g2, s8, s16)))
        t1 = (j * x_ref.shape[1] + r.start + 1
              + lax.broadcasted_iota(jnp.int32, (tm, 1), 0)).astype(F32)
        cnt = jnp.minimum(t1, win)
        pooled = (wsum / cnt - up).astype(BF16)
        y = jnp.dot(pooled, pw_ref[...], preferred_element_type=F32) * ps_ref[...]
        cp_ref[0, r, CONV_WIDTH:CONV_WIDTH + POOL_WIDTH] = y.astype(BF16)
        pa[HALO:2 * HALO, :] = pa[tm + HALO:tm + 2 * HALO, :]
        h = h_next


def _in_proj(layer, x, g, w_in, conv_w, pool_bd, pool_scale):
    B, S, _ = x.shape
    tm = IN_TM
    const = lambda b, j: (layer, 0, 0)
    return pl.pallas_call(
        _in_proj_kernel,
        out_shape=(jax.ShapeDtypeStruct((B, S, QKV_WIDTH), BF16),
                   jax.ShapeDtypeStruct((B, S, CONV_WIDTH + POOL_WIDTH), BF16)),
        grid=(B, S // tm),
        in_specs=[
            pl.BlockSpec((1, tm, D_MODEL), lambda b, j: (b, j, 0)),
            pl.BlockSpec((None, 1, D_MODEL), const),
            pl.BlockSpec((None, D_MODEL, IN_WIDTH), const, pipeline_mode=pl.Buffered(1)),
            pl.BlockSpec((None, CONV_K, CONV_WIDTH), const),
            pl.BlockSpec((None, POOL_WIDTH, POOL_WIDTH), const),
            pl.BlockSpec((None, 1, POOL_WIDTH), const),
        ],
        out_specs=(pl.BlockSpec((1, tm, QKV_WIDTH), lambda b, j: (b, j, 0)),
                   pl.BlockSpec((1, tm, CONV_WIDTH + POOL_WIDTH), lambda b, j: (b, j, 0))),
        scratch_shapes=[
            pltpu.VMEM((HALO + IN_ROWS, CONV_WIDTH), F32),
            pltpu.VMEM((2 * HALO + IN_ROWS, POOL_WIDTH), F32),
            pltpu.VMEM((2 * HALO + IN_ROWS, POOL_WIDTH), F32),
            pltpu.VMEM((2 * HALO + IN_ROWS, POOL_WIDTH), F32),
            pltpu.VMEM((2 * HALO + IN_ROWS, POOL_WIDTH), F32),
        ],
        compiler_params=pltpu.CompilerParams(
            dimension_semantics=("arbitrary", "arbitrary"),
            vmem_limit_bytes=VMEM_LIMIT),
        name="in_proj",
    )(x, g, w_in, conv_w, pool_bd, pool_scale)


NEG_BIG = -1e30
ALIBI_PARTS = 3
AUX_ALIBI_HI = 0
AUX_ALIBI_LO = ALIBI_PARTS
AUX_BLOCK0 = 8
VT_ROWS = HEAD_DIM + 16
EXP_LAG = 2
PV_LAG = 4
POS_SPLIT = 16


def _block_mask_t(g, q_blk):
    nb = g.shape[0]
    blk_idx = lax.broadcasted_iota(jnp.int32, g.shape, 0)
    elig = blk_idx < q_blk
    ranks = []
    for jb in range(nb):
        gj = g[jb:jb + 1, :]
        beats = ((g > gj) | ((g == gj) & (blk_idx < jb))) & elig
        ranks.append(jnp.sum(beats.astype(F32), axis=0, keepdims=True))
    rank = jnp.concatenate(ranks, axis=0)
    attend = ((rank < MOBA_TOPK) & elig) | (blk_idx == q_blk)
    return jnp.where(attend, F32(0.0), F32(NEG_BIG))


def _attn_kernel(q_ref, k_ref, v_ref, qx_ref, kx_ref, o_ref, qa_ref, ka_ref, vt_ref):
    S = k_ref.shape[1]
    blk = MOBA_BLOCK
    nb = S // blk
    k = k_ref[0]
    v = v_ref[0]
    lane = lax.broadcasted_iota(jnp.int32, (1, LANES), 1)
    nt = (((1,), (1,)), ((), ()))

    km = jnp.mean(k.astype(F32).reshape(nb, blk, LANES), axis=1)
    km = jnp.concatenate([km, jnp.zeros((GATE_ROWS - nb, LANES), F32)], axis=0)
    km_hi = km.astype(BF16)
    km_lo = (km - km_hi.astype(F32)).astype(BF16)

    gated = min((MOBA_TOPK + 1) * blk, S)
    q_free = q_ref[0, 0:gated, :]
    q_gate = q_ref[0, gated:S, :]
    q_blk = (gated + lax.broadcasted_iota(jnp.int32, (nb, S - gated), 1)) // blk
    in_heads = [(lane >= h * HEAD_DIM) & (lane < (h + 1) * HEAD_DIM) for h in range(2)]
    v_t = v.T
    ones_tile = jnp.where(lax.broadcasted_iota(jnp.int32, (VT_ROWS - HEAD_DIM, S), 0) == 0,
                          F32(1.0), F32(0.0)).astype(BF16)
    for h in range(2):
        alibi = qx_ref[0, h:h + 1, :]
        qa_ref[h, 0:gated, :] = jnp.where(
            in_heads[h], q_free, jnp.broadcast_to(alibi.astype(BF16), q_free.shape))
        ka_ref[h] = jnp.where(in_heads[h], k, kx_ref[h])
        vt_ref[h] = jnp.concatenate([v_t[h * HEAD_DIM:(h + 1) * HEAD_DIM], ones_tile], axis=0)

    def gate_queries(h):
        qh = jnp.where(in_heads[h], q_gate, jnp.zeros_like(q_gate))
        g = (lax.dot_general(km_hi, qh, nt, preferred_element_type=F32)
             + lax.dot_general(km_lo, qh, nt, preferred_element_type=F32))[0:nb]
        aux = (1 - h) * HEAD_DIM + AUX_BLOCK0
        mask_t = jnp.concatenate([jnp.zeros((aux, S - gated), F32), _block_mask_t(g, q_blk),
                                  jnp.zeros((LANES - aux - nb, S - gated), F32)], axis=0)
        extra = mask_t.T + qx_ref[0, h:h + 1, :]
        qa_ref[h, gated:S, :] = jnp.where(in_heads[h], q_gate, extra.astype(BF16))

    key = lax.broadcasted_iota(jnp.int32, (blk, blk), 0)
    qry = lax.broadcasted_iota(jnp.int32, (blk, blk), 1)
    causal = key <= qry

    def scores(c, h):
        lo, hi = c * blk, (c + 1) * blk
        s = lax.dot_general(ka_ref[h, 0:hi, :], qa_ref[h, lo:hi, :], nt,
                            preferred_element_type=F32)
        s_own = jnp.where(causal, s[lo:hi], F32(NEG_BIG))
        m = jnp.max(s_own, axis=0, keepdims=True)
        if not c:
            return [s_own], m
        s_past = s[0:lo]
        return [s_past, s_own], jnp.maximum(m, jnp.max(s_past, axis=0, keepdims=True))

    def probs(parts, m):
        p = [jnp.exp2(s - m).astype(BF16) for s in parts]
        return p[0] if len(p) == 1 else jnp.concatenate(p, axis=0)

    def weighted_values(c, h, p):
        acc = jnp.dot(vt_ref[h, :, 0:(c + 1) * blk], p, preferred_element_type=F32)
        return acc[0:HEAD_DIM] * (1.0 / acc[HEAD_DIM:HEAD_DIM + 1])

    chains = [(c, h) for c in range(nb) for h in range(2)]
    n_free = 2 * (gated // blk)
    gate_after = {n_free // 4: 0, n_free * 5 // 8: 1} if gated < S else {}
    staged_s, staged_p, outs = {}, {}, {}
    for t in range(len(chains) + PV_LAG):
        if t < len(chains):
            staged_s[t] = scores(*chains[t])
        if t in gate_after:
            gate_queries(gate_after[t])
        if 0 <= t - EXP_LAG < len(chains):
            staged_p[t - EXP_LAG] = probs(*staged_s.pop(t - EXP_LAG))
        if 0 <= t - PV_LAG < len(chains):
            c, h = chains[t - PV_LAG]
            outs[h] = weighted_values(c, h, staged_p.pop(t - PV_LAG))
            if h == 1:
                o_ref[0, c * blk:(c + 1) * blk, :] = jnp.concatenate(
                    [outs[0], outs[1]], axis=0).T.astype(BF16)


def _moba_attn(qkv, qx, kx):
    B, S, _ = qkv.shape
    return pl.pallas_call(
        _attn_kernel,
        out_shape=jax.ShapeDtypeStruct((B, S, ATTN_WIDTH), BF16),
        grid=(B, HEAD_PAIRS),
        in_specs=[
            pl.BlockSpec((1, S, LANES), lambda b, hp: (b, 0, hp)),
            pl.BlockSpec((1, S, LANES), lambda b, hp: (b, 0, HEAD_PAIRS + hp)),
            pl.BlockSpec((1, S, LANES), lambda b, hp: (b, 0, 2 * HEAD_PAIRS + hp)),
            pl.BlockSpec((1, 2, LANES), lambda b, hp: (hp, 0, 0)),
            pl.BlockSpec((2, S, LANES), lambda b, hp: (0, 0, 0)),
        ],
        out_specs=pl.BlockSpec((1, S, LANES), lambda b, hp: (b, 0, hp)),
        scratch_shapes=[
            pltpu.VMEM((2, S, LANES), BF16),
            pltpu.VMEM((2, S, LANES), BF16),
            pltpu.VMEM((2, VT_ROWS, S), BF16),
        ],
        compiler_params=pltpu.CompilerParams(
            dimension_semantics=("arbitrary", "arbitrary"),
            vmem_limit_bytes=VMEM_LIMIT),
        name="moba_attn",
    )(qkv, qkv, qkv, qx, kx)


def _attn_aux(S):
    nb = S // MOBA_BLOCK
    assert S // POS_SPLIT <= 128 and 2 * ALIBI_PARTS <= AUX_BLOCK0 and AUX_BLOCK0 + nb <= HEAD_DIM
    slopes = 2.0 ** (-8.0 * np.arange(1, ATTN_HEADS + 1) / ATTN_HEADS)
    rest = (slopes * LOG2E).astype(np.float32)
    pos = np.arange(S)
    qx = np.zeros((ATTN_HEADS, LANES), np.float32)
    kx = np.zeros((2, S, LANES), np.float32)
    for i in range(ALIBI_PARTS):
        part = rest.astype(BF16).astype(np.float32)
        rest = rest - part
        for h in range(2):
            aux = (1 - h) * HEAD_DIM
            qx[h::2, aux + AUX_ALIBI_HI + i] = part[h::2] * POS_SPLIT
            qx[h::2, aux + AUX_ALIBI_LO + i] = part[h::2]
            kx[h, :, aux + AUX_ALIBI_HI + i] = pos // POS_SPLIT
            kx[h, :, aux + AUX_ALIBI_LO + i] = pos % POS_SPLIT
    for h in range(2):
        aux = (1 - h) * HEAD_DIM
        kx[h, pos, aux + AUX_BLOCK0 + pos // MOBA_BLOCK] = 1.0
    return jnp.asarray(qx.reshape(HEAD_PAIRS, 2, LANES)), jnp.asarray(kx, dtype=BF16)


def _stage_weights(layer, jobs):
    chunks = []
    uses = {}
    for src, dst, rows, stage, sems in jobs:
        for r in range(0, src.shape[1], rows):
            slot = uses.get(id(stage), 0) % 2
            uses[id(stage)] = uses.get(id(stage), 0) + 1
            copy = pltpu.make_async_copy(src.at[layer, pl.ds(r, rows), :], stage.at[slot],
                                         sems.at[slot])
            chunks.append((copy, stage, slot, dst, r, rows))
    chunks[0][0].start()
    for k, (copy, stage, slot, dst, r, rows) in enumerate(chunks):
        if k + 1 < len(chunks):
            chunks[k + 1][0].start()
        copy.wait()
        dst[r:r + rows, :] = stage[slot].astype(BF16)


def _out_ffn_kernel(layer, x_ref, a_ref, cp_ref, wo_hbm, gpm_ref, gpf_ref, gqf_ref,
                    wg_hbm, wu_hbm, wd_hbm, o_ref,
                    wo_ref, wg_ref, wu_ref, wd_ref, st_sq, st_wide, sem_sq, sem_wide):
    @pl.when(pl.program_id(0) == 0)
    def _():
        _stage_weights(layer, [
            (wo_hbm, wo_ref, FFN_STAGE_ROWS, st_sq, sem_sq),
            (wg_hbm, wg_ref, FFN_STAGE_ROWS // 2, st_wide, sem_wide),
            (wu_hbm, wu_ref, FFN_STAGE_ROWS // 2, st_wide, sem_wide),
            (wd_hbm, wd_ref, FFN_STAGE_ROWS, st_sq, sem_sq),
        ])

    @pl.when(pl.program_id(0) > 0)
    def _():
        _out_ffn_tile(x_ref, a_ref, cp_ref, wo_ref, gpm_ref, gpf_ref, gqf_ref,
                      wg_ref, wu_ref, wd_ref, o_ref)


def _out_ffn_tile(x_ref, a_ref, cp_ref, wo_ref, gpm_ref, gpf_ref, gqf_ref,
                  wg_ref, wu_ref, wd_ref, o_ref):
    groups = [slice(r, r + FFN_ROWS) for r in range(0, x_ref.shape[0], FFN_ROWS)]
    n_chunks = D_FF // FFN_CHUNK

    def mix(r):
        return (jnp.dot(a_ref[r, :], wo_ref[0:ATTN_WIDTH, :], preferred_element_type=F32)
                + jnp.dot(cp_ref[r, :], wo_ref[ATTN_WIDTH:D_MODEL, :], preferred_element_type=F32))

    def norms(r, mixed):
        x1 = x_ref[r, :] + _rms(mixed, gpm_ref[...])
        return x1, _rms(x1, gpf_ref[...]).astype(BF16)

    def gate_up(hf, c):
        cols = slice(c * FFN_CHUNK, (c + 1) * FFN_CHUNK)
        return (jnp.dot(hf, wg_ref[:, cols], preferred_element_type=F32),
                jnp.dot(hf, wu_ref[:, cols], preferred_element_type=F32))

    def down(gu, c):
        gate, up = gu
        act = (gate * jax.nn.sigmoid(gate) * up).astype(BF16)
        return jnp.dot(act, wd_ref[c * FFN_CHUNK:(c + 1) * FFN_CHUNK, :],
                       preferred_element_type=F32)

    def finish(r, x1, ff):
        o_ref[r, :] = x1 + _rms(ff, gqf_ref[...])

    mixed = [mix(r) for r in groups]
    x1, hf, gu = [], [], []
    for r, mx in zip(groups, mixed):
        x1_r, hf_r = norms(r, mx)
        x1.append(x1_r)
        hf.append(hf_r)
        gu.append(gate_up(hf_r, 0))
    ff = [jnp.zeros((FFN_ROWS, D_MODEL), F32) for _ in groups]
    for c in range(n_chunks):
        for i, r in enumerate(groups):
            nxt = gate_up(hf[i], c + 1) if c + 1 < n_chunks else None
            ff[i] = ff[i] + down(gu[i], c)
            gu[i] = nxt
            if c + 1 == n_chunks:
                finish(r, x1[i], ff[i])


def _out_ffn(layer, x, attn, cp, w_out, g_post_mix, g_pre_ffn, g_post_ffn, w_gate, w_up, w_down):
    T = x.shape[0]
    tm = FFN_TM
    const = lambda t: (layer, 0, 0)
    tile = lambda t: (jnp.maximum(t - 1, 0), 0)
    in_hbm = pl.BlockSpec(memory_space=pl.ANY)
    return pl.pallas_call(
        functools.partial(_out_ffn_kernel, layer),
        out_shape=jax.ShapeDtypeStruct((T, D_MODEL), F32),
        grid=(T // tm + 1,),
        in_specs=[
            pl.BlockSpec((tm, D_MODEL), tile),
            pl.BlockSpec((tm, ATTN_WIDTH), tile),
            pl.BlockSpec((tm, CONV_WIDTH + POOL_WIDTH), tile),
            in_hbm,
            pl.BlockSpec((None, 1, D_MODEL), const),
            pl.BlockSpec((None, 1, D_MODEL), const),
            pl.BlockSpec((None, 1, D_MODEL), const),
            in_hbm,
            in_hbm,
            in_hbm,
        ],
        out_specs=pl.BlockSpec((tm, D_MODEL), tile),
        scratch_shapes=[
            pltpu.VMEM((D_MODEL, D_MODEL), BF16),
            pltpu.VMEM((D_MODEL, D_FF), BF16),
            pltpu.VMEM((D_MODEL, D_FF), BF16),
            pltpu.VMEM((D_FF, D_MODEL), BF16),
            pltpu.VMEM((2, FFN_STAGE_ROWS, D_MODEL), F32),
            pltpu.VMEM((2, FFN_STAGE_ROWS // 2, D_FF), F32),
            pltpu.SemaphoreType.DMA((2,)),
            pltpu.SemaphoreType.DMA((2,)),
        ],
        compiler_params=pltpu.CompilerParams(
            dimension_semantics=("arbitrary",),
            vmem_limit_bytes=VMEM_LIMIT),
        name="out_ffn",
    )(x, attn, cp, w_out, g_post_mix, g_pre_ffn, g_post_ffn, w_gate, w_up, w_down)


def _block_diag(pool_w):
    L, G, C, _ = pool_w.shape
    eye = jnp.eye(G, dtype=pool_w.dtype)
    return (eye[None, :, None, :, None] * pool_w[:, :, :, None, :]).reshape(L, G * C, G * C)


def kernel(x, w_in, w_out, conv_w, pool_w, pool_scale, g_pre_mix, g_post_mix, g_pre_ffn,
           g_post_ffn, w_gate, w_up, w_down):
    B, S, D = x.shape
    depth = w_in.shape[0]
    assert D == D_MODEL and S % MOBA_BLOCK == 0 and S % IN_TM == 0 and (B * S) % FFN_TM == 0
    qx, kx = _attn_aux(S)
    rows = lambda a: a.reshape(depth, 1, -1)
    w_in = w_in.astype(BF16)
    pool_bd = _block_diag(pool_w).astype(BF16)
    g_pre_mix, g_post_mix, g_pre_ffn, g_post_ffn, pool_scale = (
        rows(a) for a in (g_pre_mix, g_post_mix, g_pre_ffn, g_post_ffn, pool_scale))
    for l in range(depth):
        qkv, cp = _in_proj(l, x, g_pre_mix, w_in, conv_w, pool_bd, pool_scale)
        attn = _moba_attn(qkv, qx, kx)
        x = _out_ffn(l, x.reshape(B * S, D), attn.reshape(B * S, ATTN_WIDTH),
                     cp.reshape(B * S, CONV_WIDTH + POOL_WIDTH), w_out,
                     g_post_mix, g_pre_ffn, g_post_ffn, w_gate, w_up, w_down).reshape(B, S, D)
    return x
```

```python
import functools

import jax
import jax.numpy as jnp
import numpy as np
from jax import lax
from jax.experimental import pallas as pl
from jax.experimental.pallas import tpu as pltpu

D_MODEL = 1024
HEAD_DIM = 64
ATTN_WIDTH = 512
ATTN_HEADS = 8
CONV_WIDTH = 256
CONV_K = 3
POOL_WIDTH = 256
POOL_WINDOWS = (2, 4, 8, 16)
POOL_GROUP_DIM = 64
IN_WIDTH = 2560
MOBA_BLOCK = 256
MOBA_TOPK = 3
D_FF = 2816
NORM_EPS = 1e-6

LANES = 128
HEAD_PAIRS = ATTN_WIDTH // LANES
HALO = 16
GATE_ROWS = 16
QKV_WIDTH = 3 * ATTN_WIDTH
CONV_OFF = QKV_WIDTH
POOL_OFF = CONV_OFF + 3 * CONV_WIDTH

IN_TM = 1024
IN_ROWS = 256
FFN_TM = 1024
FFN_ROWS = 256
FFN_CHUNK = 256
VMEM_LIMIT = 56 * 1024 * 1024

BF16 = jnp.bfloat16
F32 = jnp.float32

LOG2E = float(np.log2(np.e))
Q_SCALE = HEAD_DIM ** -0.5 * LOG2E


def _rms(x, g):
    return x * lax.rsqrt(jnp.mean(x * x, axis=-1, keepdims=True) + NORM_EPS) * g


def _in_proj_kernel(x_ref, g_ref, w_ref, cw_ref, pw_ref, ps_ref, qkv_ref, cp_ref,
                    cbuf, pa, pb, pc, pd):
    j = pl.program_id(1)
    tm = IN_ROWS

    @pl.when(j == 0)
    def _():
        cbuf[0:HALO, :] = jnp.zeros((HALO, CONV_WIDTH), F32)
        pa[0:2 * HALO, :] = jnp.zeros((2 * HALO, POOL_WIDTH), F32)
        pb[0:HALO, :] = jnp.zeros((HALO, POOL_WIDTH), F32)
        pc[0:HALO, :] = jnp.zeros((HALO, POOL_WIDTH), F32)
        pd[0:HALO, :] = jnp.zeros((HALO, POOL_WIDTH), F32)

    lane = lax.broadcasted_iota(jnp.int32, (1, POOL_WIDTH), 1)
    g0 = lane < POOL_GROUP_DIM
    g1 = lane < 2 * POOL_GROUP_DIM
    g2 = lane < 3 * POOL_GROUP_DIM
    w2, w4, w8, w16 = (F32(w) for w in POOL_WINDOWS)
    win = jnp.where(g0, w2, jnp.where(g1, w4, jnp.where(g2, w8, w16)))

    def normed(r):
        return _rms(x_ref[0, r, :], g_ref[...]).astype(BF16)

    groups = [slice(r, r + tm) for r in range(0, x_ref.shape[1], tm)]
    h = normed(groups[0])
    for i, r in enumerate(groups):
        cv = jnp.dot(h, w_ref[:, CONV_OFF:POOL_OFF], preferred_element_type=F32)
        up = jnp.dot(h, w_ref[:, POOL_OFF:IN_WIDTH], preferred_element_type=F32)

        q = jnp.dot(h, w_ref[:, 0:ATTN_WIDTH], preferred_element_type=F32) * F32(Q_SCALE)
        qkv_ref[0, r, 0:ATTN_WIDTH] = q.astype(BF16)
        h_next = normed(groups[i + 1]) if i + 1 < len(groups) else None

        h_conv = cv[:, 0:CONV_WIDTH]
        b_gate = cv[:, CONV_WIDTH:2 * CONV_WIDTH]
        c_gate = cv[:, 2 * CONV_WIDTH:3 * CONV_WIDTH]
        u = c_gate * h_conv
        cbuf[HALO:HALO + tm, :] = u
        conv = (cbuf[HALO - 2:HALO - 2 + tm, :] * cw_ref[0:1, :]
                + cbuf[HALO - 1:HALO - 1 + tm, :] * cw_ref[1:2, :]
                + u * cw_ref[2:3, :])
        cp_ref[0, r, 0:CONV_WIDTH] = (b_gate * conv).astype(BF16)
        cbuf[0:HALO, :] = cbuf[tm:tm + HALO, :]

        qkv_ref[0, r, ATTN_WIDTH:QKV_WIDTH] = jnp.dot(
            h, w_ref[:, ATTN_WIDTH:QKV_WIDTH], preferred_element_type=F32).astype(BF16)

        pa[2 * HALO:2 * HALO + tm, :] = up
        n = tm + HALO
        pb[HALO:HALO + n, :] = pa[HALO:HALO + n, :] + pa[HALO - 1:HALO - 1 + n, :]
        pc[HALO:HALO + n, :] = pb[HALO:HALO + n, :] + pb[HALO - 2:HALO - 2 + n, :]
        pd[HALO:HALO + n, :] = pc[HALO:HALO + n, :] + pc[HALO - 4:HALO - 4 + n, :]
        s2 = pb[2 * HALO:2 * HALO + tm, :]
        s4 = pc[2 * HALO:2 * HALO + tm, :]
        s8 = pd[2 * HALO:2 * HALO + tm, :]
        s16 = s8 + pd[2 * HALO - 8:2 * HALO - 8 + tm, :]
        wsum = jnp.where(g0, s2, jnp.where(g1, s4, jnp.where(g2, s8, s16)))
        t1 = (j * x_ref.shape[1] + r.start + 1
              + lax.broadcasted_iota(jnp.int32, (tm, 1), 0)).astype(F32)
        cnt = jnp.minimum(t1, win)
        pooled = (wsum / cnt - up).astype(BF16)
        y = jnp.dot(pooled, pw_ref[...], preferred_element_type=F32) * ps_ref[...]
        cp_ref[0, r, CONV_WIDTH:CONV_WIDTH + POOL_WIDTH] = y.astype(BF16)
        pa[HALO:2 * HALO, :] = pa[tm + HALO:tm + 2 * HALO, :]
        h = h_next


def _in_proj(layer, x, g, w_in, conv_w, pool_bd, pool_scale):
    B, S, _ = x.shape
    tm = IN_TM
    const = lambda b, j: (layer, 0, 0)
    return pl.pallas_call(
        _in_proj_kernel,
        out_shape=(jax.ShapeDtypeStruct((B, S, QKV_WIDTH), BF16),
                   jax.ShapeDtypeStruct((B, S, CONV_WIDTH + POOL_WIDTH), BF16)),
        grid=(B, S // tm),
        in_specs=[
            pl.BlockSpec((1, tm, D_MODEL), lambda b, j: (b, j, 0)),
            pl.BlockSpec((None, 1, D_MODEL), const),
            pl.BlockSpec((None, D_MODEL, IN_WIDTH), const, pipeline_mode=pl.Buffered(1)),
            pl.BlockSpec((None, CONV_K, CONV_WIDTH), const),
            pl.BlockSpec((None, POOL_WIDTH, POOL_WIDTH), const),
            pl.BlockSpec((None, 1, POOL_WIDTH), const),
        ],
        out_specs=(pl.BlockSpec((1, tm, QKV_WIDTH), lambda b, j: (b, j, 0)),
                   pl.BlockSpec((1, tm, CONV_WIDTH + POOL_WIDTH), lambda b, j: (b, j, 0))),
        scratch_shapes=[
            pltpu.VMEM((HALO + IN_ROWS, CONV_WIDTH), F32),
            pltpu.VMEM((2 * HALO + IN_ROWS, POOL_WIDTH), F32),
            pltpu.VMEM((2 * HALO + IN_ROWS, POOL_WIDTH), F32),
            pltpu.VMEM((2 * HALO + IN_ROWS, POOL_WIDTH), F32),
            pltpu.VMEM((2 * HALO + IN_ROWS, POOL_WIDTH), F32),
        ],
        compiler_params=pltpu.CompilerParams(
            dimension_semantics=("arbitrary", "arbitrary"),
            vmem_limit_bytes=VMEM_LIMIT),
        name="in_proj",
    )(x, g, w_in, conv_w, pool_bd, pool_scale)


NEG_BIG = -1e30
ALIBI_PARTS = 3
AUX_ALIBI_HI = 0
AUX_ALIBI_LO = ALIBI_PARTS
AUX_BLOCK0 = 8
VT_ROWS = HEAD_DIM + 16
EXP_LAG = 4
PV_LAG = 8
POS_SPLIT = 16


def _block_mask_t(g, q_blk):
    nb = g.shape[0]
    blk_idx = lax.broadcasted_iota(jnp.int32, g.shape, 0)
    elig = blk_idx < q_blk
    ranks = []
    for jb in range(nb):
        gj = g[jb:jb + 1, :]
        beats = ((g > gj) | ((g == gj) & (blk_idx < jb))) & elig
        ranks.append(jnp.sum(beats.astype(F32), axis=0, keepdims=True))
    rank = jnp.concatenate(ranks, axis=0)
    attend = ((rank < MOBA_TOPK) & elig) | (blk_idx == q_blk)
    return jnp.where(attend, F32(0.0), F32(NEG_BIG))


def _attn_kernel(q_ref, k_ref, v_ref, qx_ref, kx_ref, o_ref, qa_ref, ka_ref, vt_ref):
    S = k_ref.shape[1]
    blk = MOBA_BLOCK
    nb = S // blk
    k = k_ref[0]
    v = v_ref[0]
    lane = lax.broadcasted_iota(jnp.int32, (1, LANES), 1)
    nt = (((1,), (1,)), ((), ()))

    km = jnp.mean(k.astype(F32).reshape(nb, blk, LANES), axis=1)
    km = jnp.concatenate([km, jnp.zeros((GATE_ROWS - nb, LANES), F32)], axis=0)
    km_hi = km.astype(BF16)
    km_lo = (km - km_hi.astype(F32)).astype(BF16)

    gated = min((MOBA_TOPK + 1) * blk, S)
    q_free = q_ref[0, 0:gated, :]
    q_gate = q_ref[0, gated:S, :]
    q_blk = (gated + lax.broadcasted_iota(jnp.int32, (nb, S - gated), 1)) // blk
    in_heads = [(lane >= h * HEAD_DIM) & (lane < (h + 1) * HEAD_DIM) for h in range(2)]
    v_t = v.T
    ones_tile = jnp.where(lax.broadcasted_iota(jnp.int32, (VT_ROWS - HEAD_DIM, S), 0) == 0,
                          F32(1.0), F32(0.0)).astype(BF16)
    for h in range(2):
        alibi = qx_ref[0, h:h + 1, :]
        qa_ref[h, 0:gated, :] = jnp.where(
            in_heads[h], q_free, jnp.broadcast_to(alibi.astype(BF16), q_free.shape))
        ka_ref[h] = jnp.where(in_heads[h], k, kx_ref[h])
        vt_ref[h] = jnp.concatenate([v_t[h * HEAD_DIM:(h + 1) * HEAD_DIM], ones_tile], axis=0)

    def gate_queries(h):
        qh = jnp.where(in_heads[h], q_gate, jnp.zeros_like(q_gate))
        g = (lax.dot_general(km_hi, qh, nt, preferred_element_type=F32)
             + lax.dot_general(km_lo, qh, nt, preferred_element_type=F32))[0:nb]
        aux = (1 - h) * HEAD_DIM + AUX_BLOCK0
        mask_t = jnp.concatenate([jnp.zeros((aux, S - gated), F32), _block_mask_t(g, q_blk),
                                  jnp.zeros((LANES - aux - nb, S - gated), F32)], axis=0)
        extra = mask_t.T + qx_ref[0, h:h + 1, :]
        qa_ref[h, gated:S, :] = jnp.where(in_heads[h], q_gate, extra.astype(BF16))

    key = lax.broadcasted_iota(jnp.int32, (blk, blk), 0)
    qry = lax.broadcasted_iota(jnp.int32, (blk, blk), 1)
    causal = key <= qry

    def scores(c, h, j):
        s = lax.dot_general(ka_ref[h, j * blk:(j + 1) * blk, :], qa_ref[h, c * blk:(c + 1) * blk, :],
                            nt, preferred_element_type=F32)
        if j == c:
            s = jnp.where(causal, s, F32(NEG_BIG))
        return s, jnp.max(s, axis=0, keepdims=True)

    def probs(s, cm, m_old):
        m_new = cm if m_old is None else jnp.maximum(m_old, cm)
        alpha = None if m_old is None else jnp.exp2(m_old - m_new)
        return jnp.exp2(s - m_new).astype(BF16), m_new, alpha

    def accumulate(h, j, p, alpha, acc):
        pv = jnp.dot(vt_ref[h, :, j * blk:(j + 1) * blk], p, preferred_element_type=F32)
        return pv if acc is None else acc * alpha + pv

    for h in range(2):
        if gated < S:
            gate_queries(h)
    steps = [(c, h, c) for c in range(nb) for h in range(2)]
    steps += [(c, h, j) for j in range(nb) for c in range(j + 1, nb) for h in range(2)]
    last_block = {c: (c - 1 if c else 0) for c in range(nb)}
    m_run, acc_run, outs = {}, {}, {}
    staged_s, staged_p = {}, {}
    for t in range(len(steps) + PV_LAG):
        if t < len(steps):
            staged_s[t] = scores(*steps[t])
        if 0 <= t - EXP_LAG < len(steps):
            c, h, j = steps[t - EXP_LAG]
            sc, cm = staged_s.pop(t - EXP_LAG)
            p, m_run[c, h], alpha = probs(sc, cm, m_run.get((c, h)))
            staged_p[t - EXP_LAG] = (p, alpha)
        if 0 <= t - PV_LAG < len(steps):
            c, h, j = steps[t - PV_LAG]
            p, alpha = staged_p.pop(t - PV_LAG)
            acc_run[c, h] = accumulate(h, j, p, alpha, acc_run.get((c, h)))
            if j == last_block[c]:
                acc = acc_run.pop((c, h))
                outs[c, h] = acc[0:HEAD_DIM] * (1.0 / acc[HEAD_DIM:HEAD_DIM + 1])
                if (c, 0) in outs and (c, 1) in outs:
                    o_ref[0, c * blk:(c + 1) * blk, :] = jnp.concatenate(
                        [outs.pop((c, 0)), outs.pop((c, 1))], axis=0).T.astype(BF16)


def _moba_attn(qkv, qx, kx):
    B, S, _ = qkv.shape
    return pl.pallas_call(
        _attn_kernel,
        out_shape=jax.ShapeDtypeStruct((B, S, ATTN_WIDTH), BF16),
        grid=(B, HEAD_PAIRS),
        in_specs=[
            pl.BlockSpec((1, S, LANES), lambda b, hp: (b, 0, hp)),
            pl.BlockSpec((1, S, LANES), lambda b, hp: (b, 0, HEAD_PAIRS + hp)),
            pl.BlockSpec((1, S, LANES), lambda b, hp: (b, 0, 2 * HEAD_PAIRS + hp)),
            pl.BlockSpec((1, 2, LANES), lambda b, hp: (hp, 0, 0)),
            pl.BlockSpec((2, S, LANES), lambda b, hp: (0, 0, 0)),
        ],
        out_specs=pl.BlockSpec((1, S, LANES), lambda b, hp: (b, 0, hp)),
        scratch_shapes=[
            pltpu.VMEM((2, S, LANES), BF16),
            pltpu.VMEM((2, S, LANES), BF16),
            pltpu.VMEM((2, VT_ROWS, S), BF16),
        ],
        compiler_params=pltpu.CompilerParams(
            dimension_semantics=("arbitrary", "arbitrary"),
            vmem_limit_bytes=VMEM_LIMIT),
        name="moba_attn",
    )(qkv, qkv, qkv, qx, kx)


def _attn_aux(S):
    nb = S // MOBA_BLOCK
    assert S // POS_SPLIT <= 128 and 2 * ALIBI_PARTS <= AUX_BLOCK0 and AUX_BLOCK0 + nb <= HEAD_DIM
    slopes = 2.0 ** (-8.0 * np.arange(1, ATTN_HEADS + 1) / ATTN_HEADS)
    rest = (slopes * LOG2E).astype(np.float32)
    pos = np.arange(S)
    qx = np.zeros((ATTN_HEADS, LANES), np.float32)
    kx = np.zeros((2, S, LANES), np.float32)
    for i in range(ALIBI_PARTS):
        part = rest.astype(BF16).astype(np.float32)
        rest = rest - part
        for h in range(2):
            aux = (1 - h) * HEAD_DIM
            qx[h::2, aux + AUX_ALIBI_HI + i] = part[h::2] * POS_SPLIT
            qx[h::2, aux + AUX_ALIBI_LO + i] = part[h::2]
            kx[h, :, aux + AUX_ALIBI_HI + i] = pos // POS_SPLIT
            kx[h, :, aux + AUX_ALIBI_LO + i] = pos % POS_SPLIT
    for h in range(2):
        aux = (1 - h) * HEAD_DIM
        kx[h, pos, aux + AUX_BLOCK0 + pos // MOBA_BLOCK] = 1.0
    return jnp.asarray(qx.reshape(HEAD_PAIRS, 2, LANES)), jnp.asarray(kx, dtype=BF16)


def _out_ffn_kernel(x_ref, a_ref, cp_ref, wo_ref, gpm_ref, gpf_ref, gqf_ref,
                    wg_ref, wu_ref, wd_ref, o_ref):
    groups = [slice(r, r + FFN_ROWS) for r in range(0, x_ref.shape[0], FFN_ROWS)]
    n_chunks = D_FF // FFN_CHUNK

    def mix(r):
        return (jnp.dot(a_ref[r, :], wo_ref[0:ATTN_WIDTH, :], preferred_element_type=F32)
                + jnp.dot(cp_ref[r, :], wo_ref[ATTN_WIDTH:D_MODEL, :], preferred_element_type=F32))

    def norms(r, mixed):
        x1 = x_ref[r, :] + _rms(mixed, gpm_ref[...])
        return x1, _rms(x1, gpf_ref[...]).astype(BF16)

    def gate_up(hf, c):
        cols = slice(c * FFN_CHUNK, (c + 1) * FFN_CHUNK)
        return (jnp.dot(hf, wg_ref[:, cols], preferred_element_type=F32),
                jnp.dot(hf, wu_ref[:, cols], preferred_element_type=F32))

    def down(gu, c):
        gate, up = gu
        act = (gate * jax.nn.sigmoid(gate) * up).astype(BF16)
        return jnp.dot(act, wd_ref[c * FFN_CHUNK:(c + 1) * FFN_CHUNK, :],
                       preferred_element_type=F32)

    def finish(r, x1, ff):
        o_ref[r, :] = x1 + _rms(ff, gqf_ref[...])

    mixed = [mix(r) for r in groups]
    x1, hf, gu = [], [], []
    for r, mx in zip(groups, mixed):
        x1_r, hf_r = norms(r, mx)
        x1.append(x1_r)
        hf.append(hf_r)
        gu.append(gate_up(hf_r, 0))
    ff = [jnp.zeros((FFN_ROWS, D_MODEL), F32) for _ in groups]
    for c in range(n_chunks):
        for i, r in enumerate(groups):
            nxt = gate_up(hf[i], c + 1) if c + 1 < n_chunks else None
            ff[i] = ff[i] + down(gu[i], c)
            gu[i] = nxt
            if c + 1 == n_chunks:
                finish(r, x1[i], ff[i])


def _out_ffn(layer, x, attn, cp, w_out, g_post_mix, g_pre_ffn, g_post_ffn, w_gate, w_up, w_down):
    T = x.shape[0]
    tm = FFN_TM
    const = lambda t: (layer, 0, 0)
    resident = functools.partial(pl.BlockSpec, index_map=const, pipeline_mode=pl.Buffered(1))
    return pl.pallas_call(
        _out_ffn_kernel,
        out_shape=jax.ShapeDtypeStruct((T, D_MODEL), F32),
        grid=(T // tm,),
        in_specs=[
            pl.BlockSpec((tm, D_MODEL), lambda t: (t, 0)),
            pl.BlockSpec((tm, ATTN_WIDTH), lambda t: (t, 0)),
            pl.BlockSpec((tm, CONV_WIDTH + POOL_WIDTH), lambda t: (t, 0)),
            resident((None, D_MODEL, D_MODEL)),
            pl.BlockSpec((None, 1, D_MODEL), const),
            pl.BlockSpec((None, 1, D_MODEL), const),
            pl.BlockSpec((None, 1, D_MODEL), const),
            resident((None, D_MODEL, D_FF)),
            resident((None, D_MODEL, D_FF)),
            resident((None, D_FF, D_MODEL)),
        ],
        out_specs=pl.BlockSpec((tm, D_MODEL), lambda t: (t, 0)),
        compiler_params=pltpu.CompilerParams(
            dimension_semantics=("arbitrary",),
            vmem_limit_bytes=VMEM_LIMIT),
        name="out_ffn",
    )(x, attn, cp, w_out, g_post_mix, g_pre_ffn, g_post_ffn, w_gate, w_up, w_down)


def _block_diag(pool_w):
    L, G, C, _ = pool_w.shape
    eye = jnp.eye(G, dtype=pool_w.dtype)
    return (eye[None, :, None, :, None] * pool_w[:, :, :, None, :]).reshape(L, G * C, G * C)


def kernel(x, w_in, w_out, conv_w, pool_w, pool_scale, g_pre_mix, g_post_mix, g_pre_ffn,
           g_post_ffn, w_gate, w_up, w_down):
    B, S, D = x.shape
    depth = w_in.shape[0]
    assert D == D_MODEL and S % MOBA_BLOCK == 0 and S % IN_TM == 0 and (B * S) % FFN_TM == 0
    qx, kx = _attn_aux(S)
    rows = lambda a: a.reshape(depth, 1, -1)
    w_in, w_out, w_gate, w_up, w_down = (w.astype(BF16) for w in (w_in, w_out, w_gate, w_up, w_down))
    pool_bd = _block_diag(pool_w).astype(BF16)
    g_pre_mix, g_post_mix, g_pre_ffn, g_post_ffn, pool_scale = (
        rows(a) for a in (g_pre_mix, g_post_mix, g_pre_ffn, g_post_ffn, pool_scale))
    for l in range(depth):
        qkv, cp = _in_proj(l, x, g_pre_mix, w_in, conv_w, pool_bd, pool_scale)
        attn = _moba_attn(qkv, qx, kx)
        x = _out_ffn(l, x.reshape(B * S, D), attn.reshape(B * S, ATTN_WIDTH),
                     cp.reshape(B * S, CONV_WIDTH + POOL_WIDTH), w_out,
                     g_post_mix, g_pre_ffn, g_post_ffn, w_gate, w_up, w_down).reshape(B, S, D)
    return x
```

```python
import functools

import jax
import jax.numpy as jnp
import numpy as np
from jax import lax
from jax.experimental import pallas as pl
from jax.experimental.pallas import tpu as pltpu

D_MODEL = 1024
HEAD_DIM = 64
ATTN_WIDTH = 512
ATTN_HEADS = 8
CONV_WIDTH = 256
CONV_K = 3
POOL_WIDTH = 256
POOL_WINDOWS = (2, 4, 8, 16)
POOL_GROUP_DIM = 64
IN_WIDTH = 2560
MOBA_BLOCK = 256
MOBA_TOPK = 3
D_FF = 2816
NORM_EPS = 1e-6

LANES = 128
HEAD_PAIRS = ATTN_WIDTH // LANES
HALO = 16
GATE_ROWS = 16
QKV_WIDTH = 3 * ATTN_WIDTH
CONV_OFF = QKV_WIDTH
POOL_OFF = CONV_OFF + 3 * CONV_WIDTH

IN_TM = 1024
IN_ROWS = 256
FFN_TM = 1024
FFN_ROWS = 256
FFN_CHUNK = 256
VMEM_LIMIT = 56 * 1024 * 1024

BF16 = jnp.bfloat16
F32 = jnp.float32

LOG2E = float(np.log2(np.e))
Q_SCALE = HEAD_DIM ** -0.5 * LOG2E


def _rms(x, g):
    return x * lax.rsqrt(jnp.mean(x * x, axis=-1, keepdims=True) + NORM_EPS) * g


def _in_proj_kernel(layer, x_ref, g_ref, w32_ref, cw_ref, pw_ref, ps_ref, qkv_ref, cp_ref,
                    w_ref, cbuf, pa, pb, pc, pd):
    j = pl.program_id(1)
    tm = IN_ROWS
    gain = g_ref[layer:layer + 1, :]
    pool_scale = ps_ref[layer:layer + 1, :]

    @pl.when((pl.program_id(0) == 0) & (j == 0))
    def _():
        for c in range(0, IN_WIDTH, 2 * LANES):
            w_ref[:, c:c + 2 * LANES] = w32_ref[:, c:c + 2 * LANES].astype(BF16)

    @pl.when(j == 0)
    def _():
        cbuf[0:HALO, :] = jnp.zeros((HALO, CONV_WIDTH), F32)
        pa[0:2 * HALO, :] = jnp.zeros((2 * HALO, POOL_WIDTH), F32)
        pb[0:HALO, :] = jnp.zeros((HALO, POOL_WIDTH), F32)
        pc[0:HALO, :] = jnp.zeros((HALO, POOL_WIDTH), F32)
        pd[0:HALO, :] = jnp.zeros((HALO, POOL_WIDTH), F32)

    lane = lax.broadcasted_iota(jnp.int32, (1, POOL_WIDTH), 1)
    g0 = lane < POOL_GROUP_DIM
    g1 = lane < 2 * POOL_GROUP_DIM
    g2 = lane < 3 * POOL_GROUP_DIM
    w2, w4, w8, w16 = (F32(w) for w in POOL_WINDOWS)
    win = jnp.where(g0, w2, jnp.where(g1, w4, jnp.where(g2, w8, w16)))

    def normed(r):
        return _rms(x_ref[0, r, :], gain).astype(BF16)

    groups = [slice(r, r + tm) for r in range(0, x_ref.shape[1], tm)]
    h = normed(groups[0])
    for i, r in enumerate(groups):
        cv = jnp.dot(h, w_ref[:, CONV_OFF:POOL_OFF], preferred_element_type=F32)
        up = jnp.dot(h, w_ref[:, POOL_OFF:IN_WIDTH], preferred_element_type=F32)

        q = jnp.dot(h, w_ref[:, 0:ATTN_WIDTH], preferred_element_type=F32) * F32(Q_SCALE)
        qkv_ref[0, r, 0:ATTN_WIDTH] = q.astype(BF16)
        h_next = normed(groups[i + 1]) if i + 1 < len(groups) else None

        h_conv = cv[:, 0:CONV_WIDTH]
        b_gate = cv[:, CONV_WIDTH:2 * CONV_WIDTH]
        c_gate = cv[:, 2 * CONV_WIDTH:3 * CONV_WIDTH]
        u = c_gate * h_conv
        cbuf[HALO:HALO + tm, :] = u
        conv = (cbuf[HALO - 2:HALO - 2 + tm, :] * cw_ref[0:1, :]
                + cbuf[HALO - 1:HALO - 1 + tm, :] * cw_ref[1:2, :]
                + u * cw_ref[2:3, :])
        cp_ref[0, r, 0:CONV_WIDTH] = (b_gate * conv).astype(BF16)
        cbuf[0:HALO, :] = cbuf[tm:tm + HALO, :]

        qkv_ref[0, r, ATTN_WIDTH:QKV_WIDTH] = jnp.dot(
            h, w_ref[:, ATTN_WIDTH:QKV_WIDTH], preferred_element_type=F32).astype(BF16)

        pa[2 * HALO:2 * HALO + tm, :] = up
        n = tm + HALO
        pb[HALO:HALO + n, :] = pa[HALO:HALO + n, :] + pa[HALO - 1:HALO - 1 + n, :]
        pc[HALO:HALO + n, :] = pb[HALO:HALO + n, :] + pb[HALO - 2:HALO - 2 + n, :]
        pd[HALO:HALO + n, :] = pc[HALO:HALO + n, :] + pc[HALO - 4:HALO - 4 + n, :]
        s2 = pb[2 * HALO:2 * HALO + tm, :]
        s4 = pc[2 * HALO:2 * HALO + tm, :]
        s8 = pd[2 * HALO:2 * HALO + tm, :]
        s16 = s8 + pd[2 * HALO - 8:2 * HALO - 8 + tm, :]
        wsum = jnp.where(g0, s2, jnp.where(g1, s4, jnp.where(g2, s8, s16)))
        t1 = (j * x_ref.shape[1] + r.start + 1
              + lax.broadcasted_iota(jnp.int32, (tm, 1), 0)).astype(F32)
        cnt = jnp.minimum(t1, win)
        pooled = (wsum / cnt - up).astype(BF16)
        y = jnp.dot(pooled, pw_ref[...], preferred_element_type=F32) * pool_scale
        cp_ref[0, r, CONV_WIDTH:CONV_WIDTH + POOL_WIDTH] = y.astype(BF16)
        pa[HALO:2 * HALO, :] = pa[tm + HALO:tm + 2 * HALO, :]
        h = h_next


def _in_proj(layer, x, g, w_in, conv_w, pool_bd, pool_scale):
    B, S, _ = x.shape
    tm = IN_TM
    const = lambda b, j: (layer, 0, 0)
    whole = lambda b, j: (0, 0)
    return pl.pallas_call(
        functools.partial(_in_proj_kernel, layer),
        out_shape=(jax.ShapeDtypeStruct((B, S, QKV_WIDTH), BF16),
                   jax.ShapeDtypeStruct((B, S, CONV_WIDTH + POOL_WIDTH), BF16)),
        grid=(B, S // tm),
        in_specs=[
            pl.BlockSpec((1, tm, D_MODEL), lambda b, j: (b, j, 0)),
            pl.BlockSpec(g.shape, whole),
            pl.BlockSpec((None, D_MODEL, IN_WIDTH), const, pipeline_mode=pl.Buffered(1)),
            pl.BlockSpec((None, CONV_K, CONV_WIDTH), const),
            pl.BlockSpec((None, POOL_WIDTH, POOL_WIDTH), const),
            pl.BlockSpec(pool_scale.shape, whole),
        ],
        out_specs=(pl.BlockSpec((1, tm, QKV_WIDTH), lambda b, j: (b, j, 0)),
                   pl.BlockSpec((1, tm, CONV_WIDTH + POOL_WIDTH), lambda b, j: (b, j, 0))),
        scratch_shapes=[
            pltpu.VMEM((D_MODEL, IN_WIDTH), BF16),
            pltpu.VMEM((HALO + IN_ROWS, CONV_WIDTH), F32),
            pltpu.VMEM((2 * HALO + IN_ROWS, POOL_WIDTH), F32),
            pltpu.VMEM((2 * HALO + IN_ROWS, POOL_WIDTH), F32),
            pltpu.VMEM((2 * HALO + IN_ROWS, POOL_WIDTH), F32),
            pltpu.VMEM((2 * HALO + IN_ROWS, POOL_WIDTH), F32),
        ],
        compiler_params=pltpu.CompilerParams(
            dimension_semantics=("arbitrary", "arbitrary"),
            vmem_limit_bytes=VMEM_LIMIT),
        name="in_proj",
    )(x, g, w_in, conv_w, pool_bd, pool_scale)


NEG_BIG = -1e30
ALIBI_PARTS = 3
AUX_ALIBI_HI = 0
AUX_ALIBI_LO = ALIBI_PARTS
AUX_BLOCK0 = 8
VT_ROWS = HEAD_DIM + 16
EXP_LAG = 4
PV_LAG = 8
POS_SPLIT = 16


def _block_mask_t(g, q_blk):
    nb = g.shape[0]
    blk_idx = lax.broadcasted_iota(jnp.int32, g.shape, 0)
    elig = blk_idx < q_blk
    ranks = []
    for jb in range(nb):
        gj = g[jb:jb + 1, :]
        beats = ((g > gj) | ((g == gj) & (blk_idx < jb))) & elig
        ranks.append(jnp.sum(beats.astype(F32), axis=0, keepdims=True))
    rank = jnp.concatenate(ranks, axis=0)
    attend = ((rank < MOBA_TOPK) & elig) | (blk_idx == q_blk)
    return jnp.where(attend, F32(0.0), F32(NEG_BIG))


def _attn_kernel(q_ref, k_ref, v_ref, qx_ref, kx_ref, o_ref, qa_ref, ka_ref, vt_ref):
    S = k_ref.shape[1]
    blk = MOBA_BLOCK
    nb = S // blk
    k = k_ref[0]
    v = v_ref[0]
    lane = lax.broadcasted_iota(jnp.int32, (1, LANES), 1)
    nt = (((1,), (1,)), ((), ()))

    km = jnp.mean(k.astype(F32).reshape(nb, blk, LANES), axis=1)
    km = jnp.concatenate([km, jnp.zeros((GATE_ROWS - nb, LANES), F32)], axis=0)
    km_hi = km.astype(BF16)
    km_lo = (km - km_hi.astype(F32)).astype(BF16)

    gated = min((MOBA_TOPK + 1) * blk, S)
    q_free = q_ref[0, 0:gated, :]
    q_gate = q_ref[0, gated:S, :]
    q_blk = (gated + lax.broadcasted_iota(jnp.int32, (nb, S - gated), 1)) // blk
    in_heads = [(lane >= h * HEAD_DIM) & (lane < (h + 1) * HEAD_DIM) for h in range(2)]
    v_t = v.T
    ones_tile = jnp.where(lax.broadcasted_iota(jnp.int32, (VT_ROWS - HEAD_DIM, S), 0) == 0,
                          F32(1.0), F32(0.0)).astype(BF16)
    for h in range(2):
        alibi = qx_ref[0, h:h + 1, :]
        qa_ref[h, 0:gated, :] = jnp.where(
            in_heads[h], q_free, jnp.broadcast_to(alibi.astype(BF16), q_free.shape))
        ka_ref[h] = jnp.where(in_heads[h], k, kx_ref[h])
        vt_ref[h] = jnp.concatenate([v_t[h * HEAD_DIM:(h + 1) * HEAD_DIM], ones_tile], axis=0)

    def gate_queries(h):
        qh = jnp.where(in_heads[h], q_gate, jnp.zeros_like(q_gate))
        g = (lax.dot_general(km_hi, qh, nt, preferred_element_type=F32)
             + lax.dot_general(km_lo, qh, nt, preferred_element_type=F32))[0:nb]
        aux = (1 - h) * HEAD_DIM + AUX_BLOCK0
        mask_t = jnp.concatenate([jnp.zeros((aux, S - gated), F32), _block_mask_t(g, q_blk),
                                  jnp.zeros((LANES - aux - nb, S - gated), F32)], axis=0)
        extra = mask_t.T + qx_ref[0, h:h + 1, :]
        qa_ref[h, gated:S, :] = jnp.where(in_heads[h], q_gate, extra.astype(BF16))

    key = lax.broadcasted_iota(jnp.int32, (blk, blk), 0)
    qry = lax.broadcasted_iota(jnp.int32, (blk, blk), 1)
    causal = key <= qry

    def scores(c, h, j):
        s = lax.dot_general(ka_ref[h, j * blk:(j + 1) * blk, :], qa_ref[h, c * blk:(c + 1) * blk, :],
                            nt, preferred_element_type=F32)
        if j == c:
            s = jnp.where(causal, s, F32(NEG_BIG))
        return s, jnp.max(s, axis=0, keepdims=True)

    def probs(s, cm, m_old):
        m_new = cm if m_old is None else jnp.maximum(m_old, cm)
        alpha = None if m_old is None else jnp.exp2(m_old - m_new)
        return jnp.exp2(s - m_new).astype(BF16), m_new, alpha

    def accumulate(h, j, p, alpha, acc):
        pv = jnp.dot(vt_ref[h, :, j * blk:(j + 1) * blk], p, preferred_element_type=F32)
        return pv if acc is None else acc * alpha + pv

    for h in range(2):
        if gated < S:
            gate_queries(h)
    steps = [(c, h, c) for c in range(nb) for h in range(2)]
    steps += [(c, h, j) for j in range(nb) for c in range(j + 1, nb) for h in range(2)]
    last_block = {c: (c - 1 if c else 0) for c in range(nb)}
    m_run, acc_run, outs = {}, {}, {}
    staged_s, staged_p = {}, {}
    for t in range(len(steps) + PV_LAG):
        if t < len(steps):
            staged_s[t] = scores(*steps[t])
        if 0 <= t - EXP_LAG < len(steps):
            c, h, j = steps[t - EXP_LAG]
            sc, cm = staged_s.pop(t - EXP_LAG)
            p, m_run[c, h], alpha = probs(sc, cm, m_run.get((c, h)))
            staged_p[t - EXP_LAG] = (p, alpha)
        if 0 <= t - PV_LAG < len(steps):
            c, h, j = steps[t - PV_LAG]
            p, alpha = staged_p.pop(t - PV_LAG)
            acc_run[c, h] = accumulate(h, j, p, alpha, acc_run.get((c, h)))
            if j == last_block[c]:
                acc = acc_run.pop((c, h))
                outs[c, h] = acc[0:HEAD_DIM] * (1.0 / acc[HEAD_DIM:HEAD_DIM + 1])
                if (c, 0) in outs and (c, 1) in outs:
                    o_ref[0, c * blk:(c + 1) * blk, :] = jnp.concatenate(
                        [outs.pop((c, 0)), outs.pop((c, 1))], axis=0).T.astype(BF16)


def _moba_attn(qkv, qx, kx):
    B, S, _ = qkv.shape
    return pl.pallas_call(
        _attn_kernel,
        out_shape=jax.ShapeDtypeStruct((B, S, ATTN_WIDTH), BF16),
        grid=(B, HEAD_PAIRS),
        in_specs=[
            pl.BlockSpec((1, S, LANES), lambda b, hp: (b, 0, hp)),
            pl.BlockSpec((1, S, LANES), lambda b, hp: (b, 0, HEAD_PAIRS + hp)),
            pl.BlockSpec((1, S, LANES), lambda b, hp: (b, 0, 2 * HEAD_PAIRS + hp)),
            pl.BlockSpec((1, 2, LANES), lambda b, hp: (hp, 0, 0)),
            pl.BlockSpec((2, S, LANES), lambda b, hp: (0, 0, 0)),
        ],
        out_specs=pl.BlockSpec((1, S, LANES), lambda b, hp: (b, 0, hp)),
        scratch_shapes=[
            pltpu.VMEM((2, S, LANES), BF16),
            pltpu.VMEM((2, S, LANES), BF16),
            pltpu.VMEM((2, VT_ROWS, S), BF16),
        ],
        compiler_params=pltpu.CompilerParams(
            dimension_semantics=("arbitrary", "arbitrary"),
            vmem_limit_bytes=VMEM_LIMIT),
        name="moba_attn",
    )(qkv, qkv, qkv, qx, kx)


def _attn_aux(S):
    nb = S // MOBA_BLOCK
    assert S // POS_SPLIT <= 128 and 2 * ALIBI_PARTS <= AUX_BLOCK0 and AUX_BLOCK0 + nb <= HEAD_DIM
    slopes = 2.0 ** (-8.0 * np.arange(1, ATTN_HEADS + 1) / ATTN_HEADS)
    rest = (slopes * LOG2E).astype(np.float32)
    pos = np.arange(S)
    qx = np.zeros((ATTN_HEADS, LANES), np.float32)
    kx = np.zeros((2, S, LANES), np.float32)
    for i in range(ALIBI_PARTS):
        part = rest.astype(BF16).astype(np.float32)
        rest = rest - part
        for h in range(2):
            aux = (1 - h) * HEAD_DIM
            qx[h::2, aux + AUX_ALIBI_HI + i] = part[h::2] * POS_SPLIT
            qx[h::2, aux + AUX_ALIBI_LO + i] = part[h::2]
            kx[h, :, aux + AUX_ALIBI_HI + i] = pos // POS_SPLIT
            kx[h, :, aux + AUX_ALIBI_LO + i] = pos % POS_SPLIT
    for h in range(2):
        aux = (1 - h) * HEAD_DIM
        kx[h, pos, aux + AUX_BLOCK0 + pos // MOBA_BLOCK] = 1.0
    return jnp.asarray(qx.reshape(HEAD_PAIRS, 2, LANES)), jnp.asarray(kx, dtype=BF16)


def _out_ffn_kernel(layer, x_ref, a_ref, cp_ref, wo_ref, gpm_ref, gpf_ref, gqf_ref,
                    wg_ref, wu_ref, wd_ref, o_ref):
    g_post_mix, g_pre_ffn, g_post_ffn = (g[layer:layer + 1, :] for g in (gpm_ref, gpf_ref, gqf_ref))
    groups = [slice(r, r + FFN_ROWS) for r in range(0, x_ref.shape[0], FFN_ROWS)]
    n_chunks = D_FF // FFN_CHUNK

    def mix(r):
        return (jnp.dot(a_ref[r, :], wo_ref[0:ATTN_WIDTH, :], preferred_element_type=F32)
                + jnp.dot(cp_ref[r, :], wo_ref[ATTN_WIDTH:D_MODEL, :], preferred_element_type=F32))

    def norms(r, mixed):
        x1 = x_ref[r, :] + _rms(mixed, g_post_mix)
        return x1, _rms(x1, g_pre_ffn).astype(BF16)

    def gate_up(hf, c):
        cols = slice(c * FFN_CHUNK, (c + 1) * FFN_CHUNK)
        return (jnp.dot(hf, wg_ref[:, cols], preferred_element_type=F32),
                jnp.dot(hf, wu_ref[:, cols], preferred_element_type=F32))

    def down(gu, c):
        gate, up = gu
        act = (gate * jax.nn.sigmoid(gate) * up).astype(BF16)
        return jnp.dot(act, wd_ref[c * FFN_CHUNK:(c + 1) * FFN_CHUNK, :],
                       preferred_element_type=F32)

    def finish(r, x1, ff):
        o_ref[r, :] = x1 + _rms(ff, g_post_ffn)

    mixed = [mix(r) for r in groups]
    x1, hf, gu = [], [], []
    for r, mx in zip(groups, mixed):
        x1_r, hf_r = norms(r, mx)
        x1.append(x1_r)
        hf.append(hf_r)
        gu.append(gate_up(hf_r, 0))
    ff = [jnp.zeros((FFN_ROWS, D_MODEL), F32) for _ in groups]
    for c in range(n_chunks):
        for i, r in enumerate(groups):
            nxt = gate_up(hf[i], c + 1) if c + 1 < n_chunks else None
            ff[i] = ff[i] + down(gu[i], c)
            gu[i] = nxt
            if c + 1 == n_chunks:
                finish(r, x1[i], ff[i])


def _out_ffn(layer, x, attn, cp, w_out, g_post_mix, g_pre_ffn, g_post_ffn, w_gate, w_up, w_down):
    T = x.shape[0]
    tm = FFN_TM
    const = lambda t: (layer, 0, 0)
    resident = functools.partial(pl.BlockSpec, index_map=const, pipeline_mode=pl.Buffered(1))
    whole = lambda t: (0, 0)
    return pl.pallas_call(
        functools.partial(_out_ffn_kernel, layer),
        out_shape=jax.ShapeDtypeStruct((T, D_MODEL), F32),
        grid=(T // tm,),
        in_specs=[
            pl.BlockSpec((tm, D_MODEL), lambda t: (t, 0)),
            pl.BlockSpec((tm, ATTN_WIDTH), lambda t: (t, 0)),
            pl.BlockSpec((tm, CONV_WIDTH + POOL_WIDTH), lambda t: (t, 0)),
            resident((None, D_MODEL, D_MODEL)),
            pl.BlockSpec(g_post_mix.shape, whole),
            pl.BlockSpec(g_pre_ffn.shape, whole),
            pl.BlockSpec(g_post_ffn.shape, whole),
            resident((None, D_MODEL, D_FF)),
            resident((None, D_MODEL, D_FF)),
            resident((None, D_FF, D_MODEL)),
        ],
        out_specs=pl.BlockSpec((tm, D_MODEL), lambda t: (t, 0)),
        compiler_params=pltpu.CompilerParams(
            dimension_semantics=("arbitrary",),
            vmem_limit_bytes=VMEM_LIMIT),
        name="out_ffn",
    )(x, attn, cp, w_out, g_post_mix, g_pre_ffn, g_post_ffn, w_gate, w_up, w_down)


def _block_diag(pool_w):
    L, G, C, _ = pool_w.shape
    eye = jnp.eye(G, dtype=pool_w.dtype)
    return (eye[None, :, None, :, None] * pool_w[:, :, :, None, :]).reshape(L, G * C, G * C)


def kernel(x, w_in, w_out, conv_w, pool_w, pool_scale, g_pre_mix, g_post_mix, g_pre_ffn,
           g_post_ffn, w_gate, w_up, w_down):
    B, S, D = x.shape
    depth = w_in.shape[0]
    assert D == D_MODEL and S % MOBA_BLOCK == 0 and S % IN_TM == 0 and (B * S) % FFN_TM == 0
    qx, kx = _attn_aux(S)
    w_out, w_gate, w_up, w_down = (w.astype(BF16) for w in (w_out, w_gate, w_up, w_down))
    pool_bd = _block_diag(pool_w).astype(BF16)
    for l in range(depth):
        qkv, cp = _in_proj(l, x, g_pre_mix, w_in, conv_w, pool_bd, pool_scale)
        attn = _moba_attn(qkv, qx, kx)
        x = _out_ffn(l, x.reshape(B * S, D), attn.reshape(B * S, ATTN_WIDTH),
                     cp.reshape(B * S, CONV_WIDTH + POOL_WIDTH), w_out,
                     g_post_mix, g_pre_ffn, g_post_ffn, w_gate, w_up, w_down).reshape(B, S, D)
    return x
```

```python
import functools

import jax
import jax.numpy as jnp
import numpy as np
from jax import lax
from jax.experimental import pallas as pl
from jax.experimental.pallas import tpu as pltpu

D_MODEL = 1024
HEAD_DIM = 64
ATTN_WIDTH = 512
ATTN_HEADS = 8
CONV_WIDTH = 256
CONV_K = 3
POOL_WIDTH = 256
POOL_WINDOWS = (2, 4, 8, 16)
POOL_GROUP_DIM = 64
IN_WIDTH = 2560
MOBA_BLOCK = 256
MOBA_TOPK = 3
D_FF = 2816
NORM_EPS = 1e-6

LANES = 128
HEAD_PAIRS = ATTN_WIDTH // LANES
HALO = 16
GATE_ROWS = 16
QKV_WIDTH = 3 * ATTN_WIDTH
CONV_OFF = QKV_WIDTH
POOL_OFF = CONV_OFF + 3 * CONV_WIDTH

IN_TM = 1024
IN_ROWS = 256
N_FFN_WEIGHTS = 4
FFN_TM = 1024
FFN_ROWS = 256
FFN_CHUNK = 256
VMEM_LIMIT = 56 * 1024 * 1024

BF16 = jnp.bfloat16
F32 = jnp.float32

LOG2E = float(np.log2(np.e))
Q_SCALE = HEAD_DIM ** -0.5 * LOG2E


def _rms(x, g):
    return x * lax.rsqrt(jnp.mean(x * x, axis=-1, keepdims=True) + NORM_EPS) * g


def _in_proj_kernel(layer, x_ref, g_ref, w32_ref, cw_ref, pw_ref, ps_ref, *rest):
    ffn32_refs, (qkv_ref, cp_ref), rest = rest[:N_FFN_WEIGHTS], rest[N_FFN_WEIGHTS:N_FFN_WEIGHTS + 2], rest[N_FFN_WEIGHTS + 2:]
    ffn16_refs, (w_ref, cbuf, pa, pb, pc, pd) = rest[:N_FFN_WEIGHTS], rest[N_FFN_WEIGHTS:]
    j = pl.program_id(1)
    tm = IN_ROWS
    gain = g_ref[layer:layer + 1, :]
    pool_scale = ps_ref[layer:layer + 1, :]

    @pl.when((pl.program_id(0) == 0) & (j == 0))
    def _():
        for c in range(0, IN_WIDTH, 2 * LANES):
            w_ref[:, c:c + 2 * LANES] = w32_ref[:, c:c + 2 * LANES].astype(BF16)

    @pl.when(j == 0)
    def _():
        cbuf[0:HALO, :] = jnp.zeros((HALO, CONV_WIDTH), F32)
        pa[0:2 * HALO, :] = jnp.zeros((2 * HALO, POOL_WIDTH), F32)
        pb[0:HALO, :] = jnp.zeros((HALO, POOL_WIDTH), F32)
        pc[0:HALO, :] = jnp.zeros((HALO, POOL_WIDTH), F32)
        pd[0:HALO, :] = jnp.zeros((HALO, POOL_WIDTH), F32)

    lane = lax.broadcasted_iota(jnp.int32, (1, POOL_WIDTH), 1)
    g0 = lane < POOL_GROUP_DIM
    g1 = lane < 2 * POOL_GROUP_DIM
    g2 = lane < 3 * POOL_GROUP_DIM
    w2, w4, w8, w16 = (F32(w) for w in POOL_WINDOWS)
    win = jnp.where(g0, w2, jnp.where(g1, w4, jnp.where(g2, w8, w16)))

    def normed(r):
        return _rms(x_ref[0, r, :], gain).astype(BF16)

    groups = [slice(r, r + tm) for r in range(0, x_ref.shape[1], tm)]
    h = normed(groups[0])
    for i, r in enumerate(groups):
        cv = jnp.dot(h, w_ref[:, CONV_OFF:POOL_OFF], preferred_element_type=F32)
        up = jnp.dot(h, w_ref[:, POOL_OFF:IN_WIDTH], preferred_element_type=F32)

        q = jnp.dot(h, w_ref[:, 0:ATTN_WIDTH], preferred_element_type=F32) * F32(Q_SCALE)
        qkv_ref[0, r, 0:ATTN_WIDTH] = q.astype(BF16)
        h_next = normed(groups[i + 1]) if i + 1 < len(groups) else None
        if i == 0:
            for src, dst in zip(ffn32_refs, ffn16_refs):
                dst[...] = src[...].astype(BF16)

        h_conv = cv[:, 0:CONV_WIDTH]
        b_gate = cv[:, CONV_WIDTH:2 * CONV_WIDTH]
        c_gate = cv[:, 2 * CONV_WIDTH:3 * CONV_WIDTH]
        u = c_gate * h_conv
        cbuf[HALO:HALO + tm, :] = u
        conv = (cbuf[HALO - 2:HALO - 2 + tm, :] * cw_ref[0:1, :]
                + cbuf[HALO - 1:HALO - 1 + tm, :] * cw_ref[1:2, :]
                + u * cw_ref[2:3, :])
        cp_ref[0, r, 0:CONV_WIDTH] = (b_gate * conv).astype(BF16)
        cbuf[0:HALO, :] = cbuf[tm:tm + HALO, :]

        qkv_ref[0, r, ATTN_WIDTH:QKV_WIDTH] = jnp.dot(
            h, w_ref[:, ATTN_WIDTH:QKV_WIDTH], preferred_element_type=F32).astype(BF16)

        pa[2 * HALO:2 * HALO + tm, :] = up
        n = tm + HALO
        pb[HALO:HALO + n, :] = pa[HALO:HALO + n, :] + pa[HALO - 1:HALO - 1 + n, :]
        pc[HALO:HALO + n, :] = pb[HALO:HALO + n, :] + pb[HALO - 2:HALO - 2 + n, :]
        pd[HALO:HALO + n, :] = pc[HALO:HALO + n, :] + pc[HALO - 4:HALO - 4 + n, :]
        s2 = pb[2 * HALO:2 * HALO + tm, :]
        s4 = pc[2 * HALO:2 * HALO + tm, :]
        s8 = pd[2 * HALO:2 * HALO + tm, :]
        s16 = s8 + pd[2 * HALO - 8:2 * HALO - 8 + tm, :]
        wsum = jnp.where(g0, s2, jnp.where(g1, s4, jnp.where(g2, s8, s16)))
        t1 = (j * x_ref.shape[1] + r.start + 1
              + lax.broadcasted_iota(jnp.int32, (tm, 1), 0)).astype(F32)
        cnt = jnp.minimum(t1, win)
        pooled = (wsum / cnt - up).astype(BF16)
        y = jnp.dot(pooled, pw_ref[...], preferred_element_type=F32) * pool_scale
        cp_ref[0, r, CONV_WIDTH:CONV_WIDTH + POOL_WIDTH] = y.astype(BF16)
        pa[HALO:2 * HALO, :] = pa[tm + HALO:tm + 2 * HALO, :]
        h = h_next


def _in_proj(layer, x, g, w_in, conv_w, pool_bd, pool_scale, ffn_weights):
    B, S, _ = x.shape
    tm = IN_TM
    n_steps = B * (S // tm)
    assert len(ffn_weights) == N_FFN_WEIGHTS
    assert all(w.shape[1] % (16 * n_steps) == 0 for w in ffn_weights)
    slice_spec = lambda w, idx: pl.BlockSpec((None,) * (w.ndim - 2) + (w.shape[-2] // n_steps, w.shape[-1]), idx)
    step = lambda b, j: b * (S // tm) + j
    const = lambda b, j: (layer, 0, 0)
    whole = lambda b, j: (0, 0)
    return pl.pallas_call(
        functools.partial(_in_proj_kernel, layer),
        out_shape=(jax.ShapeDtypeStruct((B, S, QKV_WIDTH), BF16),
                   jax.ShapeDtypeStruct((B, S, CONV_WIDTH + POOL_WIDTH), BF16),
                   *(jax.ShapeDtypeStruct(w.shape[1:], BF16) for w in ffn_weights)),
        grid=(B, S // tm),
        in_specs=[
            pl.BlockSpec((1, tm, D_MODEL), lambda b, j: (b, j, 0)),
            pl.BlockSpec(g.shape, whole),
            pl.BlockSpec((None, D_MODEL, IN_WIDTH), const, pipeline_mode=pl.Buffered(1)),
            pl.BlockSpec((None, CONV_K, CONV_WIDTH), const),
            pl.BlockSpec((None, POOL_WIDTH, POOL_WIDTH), const),
            pl.BlockSpec(pool_scale.shape, whole),
            *(slice_spec(w, lambda b, j: (layer, step(b, j), 0)) for w in ffn_weights),
        ],
        out_specs=(pl.BlockSpec((1, tm, QKV_WIDTH), lambda b, j: (b, j, 0)),
                   pl.BlockSpec((1, tm, CONV_WIDTH + POOL_WIDTH), lambda b, j: (b, j, 0)),
                   *(slice_spec(w[0], lambda b, j: (step(b, j), 0)) for w in ffn_weights)),
        scratch_shapes=[
            pltpu.VMEM((D_MODEL, IN_WIDTH), BF16),
            pltpu.VMEM((HALO + IN_ROWS, CONV_WIDTH), F32),
            pltpu.VMEM((2 * HALO + IN_ROWS, POOL_WIDTH), F32),
            pltpu.VMEM((2 * HALO + IN_ROWS, POOL_WIDTH), F32),
            pltpu.VMEM((2 * HALO + IN_ROWS, POOL_WIDTH), F32),
            pltpu.VMEM((2 * HALO + IN_ROWS, POOL_WIDTH), F32),
        ],
        compiler_params=pltpu.CompilerParams(
            dimension_semantics=("arbitrary", "arbitrary"),
            vmem_limit_bytes=VMEM_LIMIT),
        name="in_proj",
    )(x, g, w_in, conv_w, pool_bd, pool_scale, *ffn_weights)


NEG_BIG = -1e30
ALIBI_PARTS = 3
AUX_ALIBI_HI = 0
AUX_ALIBI_LO = ALIBI_PARTS
AUX_BLOCK0 = 8
VT_ROWS = HEAD_DIM + 16
EXP_LAG = 4
PV_LAG = 8
POS_SPLIT = 16


def _block_mask_t(g, q_blk):
    nb = g.shape[0]
    blk_idx = lax.broadcasted_iota(jnp.int32, g.shape, 0)
    elig = blk_idx < q_blk
    ranks = []
    for jb in range(nb):
        gj = g[jb:jb + 1, :]
        beats = ((g > gj) | ((g == gj) & (blk_idx < jb))) & elig
        ranks.append(jnp.sum(beats.astype(F32), axis=0, keepdims=True))
    rank = jnp.concatenate(ranks, axis=0)
    attend = ((rank < MOBA_TOPK) & elig) | (blk_idx == q_blk)
    return jnp.where(attend, F32(0.0), F32(NEG_BIG))


def _attn_kernel(q_ref, k_ref, v_ref, qx_ref, kx_ref, o_ref, qa_ref, ka_ref, vt_ref):
    S = k_ref.shape[1]
    blk = MOBA_BLOCK
    nb = S // blk
    k = k_ref[0]
    v = v_ref[0]
    lane = lax.broadcasted_iota(jnp.int32, (1, LANES), 1)
    nt = (((1,), (1,)), ((), ()))

    km = jnp.mean(k.astype(F32).reshape(nb, blk, LANES), axis=1)
    km = jnp.concatenate([km, jnp.zeros((GATE_ROWS - nb, LANES), F32)], axis=0)
    km_hi = km.astype(BF16)
    km_lo = (km - km_hi.astype(F32)).astype(BF16)

    gated = min((MOBA_TOPK + 1) * blk, S)
    q_free = q_ref[0, 0:gated, :]
    q_gate = q_ref[0, gated:S, :]
    q_blk = (gated + lax.broadcasted_iota(jnp.int32, (nb, S - gated), 1)) // blk
    in_heads = [(lane >= h * HEAD_DIM) & (lane < (h + 1) * HEAD_DIM) for h in range(2)]
    v_t = v.T
    ones_tile = jnp.where(lax.broadcasted_iota(jnp.int32, (VT_ROWS - HEAD_DIM, S), 0) == 0,
                          F32(1.0), F32(0.0)).astype(BF16)
    for h in range(2):
        alibi = qx_ref[0, h:h + 1, :]
        qa_ref[h, 0:gated, :] = jnp.where(
            in_heads[h], q_free, jnp.broadcast_to(alibi.astype(BF16), q_free.shape))
        ka_ref[h] = jnp.where(in_heads[h], k, kx_ref[h])
        vt_ref[h] = jnp.concatenate([v_t[h * HEAD_DIM:(h + 1) * HEAD_DIM], ones_tile], axis=0)

    def gate_queries(h):
        qh = jnp.where(in_heads[h], q_gate, jnp.zeros_like(q_gate))
        g = (lax.dot_general(km_hi, qh, nt, preferred_element_type=F32)
             + lax.dot_general(km_lo, qh, nt, preferred_element_type=F32))[0:nb]
        aux = (1 - h) * HEAD_DIM + AUX_BLOCK0
        mask_t = jnp.concatenate([jnp.zeros((aux, S - gated), F32), _block_mask_t(g, q_blk),
                                  jnp.zeros((LANES - aux - nb, S - gated), F32)], axis=0)
        extra = mask_t.T + qx_ref[0, h:h + 1, :]
        qa_ref[h, gated:S, :] = jnp.where(in_heads[h], q_gate, extra.astype(BF16))

    key = lax.broadcasted_iota(jnp.int32, (blk, blk), 0)
    qry = lax.broadcasted_iota(jnp.int32, (blk, blk), 1)
    causal = key <= qry

    def scores(c, h, j):
        s = lax.dot_general(ka_ref[h, j * blk:(j + 1) * blk, :], qa_ref[h, c * blk:(c + 1) * blk, :],
                            nt, preferred_element_type=F32)
        if j == c:
            s = jnp.where(causal, s, F32(NEG_BIG))
        return s, jnp.max(s, axis=0, keepdims=True)

    def probs(s, cm, m_old):
        m_new = cm if m_old is None else jnp.maximum(m_old, cm)
        alpha = None if m_old is None else jnp.exp2(m_old - m_new)
        return jnp.exp2(s - m_new).astype(BF16), m_new, alpha

    def accumulate(h, j, p, alpha, acc):
        pv = jnp.dot(vt_ref[h, :, j * blk:(j + 1) * blk], p, preferred_element_type=F32)
        return pv if acc is None else acc * alpha + pv

    for h in range(2):
        if gated < S:
            gate_queries(h)
    steps = [(c, h, c) for c in range(nb) for h in range(2)]
    steps += [(c, h, j) for j in range(nb) for c in range(j + 1, nb) for h in range(2)]
    last_block = {c: (c - 1 if c else 0) for c in range(nb)}
    m_run, acc_run, outs = {}, {}, {}
    staged_s, staged_p = {}, {}
    for t in range(len(steps) + PV_LAG):
        if t < len(steps):
            staged_s[t] = scores(*steps[t])
        if 0 <= t - EXP_LAG < len(steps):
            c, h, j = steps[t - EXP_LAG]
            sc, cm = staged_s.pop(t - EXP_LAG)
            p, m_run[c, h], alpha = probs(sc, cm, m_run.get((c, h)))
            staged_p[t - EXP_LAG] = (p, alpha)
        if 0 <= t - PV_LAG < len(steps):
            c, h, j = steps[t - PV_LAG]
            p, alpha = staged_p.pop(t - PV_LAG)
            acc_run[c, h] = accumulate(h, j, p, alpha, acc_run.get((c, h)))
            if j == last_block[c]:
                acc = acc_run.pop((c, h))
                outs[c, h] = acc[0:HEAD_DIM] * (1.0 / acc[HEAD_DIM:HEAD_DIM + 1])
                if (c, 0) in outs and (c, 1) in outs:
                    o_ref[0, c * blk:(c + 1) * blk, :] = jnp.concatenate(
                        [outs.pop((c, 0)), outs.pop((c, 1))], axis=0).T.astype(BF16)


def _moba_attn(qkv, qx, kx):
    B, S, _ = qkv.shape
    return pl.pallas_call(
        _attn_kernel,
        out_shape=jax.ShapeDtypeStruct((B, S, ATTN_WIDTH), BF16),
        grid=(B, HEAD_PAIRS),
        in_specs=[
            pl.BlockSpec((1, S, LANES), lambda b, hp: (b, 0, hp)),
            pl.BlockSpec((1, S, LANES), lambda b, hp: (b, 0, HEAD_PAIRS + hp)),
            pl.BlockSpec((1, S, LANES), lambda b, hp: (b, 0, 2 * HEAD_PAIRS + hp)),
            pl.BlockSpec((1, 2, LANES), lambda b, hp: (hp, 0, 0)),
            pl.BlockSpec((2, S, LANES), lambda b, hp: (0, 0, 0)),
        ],
        out_specs=pl.BlockSpec((1, S, LANES), lambda b, hp: (b, 0, hp)),
        scratch_shapes=[
            pltpu.VMEM((2, S, LANES), BF16),
            pltpu.VMEM((2, S, LANES), BF16),
            pltpu.VMEM((2, VT_ROWS, S), BF16),
        ],
        compiler_params=pltpu.CompilerParams(
            dimension_semantics=("arbitrary", "arbitrary"),
            vmem_limit_bytes=VMEM_LIMIT),
        name="moba_attn",
    )(qkv, qkv, qkv, qx, kx)


def _attn_aux(S):
    nb = S // MOBA_BLOCK
    assert S // POS_SPLIT <= 128 and 2 * ALIBI_PARTS <= AUX_BLOCK0 and AUX_BLOCK0 + nb <= HEAD_DIM
    slopes = 2.0 ** (-8.0 * np.arange(1, ATTN_HEADS + 1) / ATTN_HEADS)
    rest = (slopes * LOG2E).astype(np.float32)
    pos = np.arange(S)
    qx = np.zeros((ATTN_HEADS, LANES), np.float32)
    kx = np.zeros((2, S, LANES), np.float32)
    for i in range(ALIBI_PARTS):
        part = rest.astype(BF16).astype(np.float32)
        rest = rest - part
        for h in range(2):
            aux = (1 - h) * HEAD_DIM
            qx[h::2, aux + AUX_ALIBI_HI + i] = part[h::2] * POS_SPLIT
            qx[h::2, aux + AUX_ALIBI_LO + i] = part[h::2]
            kx[h, :, aux + AUX_ALIBI_HI + i] = pos // POS_SPLIT
            kx[h, :, aux + AUX_ALIBI_LO + i] = pos % POS_SPLIT
    for h in range(2):
        aux = (1 - h) * HEAD_DIM
        kx[h, pos, aux + AUX_BLOCK0 + pos // MOBA_BLOCK] = 1.0
    return jnp.asarray(qx.reshape(HEAD_PAIRS, 2, LANES)), jnp.asarray(kx, dtype=BF16)


def _out_ffn_kernel(layer, x_ref, a_ref, cp_ref, wo_ref, gpm_ref, gpf_ref, gqf_ref,
                    wg_ref, wu_ref, wd_ref, o_ref):
    g_post_mix, g_pre_ffn, g_post_ffn = (g[layer:layer + 1, :] for g in (gpm_ref, gpf_ref, gqf_ref))
    groups = [slice(r, r + FFN_ROWS) for r in range(0, x_ref.shape[0], FFN_ROWS)]
    n_chunks = D_FF // FFN_CHUNK

    def mix(r):
        return (jnp.dot(a_ref[r, :], wo_ref[0:ATTN_WIDTH, :], preferred_element_type=F32)
                + jnp.dot(cp_ref[r, :], wo_ref[ATTN_WIDTH:D_MODEL, :], preferred_element_type=F32))

    def norms(r, mixed):
        x1 = x_ref[r, :] + _rms(mixed, g_post_mix)
        return x1, _rms(x1, g_pre_ffn).astype(BF16)

    def gate_up(hf, c):
        cols = slice(c * FFN_CHUNK, (c + 1) * FFN_CHUNK)
        return (jnp.dot(hf, wg_ref[:, cols], preferred_element_type=F32),
                jnp.dot(hf, wu_ref[:, cols], preferred_element_type=F32))

    def down(gu, c):
        gate, up = gu
        act = (gate * jax.nn.sigmoid(gate) * up).astype(BF16)
        return jnp.dot(act, wd_ref[c * FFN_CHUNK:(c + 1) * FFN_CHUNK, :],
                       preferred_element_type=F32)

    def finish(r, x1, ff):
        o_ref[r, :] = x1 + _rms(ff, g_post_ffn)

    mixed = [mix(r) for r in groups]
    x1, hf, gu = [], [], []
    for r, mx in zip(groups, mixed):
        x1_r, hf_r = norms(r, mx)
        x1.append(x1_r)
        hf.append(hf_r)
        gu.append(gate_up(hf_r, 0))
    ff = [jnp.zeros((FFN_ROWS, D_MODEL), F32) for _ in groups]
    for c in range(n_chunks):
        for i, r in enumerate(groups):
            nxt = gate_up(hf[i], c + 1) if c + 1 < n_chunks else None
            ff[i] = ff[i] + down(gu[i], c)
            gu[i] = nxt
            if c + 1 == n_chunks:
                finish(r, x1[i], ff[i])


def _out_ffn(layer, x, attn, cp, w_out, g_post_mix, g_pre_ffn, g_post_ffn, w_gate, w_up, w_down):
    T = x.shape[0]
    tm = FFN_TM
    whole = lambda t: (0, 0)
    resident = functools.partial(pl.BlockSpec, index_map=whole, pipeline_mode=pl.Buffered(1))
    return pl.pallas_call(
        functools.partial(_out_ffn_kernel, layer),
        out_shape=jax.ShapeDtypeStruct((T, D_MODEL), F32),
        grid=(T // tm,),
        in_specs=[
            pl.BlockSpec((tm, D_MODEL), lambda t: (t, 0)),
            pl.BlockSpec((tm, ATTN_WIDTH), lambda t: (t, 0)),
            pl.BlockSpec((tm, CONV_WIDTH + POOL_WIDTH), lambda t: (t, 0)),
            resident((D_MODEL, D_MODEL)),
            pl.BlockSpec(g_post_mix.shape, whole),
            pl.BlockSpec(g_pre_ffn.shape, whole),
            pl.BlockSpec(g_post_ffn.shape, whole),
            resident((D_MODEL, D_FF)),
            resident((D_MODEL, D_FF)),
            resident((D_FF, D_MODEL)),
        ],
        out_specs=pl.BlockSpec((tm, D_MODEL), lambda t: (t, 0)),
        compiler_params=pltpu.CompilerParams(
            dimension_semantics=("arbitrary",),
            vmem_limit_bytes=VMEM_LIMIT),
        name="out_ffn",
    )(x, attn, cp, w_out, g_post_mix, g_pre_ffn, g_post_ffn, w_gate, w_up, w_down)


def _block_diag(pool_w):
    L, G, C, _ = pool_w.shape
    eye = jnp.eye(G, dtype=pool_w.dtype)
    return (eye[None, :, None, :, None] * pool_w[:, :, :, None, :]).reshape(L, G * C, G * C)


def kernel(x, w_in, w_out, conv_w, pool_w, pool_scale, g_pre_mix, g_post_mix, g_pre_ffn,
           g_post_ffn, w_gate, w_up, w_down):
    B, S, D = x.shape
    depth = w_in.shape[0]
    assert D == D_MODEL and S % MOBA_BLOCK == 0 and S % IN_TM == 0 and (B * S) % FFN_TM == 0
    qx, kx = _attn_aux(S)
    pool_bd = _block_diag(pool_w).astype(BF16)
    for l in range(depth):
        qkv, cp, wo, wg, wu, wd = _in_proj(l, x, g_pre_mix, w_in, conv_w, pool_bd, pool_scale,
                                           (w_out, w_gate, w_up, w_down))
        attn = _moba_attn(qkv, qx, kx)
        x = _out_ffn(l, x.reshape(B * S, D), attn.reshape(B * S, ATTN_WIDTH),
                     cp.reshape(B * S, CONV_WIDTH + POOL_WIDTH), wo,
                     g_post_mix, g_pre_ffn, g_post_ffn, wg, wu, wd).reshape(B, S, D)
    return x
```

```python
import functools

import jax
import jax.numpy as jnp
import numpy as np
from jax import lax
from jax.experimental import pallas as pl
from jax.experimental.pallas import tpu as pltpu

D_MODEL = 1024
HEAD_DIM = 64
ATTN_WIDTH = 512
ATTN_HEADS = 8
CONV_WIDTH = 256
CONV_K = 3
POOL_WIDTH = 256
POOL_WINDOWS = (2, 4, 8, 16)
POOL_GROUP_DIM = 64
IN_WIDTH = 2560
MOBA_BLOCK = 256
MOBA_TOPK = 3
D_FF = 2816
NORM_EPS = 1e-6

LANES = 128
HEAD_PAIRS = ATTN_WIDTH // LANES
HALO = 16
GATE_ROWS = 16
QKV_WIDTH = 3 * ATTN_WIDTH
CONV_OFF = QKV_WIDTH
POOL_OFF = CONV_OFF + 3 * CONV_WIDTH

IN_TM = 1024
IN_ROWS = 256
N_FFN_WEIGHTS = 4
FFN_TM = 1024
FFN_ROWS = 256
FFN_CHUNK = 256
VMEM_LIMIT = 56 * 1024 * 1024

BF16 = jnp.bfloat16
F32 = jnp.float32

LOG2E = float(np.log2(np.e))
Q_SCALE = HEAD_DIM ** -0.5 * LOG2E


def _rms(x, g):
    return x * lax.rsqrt(jnp.mean(x * x, axis=-1, keepdims=True) + NORM_EPS) * g


def _in_proj_kernel(layer, x_ref, g_ref, w32_ref, cw_ref, pw_ref, ps_ref, *rest):
    ffn32_refs, (qkv_ref, cp_ref), rest = rest[:N_FFN_WEIGHTS], rest[N_FFN_WEIGHTS:N_FFN_WEIGHTS + 2], rest[N_FFN_WEIGHTS + 2:]
    ffn16_refs, (w_ref, cbuf, pa, pb, pc, pd) = rest[:N_FFN_WEIGHTS], rest[N_FFN_WEIGHTS:]
    j = pl.program_id(1)
    tm = IN_ROWS
    gain = g_ref[layer:layer + 1, :]
    pool_scale = ps_ref[layer:layer + 1, :]

    @pl.when((pl.program_id(0) == 0) & (j == 0))
    def _():
        for c in range(0, IN_WIDTH, 2 * LANES):
            w_ref[:, c:c + 2 * LANES] = w32_ref[:, c:c + 2 * LANES].astype(BF16)

    @pl.when(j == 0)
    def _():
        cbuf[0:HALO, :] = jnp.zeros((HALO, CONV_WIDTH), F32)
        pa[0:2 * HALO, :] = jnp.zeros((2 * HALO, POOL_WIDTH), F32)
        pb[0:HALO, :] = jnp.zeros((HALO, POOL_WIDTH), F32)
        pc[0:HALO, :] = jnp.zeros((HALO, POOL_WIDTH), F32)
        pd[0:HALO, :] = jnp.zeros((HALO, POOL_WIDTH), F32)

    lane = lax.broadcasted_iota(jnp.int32, (1, POOL_WIDTH), 1)
    g0 = lane < POOL_GROUP_DIM
    g1 = lane < 2 * POOL_GROUP_DIM
    g2 = lane < 3 * POOL_GROUP_DIM
    w2, w4, w8, w16 = (F32(w) for w in POOL_WINDOWS)
    win = jnp.where(g0, w2, jnp.where(g1, w4, jnp.where(g2, w8, w16)))

    def normed(r):
        return _rms(x_ref[0, r, :], gain).astype(BF16)

    groups = [slice(r, r + tm) for r in range(0, x_ref.shape[1], tm)]
    h = normed(groups[0])
    for i, r in enumerate(groups):
        cv = jnp.dot(h, w_ref[:, CONV_OFF:POOL_OFF], preferred_element_type=F32)
        up = jnp.dot(h, w_ref[:, POOL_OFF:IN_WIDTH], preferred_element_type=F32)

        q = jnp.dot(h, w_ref[:, 0:ATTN_WIDTH], preferred_element_type=F32) * F32(Q_SCALE)
        qkv_ref[0, r, 0:ATTN_WIDTH] = q.astype(BF16)
        h_next = normed(groups[i + 1]) if i + 1 < len(groups) else None
        if i == 0:
            for src, dst in zip(ffn32_refs, ffn16_refs):
                dst[...] = src[...].astype(BF16)

        h_conv = cv[:, 0:CONV_WIDTH]
        b_gate = cv[:, CONV_WIDTH:2 * CONV_WIDTH]
        c_gate = cv[:, 2 * CONV_WIDTH:3 * CONV_WIDTH]
        u = c_gate * h_conv
        cbuf[HALO:HALO + tm, :] = u
        conv = (cbuf[HALO - 2:HALO - 2 + tm, :] * cw_ref[0:1, :]
                + cbuf[HALO - 1:HALO - 1 + tm, :] * cw_ref[1:2, :]
                + u * cw_ref[2:3, :])
        cp_ref[0, r, 0:CONV_WIDTH] = (b_gate * conv).astype(BF16)
        cbuf[0:HALO, :] = cbuf[tm:tm + HALO, :]

        qkv_ref[0, r, ATTN_WIDTH:QKV_WIDTH] = jnp.dot(
            h, w_ref[:, ATTN_WIDTH:QKV_WIDTH], preferred_element_type=F32).astype(BF16)

        pa[2 * HALO:2 * HALO + tm, :] = up
        n = tm + HALO
        pb[HALO:HALO + n, :] = pa[HALO:HALO + n, :] + pa[HALO - 1:HALO - 1 + n, :]
        pc[HALO:HALO + n, :] = pb[HALO:HALO + n, :] + pb[HALO - 2:HALO - 2 + n, :]
        pd[HALO:HALO + n, :] = pc[HALO:HALO + n, :] + pc[HALO - 4:HALO - 4 + n, :]
        s2 = pb[2 * HALO:2 * HALO + tm, :]
        s4 = pc[2 * HALO:2 * HALO + tm, :]
        s8 = pd[2 * HALO:2 * HALO + tm, :]
        s16 = s8 + pd[2 * HALO - 8:2 * HALO - 8 + tm, :]
        wsum = jnp.where(g0, s2, jnp.where(g1, s4, jnp.where(g2, s8, s16)))
        t1 = (j * x_ref.shape[1] + r.start + 1
              + lax.broadcasted_iota(jnp.int32, (tm, 1), 0)).astype(F32)
        cnt = jnp.minimum(t1, win)
        pooled = (wsum / cnt - up).astype(BF16)
        y = jnp.dot(pooled, pw_ref[...], preferred_element_type=F32) * pool_scale
        cp_ref[0, r, CONV_WIDTH:CONV_WIDTH + POOL_WIDTH] = y.astype(BF16)
        pa[HALO:2 * HALO, :] = pa[tm + HALO:tm + 2 * HALO, :]
        h = h_next


def _in_proj(layer, x, g, w_in, conv_w, pool_bd, pool_scale, ffn_weights):
    B, S, _ = x.shape
    tm = IN_TM
    n_steps = B * (S // tm)
    assert len(ffn_weights) == N_FFN_WEIGHTS
    assert all(w.shape[1] % (16 * n_steps) == 0 for w in ffn_weights)
    slice_spec = lambda w, idx: pl.BlockSpec((None,) * (w.ndim - 2) + (w.shape[-2] // n_steps, w.shape[-1]), idx)
    step = lambda b, j: b * (S // tm) + j
    const = lambda b, j: (layer, 0, 0)
    whole = lambda b, j: (0, 0)
    return pl.pallas_call(
        functools.partial(_in_proj_kernel, layer),
        out_shape=(jax.ShapeDtypeStruct((B, S, QKV_WIDTH), BF16),
                   jax.ShapeDtypeStruct((B, S, CONV_WIDTH + POOL_WIDTH), BF16),
                   *(jax.ShapeDtypeStruct(w.shape[1:], BF16) for w in ffn_weights)),
        grid=(B, S // tm),
        in_specs=[
            pl.BlockSpec((1, tm, D_MODEL), lambda b, j: (b, j, 0)),
            pl.BlockSpec(g.shape, whole),
            pl.BlockSpec((None, D_MODEL, IN_WIDTH), const, pipeline_mode=pl.Buffered(1)),
            pl.BlockSpec((None, CONV_K, CONV_WIDTH), const),
            pl.BlockSpec((None, POOL_WIDTH, POOL_WIDTH), const),
            pl.BlockSpec(pool_scale.shape, whole),
            *(slice_spec(w, lambda b, j: (layer, step(b, j), 0)) for w in ffn_weights),
        ],
        out_specs=(pl.BlockSpec((1, tm, QKV_WIDTH), lambda b, j: (b, j, 0)),
                   pl.BlockSpec((1, tm, CONV_WIDTH + POOL_WIDTH), lambda b, j: (b, j, 0)),
                   *(slice_spec(w[0], lambda b, j: (step(b, j), 0)) for w in ffn_weights)),
        scratch_shapes=[
            pltpu.VMEM((D_MODEL, IN_WIDTH), BF16),
            pltpu.VMEM((HALO + IN_ROWS, CONV_WIDTH), F32),
            pltpu.VMEM((2 * HALO + IN_ROWS, POOL_WIDTH), F32),
            pltpu.VMEM((2 * HALO + IN_ROWS, POOL_WIDTH), F32),
            pltpu.VMEM((2 * HALO + IN_ROWS, POOL_WIDTH), F32),
            pltpu.VMEM((2 * HALO + IN_ROWS, POOL_WIDTH), F32),
        ],
        compiler_params=pltpu.CompilerParams(
            dimension_semantics=("arbitrary", "arbitrary"),
            vmem_limit_bytes=VMEM_LIMIT),
        name="in_proj",
    )(x, g, w_in, conv_w, pool_bd, pool_scale, *ffn_weights)


NEG_BIG = -1e30
ALIBI_PARTS = 3
AUX_ALIBI_HI = 0
AUX_ALIBI_LO = ALIBI_PARTS
AUX_BLOCK0 = 8
VT_ROWS = HEAD_DIM + 16
EXP_LAG = 4
PV_LAG = 8
POS_SPLIT = 16


def _block_mask_t(g, q_blk):
    nb = g.shape[0]
    blk_idx = lax.broadcasted_iota(jnp.int32, g.shape, 0)
    elig = blk_idx < q_blk
    ranks = []
    for jb in range(nb):
        gj = g[jb:jb + 1, :]
        beats = ((g > gj) | ((g == gj) & (blk_idx < jb))) & elig
        ranks.append(jnp.sum(beats.astype(F32), axis=0, keepdims=True))
    rank = jnp.concatenate(ranks, axis=0)
    attend = ((rank < MOBA_TOPK) & elig) | (blk_idx == q_blk)
    return jnp.where(attend, F32(0.0), F32(NEG_BIG))


def _attn_kernel(q_ref, k_ref, v_ref, qx_ref, kx_ref, o_ref, qa_ref, ka_ref, vt_ref):
    S = k_ref.shape[1]
    blk = MOBA_BLOCK
    nb = S // blk
    k = k_ref[0]
    v = v_ref[0]
    lane = lax.broadcasted_iota(jnp.int32, (1, LANES), 1)
    nt = (((1,), (1,)), ((), ()))

    km = jnp.mean(k.astype(F32).reshape(nb, blk, LANES), axis=1)
    km = jnp.concatenate([km, jnp.zeros((GATE_ROWS - nb, LANES), F32)], axis=0)
    km_hi = km.astype(BF16)
    km_lo = (km - km_hi.astype(F32)).astype(BF16)

    gated = min((MOBA_TOPK + 1) * blk, S)
    q_free = q_ref[0, 0:gated, :]
    q_gate = q_ref[0, gated:S, :]
    q_blk = (gated + lax.broadcasted_iota(jnp.int32, (nb, S - gated), 1)) // blk
    in_heads = [(lane >= h * HEAD_DIM) & (lane < (h + 1) * HEAD_DIM) for h in range(2)]
    v_t = v.T
    ones_tile = jnp.where(lax.broadcasted_iota(jnp.int32, (VT_ROWS - HEAD_DIM, S), 0) == 0,
                          F32(1.0), F32(0.0)).astype(BF16)
    for h in range(2):
        alibi = qx_ref[0, h:h + 1, :]
        qa_ref[h, 0:gated, :] = jnp.where(
            in_heads[h], q_free, jnp.broadcast_to(alibi.astype(BF16), q_free.shape))
        ka_ref[h] = jnp.where(in_heads[h], k, kx_ref[h])
        vt_ref[h] = jnp.concatenate([v_t[h * HEAD_DIM:(h + 1) * HEAD_DIM], ones_tile], axis=0)

    def gate_queries(h):
        qh = jnp.where(in_heads[h], q_gate, jnp.zeros_like(q_gate))
        g = (lax.dot_general(km_hi, qh, nt, preferred_element_type=F32)
             + lax.dot_general(km_lo, qh, nt, preferred_element_type=F32))[0:nb]
        aux = (1 - h) * HEAD_DIM + AUX_BLOCK0
        mask_t = jnp.concatenate([jnp.zeros((aux, S - gated), F32), _block_mask_t(g, q_blk),
                                  jnp.zeros((LANES - aux - nb, S - gated), F32)], axis=0)
        extra = mask_t.T + qx_ref[0, h:h + 1, :]
        qa_ref[h, gated:S, :] = jnp.where(in_heads[h], q_gate, extra.astype(BF16))

    key = lax.broadcasted_iota(jnp.int32, (blk, blk), 0)
    qry = lax.broadcasted_iota(jnp.int32, (blk, blk), 1)
    causal = key <= qry

    def scores(c, h, j):
        s = lax.dot_general(ka_ref[h, j * blk:(j + 1) * blk, :], qa_ref[h, c * blk:(c + 1) * blk, :],
                            nt, preferred_element_type=F32)
        if j == c:
            s = jnp.where(causal, s, F32(NEG_BIG))
        return s, jnp.max(s, axis=0, keepdims=True)

    def probs(s, cm, m_old):
        m_new = cm if m_old is None else jnp.maximum(m_old, cm)
        alpha = None if m_old is None else jnp.exp2(m_old - m_new)
        return jnp.exp2(s - m_new).astype(BF16), m_new, alpha

    def accumulate(h, j, p, alpha, acc):
        pv = jnp.dot(vt_ref[h, :, j * blk:(j + 1) * blk], p, preferred_element_type=F32)
        return pv if acc is None else acc * alpha + pv

    for h in range(2):
        if gated < S:
            gate_queries(h)
    steps = [(c, h, c) for c in range(nb) for h in range(2)]
    steps += [(c, h, j) for j in range(nb) for c in range(j + 1, nb) for h in range(2)]
    last_block = {c: (c - 1 if c else 0) for c in range(nb)}
    m_run, acc_run, outs = {}, {}, {}
    staged_s, staged_p = {}, {}
    for t in range(len(steps) + PV_LAG):
        if t < len(steps):
            staged_s[t] = scores(*steps[t])
        if 0 <= t - EXP_LAG < len(steps):
            c, h, j = steps[t - EXP_LAG]
            sc, cm = staged_s.pop(t - EXP_LAG)
            p, m_run[c, h], alpha = probs(sc, cm, m_run.get((c, h)))
            staged_p[t - EXP_LAG] = (p, alpha)
        if 0 <= t - PV_LAG < len(steps):
            c, h, j = steps[t - PV_LAG]
            p, alpha = staged_p.pop(t - PV_LAG)
            acc_run[c, h] = accumulate(h, j, p, alpha, acc_run.get((c, h)))
            if j == last_block[c]:
                acc = acc_run.pop((c, h))
                outs[c, h] = acc[0:HEAD_DIM] * (1.0 / acc[HEAD_DIM:HEAD_DIM + 1])
                if (c, 0) in outs and (c, 1) in outs:
                    o_ref[0, c * blk:(c + 1) * blk, :] = jnp.concatenate(
                        [outs.pop((c, 0)), outs.pop((c, 1))], axis=0).T.astype(BF16)


def _moba_attn(qkv, qx, kx):
    B, S, _ = qkv.shape
    return pl.pallas_call(
        _attn_kernel,
        out_shape=jax.ShapeDtypeStruct((B, S, ATTN_WIDTH), BF16),
        grid=(B, HEAD_PAIRS),
        in_specs=[
            pl.BlockSpec((1, S, LANES), lambda b, hp: (b, 0, hp)),
            pl.BlockSpec((1, S, LANES), lambda b, hp: (b, 0, HEAD_PAIRS + hp)),
            pl.BlockSpec((1, S, LANES), lambda b, hp: (b, 0, 2 * HEAD_PAIRS + hp)),
            pl.BlockSpec((1, 2, LANES), lambda b, hp: (hp, 0, 0)),
            pl.BlockSpec((2, S, LANES), lambda b, hp: (0, 0, 0)),
        ],
        out_specs=pl.BlockSpec((1, S, LANES), lambda b, hp: (b, 0, hp)),
        scratch_shapes=[
            pltpu.VMEM((2, S, LANES), BF16),
            pltpu.VMEM((2, S, LANES), BF16),
            pltpu.VMEM((2, VT_ROWS, S), BF16),
        ],
        compiler_params=pltpu.CompilerParams(
            dimension_semantics=("arbitrary", "arbitrary"),
            vmem_limit_bytes=VMEM_LIMIT),
        name="moba_attn",
    )(qkv, qkv, qkv, qx, kx)


def _attn_aux(S):
    nb = S // MOBA_BLOCK
    assert S // POS_SPLIT <= 128 and 2 * ALIBI_PARTS <= AUX_BLOCK0 and AUX_BLOCK0 + nb <= HEAD_DIM
    slopes = 2.0 ** (-8.0 * np.arange(1, ATTN_HEADS + 1) / ATTN_HEADS)
    rest = (slopes * LOG2E).astype(np.float32)
    pos = np.arange(S)
    qx = np.zeros((ATTN_HEADS, LANES), np.float32)
    kx = np.zeros((2, S, LANES), np.float32)
    for i in range(ALIBI_PARTS):
        part = rest.astype(BF16).astype(np.float32)
        rest = rest - part
        for h in range(2):
            aux = (1 - h) * HEAD_DIM
            qx[h::2, aux + AUX_ALIBI_HI + i] = part[h::2] * POS_SPLIT
            qx[h::2, aux + AUX_ALIBI_LO + i] = part[h::2]
            kx[h, :, aux + AUX_ALIBI_HI + i] = pos // POS_SPLIT
            kx[h, :, aux + AUX_ALIBI_LO + i] = pos % POS_SPLIT
    for h in range(2):
        aux = (1 - h) * HEAD_DIM
        kx[h, pos, aux + AUX_BLOCK0 + pos // MOBA_BLOCK] = 1.0
    return jnp.asarray(qx.reshape(HEAD_PAIRS, 2, LANES)), jnp.asarray(kx, dtype=BF16)


def _out_ffn_kernel(layer, x_ref, a_ref, cp_ref, wo_ref, gpm_ref, gpf_ref, gqf_ref,
                    wg_ref, wu_ref, wd_ref, o_ref):
    g_post_mix, g_pre_ffn, g_post_ffn = (g[layer:layer + 1, :] for g in (gpm_ref, gpf_ref, gqf_ref))
    groups = [slice(r, r + FFN_ROWS) for r in range(0, x_ref.shape[0], FFN_ROWS)]
    n_chunks = D_FF // FFN_CHUNK

    def mix(r):
        return (jnp.dot(a_ref[r, :], wo_ref[0:ATTN_WIDTH, :], preferred_element_type=F32)
                + jnp.dot(cp_ref[r, :], wo_ref[ATTN_WIDTH:D_MODEL, :], preferred_element_type=F32))

    def norms(r, mixed):
        x1 = x_ref[r, :] + _rms(mixed, g_post_mix)
        return x1, _rms(x1, g_pre_ffn).astype(BF16)

    def gate_up(hf, c):
        cols = slice(c * FFN_CHUNK, (c + 1) * FFN_CHUNK)
        return (jnp.dot(hf, wg_ref[:, cols], preferred_element_type=F32),
                jnp.dot(hf, wu_ref[:, cols], preferred_element_type=F32))

    def down(gu, c):
        gate, up = gu
        act = (gate * jax.nn.sigmoid(gate) * up).astype(BF16)
        return jnp.dot(act, wd_ref[c * FFN_CHUNK:(c + 1) * FFN_CHUNK, :],
                       preferred_element_type=F32)

    def finish(r, x1, ff):
        o_ref[r, :] = x1 + _rms(ff, g_post_ffn)

    mixed = [mix(r) for r in groups]
    x1, hf, gu, ff = {}, {}, {}, {}
    for tick in range(n_chunks + len(groups) - 1):
        for i, r in enumerate(groups):
            c = tick - i
            if c == 0:
                x1[i], hf[i] = norms(r, mixed[i])
                gu[i] = gate_up(hf[i], 0)
                ff[i] = jnp.zeros((FFN_ROWS, D_MODEL), F32)
            if 0 <= c < n_chunks:
                nxt = gate_up(hf[i], c + 1) if c + 1 < n_chunks else None
                ff[i] = ff[i] + down(gu[i], c)
                gu[i] = nxt
                if c + 1 == n_chunks:
                    finish(r, x1[i], ff[i])


def _out_ffn(layer, x, attn, cp, w_out, g_post_mix, g_pre_ffn, g_post_ffn, w_gate, w_up, w_down):
    T = x.shape[0]
    tm = FFN_TM
    whole = lambda t: (0, 0)
    resident = functools.partial(pl.BlockSpec, index_map=whole, pipeline_mode=pl.Buffered(1))
    return pl.pallas_call(
        functools.partial(_out_ffn_kernel, layer),
        out_shape=jax.ShapeDtypeStruct((T, D_MODEL), F32),
        grid=(T // tm,),
        in_specs=[
            pl.BlockSpec((tm, D_MODEL), lambda t: (t, 0)),
            pl.BlockSpec((tm, ATTN_WIDTH), lambda t: (t, 0)),
            pl.BlockSpec((tm, CONV_WIDTH + POOL_WIDTH), lambda t: (t, 0)),
            resident((D_MODEL, D_MODEL)),
            pl.BlockSpec(g_post_mix.shape, whole),
            pl.BlockSpec(g_pre_ffn.shape, whole),
            pl.BlockSpec(g_post_ffn.shape, whole),
            resident((D_MODEL, D_FF)),
            resident((D_MODEL, D_FF)),
            resident((D_FF, D_MODEL)),
        ],
        out_specs=pl.BlockSpec((tm, D_MODEL), lambda t: (t, 0)),
        compiler_params=pltpu.CompilerParams(
            dimension_semantics=("arbitrary",),
            vmem_limit_bytes=VMEM_LIMIT),
        name="out_ffn",
    )(x, attn, cp, w_out, g_post_mix, g_pre_ffn, g_post_ffn, w_gate, w_up, w_down)


def _block_diag(pool_w):
    L, G, C, _ = pool_w.shape
    eye = jnp.eye(G, dtype=pool_w.dtype)
    return (eye[None, :, None, :, None] * pool_w[:, :, :, None, :]).reshape(L, G * C, G * C)


def kernel(x, w_in, w_out, conv_w, pool_w, pool_scale, g_pre_mix, g_post_mix, g_pre_ffn,
           g_post_ffn, w_gate, w_up, w_down):
    B, S, D = x.shape
    depth = w_in.shape[0]
    assert D == D_MODEL and S % MOBA_BLOCK == 0 and S % IN_TM == 0 and (B * S) % FFN_TM == 0
    qx, kx = _attn_aux(S)
    pool_bd = _block_diag(pool_w).astype(BF16)
    for l in range(depth):
        qkv, cp, wo, wg, wu, wd = _in_proj(l, x, g_pre_mix, w_in, conv_w, pool_bd, pool_scale,
                                           (w_out, w_gate, w_up, w_down))
        attn = _moba_attn(qkv, qx, kx)
        x = _out_ffn(l, x.reshape(B * S, D), attn.reshape(B * S, ATTN_WIDTH),
                     cp.reshape(B * S, CONV_WIDTH + POOL_WIDTH), wo,
                     g_post_mix, g_pre_ffn, g_post_ffn, wg, wu, wd).reshape(B, S, D)
    return x
```

```python
import functools

import jax
import jax.numpy as jnp
import numpy as np
from jax import lax
from jax.experimental import pallas as pl
from jax.experimental.pallas import tpu as pltpu

D_MODEL = 1024
HEAD_DIM = 64
ATTN_WIDTH = 512
ATTN_HEADS = 8
CONV_WIDTH = 256
CONV_K = 3
POOL_WIDTH = 256
POOL_WINDOWS = (2, 4, 8, 16)
POOL_GROUP_DIM = 64
IN_WIDTH = 2560
MOBA_BLOCK = 256
MOBA_TOPK = 3
D_FF = 2816
NORM_EPS = 1e-6

LANES = 128
HEAD_PAIRS = ATTN_WIDTH // LANES
HALO = 16
GATE_ROWS = 16
QKV_WIDTH = 3 * ATTN_WIDTH
CONV_OFF = QKV_WIDTH
POOL_OFF = CONV_OFF + 3 * CONV_WIDTH

IN_TM = 1024
IN_ROWS = 256
N_FFN_WEIGHTS = 4
FFN_TM = 1024
FFN_ROWS = 256
FFN_CHUNK = 256
VMEM_LIMIT = 56 * 1024 * 1024

BF16 = jnp.bfloat16
F32 = jnp.float32

LOG2E = float(np.log2(np.e))
Q_SCALE = HEAD_DIM ** -0.5 * LOG2E


def _rms(x, g):
    return x * lax.rsqrt(jnp.mean(x * x, axis=-1, keepdims=True) + NORM_EPS) * g


def _in_proj_kernel(layer, x_ref, g_ref, w32_ref, cw_ref, pw_ref, ps_ref, *rest):
    ffn32_refs, (qkv_ref, cp_ref), rest = rest[:N_FFN_WEIGHTS], rest[N_FFN_WEIGHTS:N_FFN_WEIGHTS + 2], rest[N_FFN_WEIGHTS + 2:]
    ffn16_refs, (w_ref, cbuf, pa, pb, pc, pd) = rest[:N_FFN_WEIGHTS], rest[N_FFN_WEIGHTS:]
    j = pl.program_id(1)
    tm = IN_ROWS
    gain = g_ref[layer:layer + 1, :]
    pool_scale = ps_ref[layer:layer + 1, :]

    @pl.when((pl.program_id(0) == 0) & (j == 0))
    def _():
        for c in range(0, IN_WIDTH, 2 * LANES):
            w_ref[:, c:c + 2 * LANES] = w32_ref[:, c:c + 2 * LANES].astype(BF16)

    @pl.when(j == 0)
    def _():
        cbuf[0:HALO, :] = jnp.zeros((HALO, CONV_WIDTH), F32)
        pa[0:2 * HALO, :] = jnp.zeros((2 * HALO, POOL_WIDTH), F32)
        pb[0:HALO, :] = jnp.zeros((HALO, POOL_WIDTH), F32)
        pc[0:HALO, :] = jnp.zeros((HALO, POOL_WIDTH), F32)
        pd[0:HALO, :] = jnp.zeros((HALO, POOL_WIDTH), F32)

    lane = lax.broadcasted_iota(jnp.int32, (1, POOL_WIDTH), 1)
    g0 = lane < POOL_GROUP_DIM
    g1 = lane < 2 * POOL_GROUP_DIM
    g2 = lane < 3 * POOL_GROUP_DIM
    w2, w4, w8, w16 = (F32(w) for w in POOL_WINDOWS)
    win = jnp.where(g0, w2, jnp.where(g1, w4, jnp.where(g2, w8, w16)))

    def normed(r):
        return _rms(x_ref[0, r, :], gain).astype(BF16)

    groups = [slice(r, r + tm) for r in range(0, x_ref.shape[1], tm)]
    h = normed(groups[0])
    for i, r in enumerate(groups):
        cv = jnp.dot(h, w_ref[:, CONV_OFF:POOL_OFF], preferred_element_type=F32)
        up = jnp.dot(h, w_ref[:, POOL_OFF:IN_WIDTH], preferred_element_type=F32)

        q = jnp.dot(h, w_ref[:, 0:ATTN_WIDTH], preferred_element_type=F32) * F32(Q_SCALE)
        qkv_ref[0, r, 0:ATTN_WIDTH] = q.astype(BF16)
        h_next = normed(groups[i + 1]) if i + 1 < len(groups) else None
        if i == 0:
            for src, dst in zip(ffn32_refs, ffn16_refs):
                dst[...] = src[...].astype(BF16)

        h_conv = cv[:, 0:CONV_WIDTH]
        b_gate = cv[:, CONV_WIDTH:2 * CONV_WIDTH]
        c_gate = cv[:, 2 * CONV_WIDTH:3 * CONV_WIDTH]
        u = c_gate * h_conv
        cbuf[HALO:HALO + tm, :] = u
        conv = (cbuf[HALO - 2:HALO - 2 + tm, :] * cw_ref[0:1, :]
                + cbuf[HALO - 1:HALO - 1 + tm, :] * cw_ref[1:2, :]
                + u * cw_ref[2:3, :])
        cp_ref[0, r, 0:CONV_WIDTH] = (b_gate * conv).astype(BF16)
        cbuf[0:HALO, :] = cbuf[tm:tm + HALO, :]

        qkv_ref[0, r, ATTN_WIDTH:QKV_WIDTH] = jnp.dot(
            h, w_ref[:, ATTN_WIDTH:QKV_WIDTH], preferred_element_type=F32).astype(BF16)

        pa[2 * HALO:2 * HALO + tm, :] = up
        n = tm + HALO
        pb[HALO:HALO + n, :] = pa[HALO:HALO + n, :] + pa[HALO - 1:HALO - 1 + n, :]
        pc[HALO:HALO + n, :] = pb[HALO:HALO + n, :] + pb[HALO - 2:HALO - 2 + n, :]
        pd[HALO:HALO + n, :] = pc[HALO:HALO + n, :] + pc[HALO - 4:HALO - 4 + n, :]
        s2 = pb[2 * HALO:2 * HALO + tm, :]
        s4 = pc[2 * HALO:2 * HALO + tm, :]
        s8 = pd[2 * HALO:2 * HALO + tm, :]
        s16 = s8 + pd[2 * HALO - 8:2 * HALO - 8 + tm, :]
        wsum = jnp.where(g0, s2, jnp.where(g1, s4, jnp.where(g2, s8, s16)))
        t1 = (j * x_ref.shape[1] + r.start + 1
              + lax.broadcasted_iota(jnp.int32, (tm, 1), 0)).astype(F32)
        cnt = jnp.minimum(t1, win)
        pooled = (wsum / cnt - up).astype(BF16)
        y = jnp.dot(pooled, pw_ref[...], preferred_element_type=F32) * pool_scale
        cp_ref[0, r, CONV_WIDTH:CONV_WIDTH + POOL_WIDTH] = y.astype(BF16)
        pa[HALO:2 * HALO, :] = pa[tm + HALO:tm + 2 * HALO, :]
        h = h_next


def _in_proj(layer, x, g, w_in, conv_w, pool_bd, pool_scale, ffn_weights):
    B, S, _ = x.shape
    tm = IN_TM
    n_steps = B * (S // tm)
    assert len(ffn_weights) == N_FFN_WEIGHTS
    assert all(w.shape[1] % (16 * n_steps) == 0 for w in ffn_weights)
    slice_spec = lambda w, idx: pl.BlockSpec((None,) * (w.ndim - 2) + (w.shape[-2] // n_steps, w.shape[-1]), idx)
    step = lambda b, j: b * (S // tm) + j
    const = lambda b, j: (layer, 0, 0)
    whole = lambda b, j: (0, 0)
    return pl.pallas_call(
        functools.partial(_in_proj_kernel, layer),
        out_shape=(jax.ShapeDtypeStruct((B, S, QKV_WIDTH), BF16),
                   jax.ShapeDtypeStruct((B, S, CONV_WIDTH + POOL_WIDTH), BF16),
                   *(jax.ShapeDtypeStruct(w.shape[1:], BF16) for w in ffn_weights)),
        grid=(B, S // tm),
        in_specs=[
            pl.BlockSpec((1, tm, D_MODEL), lambda b, j: (b, j, 0)),
            pl.BlockSpec(g.shape, whole),
            pl.BlockSpec((None, D_MODEL, IN_WIDTH), const, pipeline_mode=pl.Buffered(1)),
            pl.BlockSpec((None, CONV_K, CONV_WIDTH), const),
            pl.BlockSpec((None, POOL_WIDTH, POOL_WIDTH), const),
            pl.BlockSpec(pool_scale.shape, whole),
            *(slice_spec(w, lambda b, j: (layer, step(b, j), 0)) for w in ffn_weights),
        ],
        out_specs=(pl.BlockSpec((1, tm, QKV_WIDTH), lambda b, j: (b, j, 0)),
                   pl.BlockSpec((1, tm, CONV_WIDTH + POOL_WIDTH), lambda b, j: (b, j, 0)),
                   *(slice_spec(w[0], lambda b, j: (step(b, j), 0)) for w in ffn_weights)),
        scratch_shapes=[
            pltpu.VMEM((D_MODEL, IN_WIDTH), BF16),
            pltpu.VMEM((HALO + IN_ROWS, CONV_WIDTH), F32),
            pltpu.VMEM((2 * HALO + IN_ROWS, POOL_WIDTH), F32),
            pltpu.VMEM((2 * HALO + IN_ROWS, POOL_WIDTH), F32),
            pltpu.VMEM((2 * HALO + IN_ROWS, POOL_WIDTH), F32),
            pltpu.VMEM((2 * HALO + IN_ROWS, POOL_WIDTH), F32),
        ],
        compiler_params=pltpu.CompilerParams(
            dimension_semantics=("arbitrary", "arbitrary"),
            vmem_limit_bytes=VMEM_LIMIT),
        name="in_proj",
    )(x, g, w_in, conv_w, pool_bd, pool_scale, *ffn_weights)


NEG_BIG = -1e30
ALIBI_PARTS = 3
AUX_ALIBI_HI = 0
AUX_ALIBI_LO = ALIBI_PARTS
AUX_BLOCK0 = 8
VT_ROWS = HEAD_DIM + 16
EXP_LAG = 4
PV_LAG = 8
PAIRS_PER_STEP = 4
POS_SPLIT = 16


def _block_mask_t(g, q_blk):
    nb = g.shape[0]
    blk_idx = lax.broadcasted_iota(jnp.int32, g.shape, 0)
    elig = blk_idx < q_blk
    ranks = []
    for jb in range(nb):
        gj = g[jb:jb + 1, :]
        beats = ((g > gj) | ((g == gj) & (blk_idx < jb))) & elig
        ranks.append(jnp.sum(beats.astype(F32), axis=0, keepdims=True))
    rank = jnp.concatenate(ranks, axis=0)
    attend = ((rank < MOBA_TOPK) & elig) | (blk_idx == q_blk)
    return jnp.where(attend, F32(0.0), F32(NEG_BIG))


def _attn_kernel(q_ref, k_ref, v_ref, qx_ref, kx_ref, o_ref, qa_ref, ka_ref, vt_ref):
    S = k_ref.shape[1]
    blk = MOBA_BLOCK
    nb = S // blk
    n_heads = 2 * PAIRS_PER_STEP
    lane = lax.broadcasted_iota(jnp.int32, (1, LANES), 1)
    nt = (((1,), (1,)), ((), ()))
    in_heads = [(lane >= h * HEAD_DIM) & (lane < (h + 1) * HEAD_DIM) for h in range(2)]

    gated = min((MOBA_TOPK + 1) * blk, S)
    q_blk = (gated + lax.broadcasted_iota(jnp.int32, (nb, S - gated), 1)) // blk
    ones_tile = jnp.where(lax.broadcasted_iota(jnp.int32, (VT_ROWS - HEAD_DIM, S), 0) == 0,
                          F32(1.0), F32(0.0)).astype(BF16)
    for p in range(PAIRS_PER_STEP):
        cols = slice(p * LANES, (p + 1) * LANES)
        k = k_ref[0, :, cols]
        km = jnp.mean(k.astype(F32).reshape(nb, blk, LANES), axis=1)
        km = jnp.concatenate([km, jnp.zeros((GATE_ROWS - nb, LANES), F32)], axis=0)
        km_hi = km.astype(BF16)
        km_lo = (km - km_hi.astype(F32)).astype(BF16)
        q_free = q_ref[0, 0:gated, cols]
        q_gate = q_ref[0, gated:S, cols]
        v_t = v_ref[0, :, cols].T
        for h in range(2):
            ph = 2 * p + h
            alibi = qx_ref[p, h:h + 1, :]
            qa_ref[ph, 0:gated, :] = jnp.where(
                in_heads[h], q_free, jnp.broadcast_to(alibi.astype(BF16), q_free.shape))
            ka_ref[ph] = jnp.where(in_heads[h], k, kx_ref[h])
            vt_ref[ph] = jnp.concatenate([v_t[h * HEAD_DIM:(h + 1) * HEAD_DIM], ones_tile], axis=0)
            if gated < S:
                qh = jnp.where(in_heads[h], q_gate, jnp.zeros_like(q_gate))
                g = (lax.dot_general(km_hi, qh, nt, preferred_element_type=F32)
                     + lax.dot_general(km_lo, qh, nt, preferred_element_type=F32))[0:nb]
                aux = (1 - h) * HEAD_DIM + AUX_BLOCK0
                mask_t = jnp.concatenate([jnp.zeros((aux, S - gated), F32), _block_mask_t(g, q_blk),
                                          jnp.zeros((LANES - aux - nb, S - gated), F32)], axis=0)
                extra = mask_t.T + alibi
                qa_ref[ph, gated:S, :] = jnp.where(in_heads[h], q_gate, extra.astype(BF16))

    key = lax.broadcasted_iota(jnp.int32, (blk, blk), 0)
    qry = lax.broadcasted_iota(jnp.int32, (blk, blk), 1)
    causal = key <= qry

    def scores(c, h, j):
        s = lax.dot_general(ka_ref[h, j * blk:(j + 1) * blk, :], qa_ref[h, c * blk:(c + 1) * blk, :],
                            nt, preferred_element_type=F32)
        if j == c:
            s = jnp.where(causal, s, F32(NEG_BIG))
        return s, jnp.max(s, axis=0, keepdims=True)

    def probs(s, cm, m_old):
        m_new = cm if m_old is None else jnp.maximum(m_old, cm)
        alpha = None if m_old is None else jnp.exp2(m_old - m_new)
        return jnp.exp2(s - m_new).astype(BF16), m_new, alpha

    def accumulate(h, j, p, alpha, acc):
        pv = jnp.dot(vt_ref[h, :, j * blk:(j + 1) * blk], p, preferred_element_type=F32)
        return pv if acc is None else acc * alpha + pv

    steps = [(c, h, c) for c in range(nb) for h in range(n_heads)]
    steps += [(c, h, j) for j in range(nb) for c in range(j + 1, nb) for h in range(n_heads)]
    last_block = {c: (c - 1 if c else 0) for c in range(nb)}
    m_run, acc_run, outs = {}, {}, {}
    staged_s, staged_p = {}, {}
    for t in range(len(steps) + PV_LAG):
        if t < len(steps):
            staged_s[t] = scores(*steps[t])
        if 0 <= t - EXP_LAG < len(steps):
            c, h, j = steps[t - EXP_LAG]
            sc, cm = staged_s.pop(t - EXP_LAG)
            p, m_run[c, h], alpha = probs(sc, cm, m_run.get((c, h)))
            staged_p[t - EXP_LAG] = (p, alpha)
        if 0 <= t - PV_LAG < len(steps):
            c, h, j = steps[t - PV_LAG]
            p, alpha = staged_p.pop(t - PV_LAG)
            acc_run[c, h] = accumulate(h, j, p, alpha, acc_run.get((c, h)))
            if j == last_block[c]:
                acc = acc_run.pop((c, h))
                outs[c, h] = acc[0:HEAD_DIM] * (1.0 / acc[HEAD_DIM:HEAD_DIM + 1])
                pair = h // 2
                if (c, 2 * pair) in outs and (c, 2 * pair + 1) in outs:
                    o_ref[0, c * blk:(c + 1) * blk, pair * LANES:(pair + 1) * LANES] = jnp.concatenate(
                        [outs.pop((c, 2 * pair)), outs.pop((c, 2 * pair + 1))], axis=0).T.astype(BF16)


def _moba_attn(qkv, qx, kx):
    B, S, _ = qkv.shape
    width = PAIRS_PER_STEP * LANES
    n_groups = HEAD_PAIRS // PAIRS_PER_STEP
    return pl.pallas_call(
        _attn_kernel,
        out_shape=jax.ShapeDtypeStruct((B, S, ATTN_WIDTH), BF16),
        grid=(B, n_groups),
        in_specs=[
            pl.BlockSpec((1, S, width), lambda b, g: (b, 0, g)),
            pl.BlockSpec((1, S, width), lambda b, g: (b, 0, n_groups + g)),
            pl.BlockSpec((1, S, width), lambda b, g: (b, 0, 2 * n_groups + g)),
            pl.BlockSpec((PAIRS_PER_STEP, 2, LANES), lambda b, g: (g, 0, 0)),
            pl.BlockSpec((2, S, LANES), lambda b, g: (0, 0, 0)),
        ],
        out_specs=pl.BlockSpec((1, S, width), lambda b, g: (b, 0, g)),
        scratch_shapes=[
            pltpu.VMEM((2 * PAIRS_PER_STEP, S, LANES), BF16),
            pltpu.VMEM((2 * PAIRS_PER_STEP, S, LANES), BF16),
            pltpu.VMEM((2 * PAIRS_PER_STEP, VT_ROWS, S), BF16),
        ],
        compiler_params=pltpu.CompilerParams(
            dimension_semantics=("arbitrary", "arbitrary"),
            vmem_limit_bytes=VMEM_LIMIT),
        name="moba_attn",
    )(qkv, qkv, qkv, qx, kx)


def _attn_aux(S):
    nb = S // MOBA_BLOCK
    assert S // POS_SPLIT <= 128 and 2 * ALIBI_PARTS <= AUX_BLOCK0 and AUX_BLOCK0 + nb <= HEAD_DIM
    slopes = 2.0 ** (-8.0 * np.arange(1, ATTN_HEADS + 1) / ATTN_HEADS)
    rest = (slopes * LOG2E).astype(np.float32)
    pos = np.arange(S)
    qx = np.zeros((ATTN_HEADS, LANES), np.float32)
    kx = np.zeros((2, S, LANES), np.float32)
    for i in range(ALIBI_PARTS):
        part = rest.astype(BF16).astype(np.float32)
        rest = rest - part
        for h in range(2):
            aux = (1 - h) * HEAD_DIM
            qx[h::2, aux + AUX_ALIBI_HI + i] = part[h::2] * POS_SPLIT
            qx[h::2, aux + AUX_ALIBI_LO + i] = part[h::2]
            kx[h, :, aux + AUX_ALIBI_HI + i] = pos // POS_SPLIT
            kx[h, :, aux + AUX_ALIBI_LO + i] = pos % POS_SPLIT
    for h in range(2):
        aux = (1 - h) * HEAD_DIM
        kx[h, pos, aux + AUX_BLOCK0 + pos // MOBA_BLOCK] = 1.0
    return jnp.asarray(qx.reshape(HEAD_PAIRS, 2, LANES)), jnp.asarray(kx, dtype=BF16)


def _out_ffn_kernel(layer, x_ref, a_ref, cp_ref, wo_ref, gpm_ref, gpf_ref, gqf_ref,
                    wg_ref, wu_ref, wd_ref, o_ref):
    g_post_mix, g_pre_ffn, g_post_ffn = (g[layer:layer + 1, :] for g in (gpm_ref, gpf_ref, gqf_ref))
    groups = [slice(r, r + FFN_ROWS) for r in range(0, x_ref.shape[0], FFN_ROWS)]
    n_chunks = D_FF // FFN_CHUNK

    def mix(r):
        return (jnp.dot(a_ref[r, :], wo_ref[0:ATTN_WIDTH, :], preferred_element_type=F32)
                + jnp.dot(cp_ref[r, :], wo_ref[ATTN_WIDTH:D_MODEL, :], preferred_element_type=F32))

    def norms(r, mixed):
        x1 = x_ref[r, :] + _rms(mixed, g_post_mix)
        return x1, _rms(x1, g_pre_ffn).astype(BF16)

    def gate_up(hf, c):
        cols = slice(c * FFN_CHUNK, (c + 1) * FFN_CHUNK)
        return (jnp.dot(hf, wg_ref[:, cols], preferred_element_type=F32),
                jnp.dot(hf, wu_ref[:, cols], preferred_element_type=F32))

    def down(gu, c):
        gate, up = gu
        act = (gate * jax.nn.sigmoid(gate) * up).astype(BF16)
        return jnp.dot(act, wd_ref[c * FFN_CHUNK:(c + 1) * FFN_CHUNK, :],
                       preferred_element_type=F32)

    def finish(r, x1, ff):
        o_ref[r, :] = x1 + _rms(ff, g_post_ffn)

    mixed = [mix(r) for r in groups]
    x1, hf, gu, ff = {}, {}, {}, {}
    for tick in range(n_chunks + len(groups) - 1):
        for i, r in enumerate(groups):
            c = tick - i
            if c == 0:
                x1[i], hf[i] = norms(r, mixed[i])
                gu[i] = gate_up(hf[i], 0)
                ff[i] = jnp.zeros((FFN_ROWS, D_MODEL), F32)
            if 0 <= c < n_chunks:
                nxt = gate_up(hf[i], c + 1) if c + 1 < n_chunks else None
                ff[i] = ff[i] + down(gu[i], c)
                gu[i] = nxt
                if c + 1 == n_chunks:
                    finish(r, x1[i], ff[i])


def _out_ffn(layer, x, attn, cp, w_out, g_post_mix, g_pre_ffn, g_post_ffn, w_gate, w_up, w_down):
    T = x.shape[0]
    tm = FFN_TM
    whole = lambda t: (0, 0)
    resident = functools.partial(pl.BlockSpec, index_map=whole, pipeline_mode=pl.Buffered(1))
    return pl.pallas_call(
        functools.partial(_out_ffn_kernel, layer),
        out_shape=jax.ShapeDtypeStruct((T, D_MODEL), F32),
        grid=(T // tm,),
        in_specs=[
            pl.BlockSpec((tm, D_MODEL), lambda t: (t, 0)),
            pl.BlockSpec((tm, ATTN_WIDTH), lambda t: (t, 0)),
            pl.BlockSpec((tm, CONV_WIDTH + POOL_WIDTH), lambda t: (t, 0)),
            resident((D_MODEL, D_MODEL)),
            pl.BlockSpec(g_post_mix.shape, whole),
            pl.BlockSpec(g_pre_ffn.shape, whole),
            pl.BlockSpec(g_post_ffn.shape, whole),
            resident((D_MODEL, D_FF)),
            resident((D_MODEL, D_FF)),
            resident((D_FF, D_MODEL)),
        ],
        out_specs=pl.BlockSpec((tm, D_MODEL), lambda t: (t, 0)),
        compiler_params=pltpu.CompilerParams(
            dimension_semantics=("arbitrary",),
            vmem_limit_bytes=VMEM_LIMIT),
        name="out_ffn",
    )(x, attn, cp, w_out, g_post_mix, g_pre_ffn, g_post_ffn, w_gate, w_up, w_down)


def _block_diag(pool_w):
    L, G, C, _ = pool_w.shape
    eye = jnp.eye(G, dtype=pool_w.dtype)
    return (eye[None, :, None, :, None] * pool_w[:, :, :, None, :]).reshape(L, G * C, G * C)


def kernel(x, w_in, w_out, conv_w, pool_w, pool_scale, g_pre_mix, g_post_mix, g_pre_ffn,
           g_post_ffn, w_gate, w_up, w_down):
    B, S, D = x.shape
    depth = w_in.shape[0]
    assert D == D_MODEL and S % MOBA_BLOCK == 0 and S % IN_TM == 0 and (B * S) % FFN_TM == 0
    qx, kx = _attn_aux(S)
    pool_bd = _block_diag(pool_w).astype(BF16)
    for l in range(depth):
        qkv, cp, wo, wg, wu, wd = _in_proj(l, x, g_pre_mix, w_in, conv_w, pool_bd, pool_scale,
                                           (w_out, w_gate, w_up, w_down))
        attn = _moba_attn(qkv, qx, kx)
        x = _out_ffn(l, x.reshape(B * S, D), attn.reshape(B * S, ATTN_WIDTH),
                     cp.reshape(B * S, CONV_WIDTH + POOL_WIDTH), wo,
                     g_post_mix, g_pre_ffn, g_post_ffn, wg, wu, wd).reshape(B, S, D)
    return x
```

```python
import functools

import jax
import jax.numpy as jnp
import numpy as np
from jax import lax
from jax.experimental import pallas as pl
from jax.experimental.pallas import tpu as pltpu

D_MODEL = 1024
HEAD_DIM = 64
ATTN_WIDTH = 512
ATTN_HEADS = 8
CONV_WIDTH = 256
CONV_K = 3
POOL_WIDTH = 256
POOL_WINDOWS = (2, 4, 8, 16)
POOL_GROUP_DIM = 64
IN_WIDTH = 2560
MOBA_BLOCK = 256
MOBA_TOPK = 3
D_FF = 2816
NORM_EPS = 1e-6

LANES = 128
HEAD_PAIRS = ATTN_WIDTH // LANES
HALO = 16
GATE_ROWS = 16
QKV_WIDTH = 3 * ATTN_WIDTH
CONV_OFF = QKV_WIDTH
POOL_OFF = CONV_OFF + 3 * CONV_WIDTH

IN_TM = 1024
IN_ROWS = 256
N_FFN_WEIGHTS = 4
FFN_TM = 1024
FFN_ROWS = 256
FFN_CHUNK = 256
VMEM_LIMIT = 56 * 1024 * 1024

BF16 = jnp.bfloat16
F32 = jnp.float32

LOG2E = float(np.log2(np.e))
Q_SCALE = HEAD_DIM ** -0.5 * LOG2E


def _rms(x, g):
    return x * lax.rsqrt(jnp.mean(x * x, axis=-1, keepdims=True) + NORM_EPS) * g


def _in_proj_kernel(layer, x_ref, g_ref, w32_ref, cw_ref, pw_ref, ps_ref, *rest):
    ffn32_refs, (qkv_ref, cp_ref), rest = rest[:N_FFN_WEIGHTS], rest[N_FFN_WEIGHTS:N_FFN_WEIGHTS + 2], rest[N_FFN_WEIGHTS + 2:]
    ffn16_refs, (w_ref, cbuf, pa, pb, pc, pd) = rest[:N_FFN_WEIGHTS], rest[N_FFN_WEIGHTS:]
    j = pl.program_id(1)
    tm = IN_ROWS
    gain = g_ref[layer:layer + 1, :]
    pool_scale = ps_ref[layer:layer + 1, :]

    @pl.when((pl.program_id(0) == 0) & (j == 0))
    def _():
        for c in range(0, IN_WIDTH, 2 * LANES):
            w_ref[:, c:c + 2 * LANES] = w32_ref[:, c:c + 2 * LANES].astype(BF16)

    @pl.when(j == 0)
    def _():
        cbuf[0:HALO, :] = jnp.zeros((HALO, CONV_WIDTH), F32)
        pa[0:2 * HALO, :] = jnp.zeros((2 * HALO, POOL_WIDTH), F32)
        pb[0:HALO, :] = jnp.zeros((HALO, POOL_WIDTH), F32)
        pc[0:HALO, :] = jnp.zeros((HALO, POOL_WIDTH), F32)
        pd[0:HALO, :] = jnp.zeros((HALO, POOL_WIDTH), F32)

    lane = lax.broadcasted_iota(jnp.int32, (1, POOL_WIDTH), 1)
    g0 = lane < POOL_GROUP_DIM
    g1 = lane < 2 * POOL_GROUP_DIM
    g2 = lane < 3 * POOL_GROUP_DIM
    w2, w4, w8, w16 = (F32(w) for w in POOL_WINDOWS)
    win = jnp.where(g0, w2, jnp.where(g1, w4, jnp.where(g2, w8, w16)))

    def normed(r):
        return _rms(x_ref[0, r, :], gain).astype(BF16)

    groups = [slice(r, r + tm) for r in range(0, x_ref.shape[1], tm)]
    h = normed(groups[0])
    for i, r in enumerate(groups):
        cv = jnp.dot(h, w_ref[:, CONV_OFF:POOL_OFF], preferred_element_type=F32)
        up = jnp.dot(h, w_ref[:, POOL_OFF:IN_WIDTH], preferred_element_type=F32)

        q = jnp.dot(h, w_ref[:, 0:ATTN_WIDTH], preferred_element_type=F32) * F32(Q_SCALE)
        qkv_ref[0, r, 0:ATTN_WIDTH] = q.astype(BF16)
        h_next = normed(groups[i + 1]) if i + 1 < len(groups) else None
        if i == 0:
            for src, dst in zip(ffn32_refs, ffn16_refs):
                dst[...] = src[...].astype(BF16)

        h_conv = cv[:, 0:CONV_WIDTH]
        b_gate = cv[:, CONV_WIDTH:2 * CONV_WIDTH]
        c_gate = cv[:, 2 * CONV_WIDTH:3 * CONV_WIDTH]
        u = c_gate * h_conv
        cbuf[HALO:HALO + tm, :] = u
        conv = (cbuf[HALO - 2:HALO - 2 + tm, :] * cw_ref[0:1, :]
                + cbuf[HALO - 1:HALO - 1 + tm, :] * cw_ref[1:2, :]
                + u * cw_ref[2:3, :])
        cp_ref[0, r, 0:CONV_WIDTH] = (b_gate * conv).astype(BF16)
        cbuf[0:HALO, :] = cbuf[tm:tm + HALO, :]

        qkv_ref[0, r, ATTN_WIDTH:QKV_WIDTH] = jnp.dot(
            h, w_ref[:, ATTN_WIDTH:QKV_WIDTH], preferred_element_type=F32).astype(BF16)

        pa[2 * HALO:2 * HALO + tm, :] = up
        n = tm + HALO
        pb[HALO:HALO + n, :] = pa[HALO:HALO + n, :] + pa[HALO - 1:HALO - 1 + n, :]
        pc[HALO:HALO + n, :] = pb[HALO:HALO + n, :] + pb[HALO - 2:HALO - 2 + n, :]
        pd[HALO:HALO + n, :] = pc[HALO:HALO + n, :] + pc[HALO - 4:HALO - 4 + n, :]
        s2 = pb[2 * HALO:2 * HALO + tm, :]
        s4 = pc[2 * HALO:2 * HALO + tm, :]
        s8 = pd[2 * HALO:2 * HALO + tm, :]
        s16 = s8 + pd[2 * HALO - 8:2 * HALO - 8 + tm, :]
        wsum = jnp.where(g0, s2, jnp.where(g1, s4, jnp.where(g2, s8, s16)))
        t1 = (j * x_ref.shape[1] + r.start + 1
              + lax.broadcasted_iota(jnp.int32, (tm, 1), 0)).astype(F32)
        cnt = jnp.minimum(t1, win)
        pooled = (wsum / cnt - up).astype(BF16)
        y = jnp.dot(pooled, pw_ref[...], preferred_element_type=F32) * pool_scale
        cp_ref[0, r, CONV_WIDTH:CONV_WIDTH + POOL_WIDTH] = y.astype(BF16)
        pa[HALO:2 * HALO, :] = pa[tm + HALO:tm + 2 * HALO, :]
        h = h_next


def _in_proj(layer, x, g, w_in, conv_w, pool_bd, pool_scale, ffn_weights):
    B, S, _ = x.shape
    tm = IN_TM
    n_steps = B * (S // tm)
    assert len(ffn_weights) == N_FFN_WEIGHTS
    assert all(w.shape[1] % (16 * n_steps) == 0 for w in ffn_weights)
    slice_spec = lambda w, idx: pl.BlockSpec((None,) * (w.ndim - 2) + (w.shape[-2] // n_steps, w.shape[-1]), idx)
    step = lambda b, j: b * (S // tm) + j
    const = lambda b, j: (layer, 0, 0)
    whole = lambda b, j: (0, 0)
    return pl.pallas_call(
        functools.partial(_in_proj_kernel, layer),
        out_shape=(jax.ShapeDtypeStruct((B, S, QKV_WIDTH), BF16),
                   jax.ShapeDtypeStruct((B, S, CONV_WIDTH + POOL_WIDTH), BF16),
                   *(jax.ShapeDtypeStruct(w.shape[1:], BF16) for w in ffn_weights)),
        grid=(B, S // tm),
        in_specs=[
            pl.BlockSpec((1, tm, D_MODEL), lambda b, j: (b, j, 0)),
            pl.BlockSpec(g.shape, whole),
            pl.BlockSpec((None, D_MODEL, IN_WIDTH), const, pipeline_mode=pl.Buffered(1)),
            pl.BlockSpec((None, CONV_K, CONV_WIDTH), const),
            pl.BlockSpec((None, POOL_WIDTH, POOL_WIDTH), const),
            pl.BlockSpec(pool_scale.shape, whole),
            *(slice_spec(w, lambda b, j: (layer, step(b, j), 0)) for w in ffn_weights),
        ],
        out_specs=(pl.BlockSpec((1, tm, QKV_WIDTH), lambda b, j: (b, j, 0)),
                   pl.BlockSpec((1, tm, CONV_WIDTH + POOL_WIDTH), lambda b, j: (b, j, 0)),
                   *(slice_spec(w[0], lambda b, j: (step(b, j), 0)) for w in ffn_weights)),
        scratch_shapes=[
            pltpu.VMEM((D_MODEL, IN_WIDTH), BF16),
            pltpu.VMEM((HALO + IN_ROWS, CONV_WIDTH), F32),
            pltpu.VMEM((2 * HALO + IN_ROWS, POOL_WIDTH), F32),
            pltpu.VMEM((2 * HALO + IN_ROWS, POOL_WIDTH), F32),
            pltpu.VMEM((2 * HALO + IN_ROWS, POOL_WIDTH), F32),
            pltpu.VMEM((2 * HALO + IN_ROWS, POOL_WIDTH), F32),
        ],
        compiler_params=pltpu.CompilerParams(
            dimension_semantics=("arbitrary", "arbitrary"),
            vmem_limit_bytes=VMEM_LIMIT),
        name="in_proj",
    )(x, g, w_in, conv_w, pool_bd, pool_scale, *ffn_weights)


NEG_BIG = -1e30
ALIBI_PARTS = 3
AUX_ALIBI_HI = 0
AUX_ALIBI_LO = ALIBI_PARTS
AUX_BLOCK0 = 8
VT_ROWS = HEAD_DIM + 16
EXP_LAG = 4
PV_LAG = 8
PAIRS_PER_STEP = 2
POS_SPLIT = 16


def _block_mask_t(g, q_blk):
    nb = g.shape[0]
    blk_idx = lax.broadcasted_iota(jnp.int32, g.shape, 0)
    elig = blk_idx < q_blk
    ranks = []
    for jb in range(nb):
        gj = g[jb:jb + 1, :]
        beats = ((g > gj) | ((g == gj) & (blk_idx < jb))) & elig
        ranks.append(jnp.sum(beats.astype(F32), axis=0, keepdims=True))
    rank = jnp.concatenate(ranks, axis=0)
    attend = ((rank < MOBA_TOPK) & elig) | (blk_idx == q_blk)
    return jnp.where(attend, F32(0.0), F32(NEG_BIG))


def _attn_kernel(q_ref, k_ref, v_ref, qx_ref, kx_ref, o_ref, qa_ref, ka_ref, vt_ref):
    S = k_ref.shape[1]
    blk = MOBA_BLOCK
    nb = S // blk
    n_heads = 2 * PAIRS_PER_STEP
    lane = lax.broadcasted_iota(jnp.int32, (1, LANES), 1)
    nt = (((1,), (1,)), ((), ()))
    in_heads = [(lane >= h * HEAD_DIM) & (lane < (h + 1) * HEAD_DIM) for h in range(2)]

    gated = min((MOBA_TOPK + 1) * blk, S)
    q_blk = (gated + lax.broadcasted_iota(jnp.int32, (nb, S - gated), 1)) // blk
    ones_tile = jnp.where(lax.broadcasted_iota(jnp.int32, (VT_ROWS - HEAD_DIM, S), 0) == 0,
                          F32(1.0), F32(0.0)).astype(BF16)
    for p in range(PAIRS_PER_STEP):
        cols = slice(p * LANES, (p + 1) * LANES)
        k = k_ref[0, :, cols]
        km = jnp.mean(k.astype(F32).reshape(nb, blk, LANES), axis=1)
        km = jnp.concatenate([km, jnp.zeros((GATE_ROWS - nb, LANES), F32)], axis=0)
        km_hi = km.astype(BF16)
        km_lo = (km - km_hi.astype(F32)).astype(BF16)
        q_free = q_ref[0, 0:gated, cols]
        q_gate = q_ref[0, gated:S, cols]
        v_t = v_ref[0, :, cols].T
        for h in range(2):
            ph = 2 * p + h
            alibi = qx_ref[p, h:h + 1, :]
            qa_ref[ph, 0:gated, :] = jnp.where(
                in_heads[h], q_free, jnp.broadcast_to(alibi.astype(BF16), q_free.shape))
            ka_ref[ph] = jnp.where(in_heads[h], k, kx_ref[h])
            vt_ref[ph] = jnp.concatenate([v_t[h * HEAD_DIM:(h + 1) * HEAD_DIM], ones_tile], axis=0)
            if gated < S:
                qh = jnp.where(in_heads[h], q_gate, jnp.zeros_like(q_gate))
                g = (lax.dot_general(km_hi, qh, nt, preferred_element_type=F32)
                     + lax.dot_general(km_lo, qh, nt, preferred_element_type=F32))[0:nb]
                aux = (1 - h) * HEAD_DIM + AUX_BLOCK0
                mask_t = jnp.concatenate([jnp.zeros((aux, S - gated), F32), _block_mask_t(g, q_blk),
                                          jnp.zeros((LANES - aux - nb, S - gated), F32)], axis=0)
                extra = mask_t.T + alibi
                qa_ref[ph, gated:S, :] = jnp.where(in_heads[h], q_gate, extra.astype(BF16))

    key = lax.broadcasted_iota(jnp.int32, (blk, blk), 0)
    qry = lax.broadcasted_iota(jnp.int32, (blk, blk), 1)
    causal = key <= qry

    def scores(c, h, j):
        s = lax.dot_general(ka_ref[h, j * blk:(j + 1) * blk, :], qa_ref[h, c * blk:(c + 1) * blk, :],
                            nt, preferred_element_type=F32)
        if j == c:
            s = jnp.where(causal, s, F32(NEG_BIG))
        return s, jnp.max(s, axis=0, keepdims=True)

    def probs(s, cm, m_old):
        m_new = cm if m_old is None else jnp.maximum(m_old, cm)
        alpha = None if m_old is None else jnp.exp2(m_old - m_new)
        return jnp.exp2(s - m_new).astype(BF16), m_new, alpha

    def accumulate(h, j, p, alpha, acc):
        pv = jnp.dot(vt_ref[h, :, j * blk:(j + 1) * blk], p, preferred_element_type=F32)
        return pv if acc is None else acc * alpha + pv

    steps = [(c, h, c) for c in range(nb) for h in range(n_heads)]
    steps += [(c, h, j) for j in range(nb) for c in range(j + 1, nb) for h in range(n_heads)]
    last_block = {c: (c - 1 if c else 0) for c in range(nb)}
    m_run, acc_run, outs = {}, {}, {}
    staged_s, staged_p = {}, {}
    for t in range(len(steps) + PV_LAG):
        if t < len(steps):
            staged_s[t] = scores(*steps[t])
        if 0 <= t - EXP_LAG < len(steps):
            c, h, j = steps[t - EXP_LAG]
            sc, cm = staged_s.pop(t - EXP_LAG)
            p, m_run[c, h], alpha = probs(sc, cm, m_run.get((c, h)))
            staged_p[t - EXP_LAG] = (p, alpha)
        if 0 <= t - PV_LAG < len(steps):
            c, h, j = steps[t - PV_LAG]
            p, alpha = staged_p.pop(t - PV_LAG)
            acc_run[c, h] = accumulate(h, j, p, alpha, acc_run.get((c, h)))
            if j == last_block[c]:
                acc = acc_run.pop((c, h))
                outs[c, h] = acc[0:HEAD_DIM] * (1.0 / acc[HEAD_DIM:HEAD_DIM + 1])
                pair = h // 2
                if (c, 2 * pair) in outs and (c, 2 * pair + 1) in outs:
                    o_ref[0, c * blk:(c + 1) * blk, pair * LANES:(pair + 1) * LANES] = jnp.concatenate(
                        [outs.pop((c, 2 * pair)), outs.pop((c, 2 * pair + 1))], axis=0).T.astype(BF16)


def _moba_attn(qkv, qx, kx):
    B, S, _ = qkv.shape
    width = PAIRS_PER_STEP * LANES
    n_groups = HEAD_PAIRS // PAIRS_PER_STEP
    return pl.pallas_call(
        _attn_kernel,
        out_shape=jax.ShapeDtypeStruct((B, S, ATTN_WIDTH), BF16),
        grid=(B, n_groups),
        in_specs=[
            pl.BlockSpec((1, S, width), lambda b, g: (b, 0, g)),
            pl.BlockSpec((1, S, width), lambda b, g: (b, 0, n_groups + g)),
            pl.BlockSpec((1, S, width), lambda b, g: (b, 0, 2 * n_groups + g)),
            pl.BlockSpec((PAIRS_PER_STEP, 2, LANES), lambda b, g: (g, 0, 0)),
            pl.BlockSpec((2, S, LANES), lambda b, g: (0, 0, 0)),
        ],
        out_specs=pl.BlockSpec((1, S, width), lambda b, g: (b, 0, g)),
        scratch_shapes=[
            pltpu.VMEM((2 * PAIRS_PER_STEP, S, LANES), BF16),
            pltpu.VMEM((2 * PAIRS_PER_STEP, S, LANES), BF16),
            pltpu.VMEM((2 * PAIRS_PER_STEP, VT_ROWS, S), BF16),
        ],
        compiler_params=pltpu.CompilerParams(
            dimension_semantics=("arbitrary", "arbitrary"),
            vmem_limit_bytes=VMEM_LIMIT),
        name="moba_attn",
    )(qkv, qkv, qkv, qx, kx)


def _attn_aux(S):
    nb = S // MOBA_BLOCK
    assert S // POS_SPLIT <= 128 and 2 * ALIBI_PARTS <= AUX_BLOCK0 and AUX_BLOCK0 + nb <= HEAD_DIM
    slopes = 2.0 ** (-8.0 * np.arange(1, ATTN_HEADS + 1) / ATTN_HEADS)
    rest = (slopes * LOG2E).astype(np.float32)
    pos = np.arange(S)
    qx = np.zeros((ATTN_HEADS, LANES), np.float32)
    kx = np.zeros((2, S, LANES), np.float32)
    for i in range(ALIBI_PARTS):
        part = rest.astype(BF16).astype(np.float32)
        rest = rest - part
        for h in range(2):
            aux = (1 - h) * HEAD_DIM
            qx[h::2, aux + AUX_ALIBI_HI + i] = part[h::2] * POS_SPLIT
            qx[h::2, aux + AUX_ALIBI_LO + i] = part[h::2]
            kx[h, :, aux + AUX_ALIBI_HI + i] = pos // POS_SPLIT
            kx[h, :, aux + AUX_ALIBI_LO + i] = pos % POS_SPLIT
    for h in range(2):
        aux = (1 - h) * HEAD_DIM
        kx[h, pos, aux + AUX_BLOCK0 + pos // MOBA_BLOCK] = 1.0
    return jnp.asarray(qx.reshape(HEAD_PAIRS, 2, LANES)), jnp.asarray(kx, dtype=BF16)


def _out_ffn_kernel(layer, x_ref, a_ref, cp_ref, wo_ref, gpm_ref, gpf_ref, gqf_ref,
                    wg_ref, wu_ref, wd_ref, o_ref):
    g_post_mix, g_pre_ffn, g_post_ffn = (g[layer:layer + 1, :] for g in (gpm_ref, gpf_ref, gqf_ref))
    groups = [slice(r, r + FFN_ROWS) for r in range(0, x_ref.shape[0], FFN_ROWS)]
    n_chunks = D_FF // FFN_CHUNK

    def mix(r):
        return (jnp.dot(a_ref[r, :], wo_ref[0:ATTN_WIDTH, :], preferred_element_type=F32)
                + jnp.dot(cp_ref[r, :], wo_ref[ATTN_WIDTH:D_MODEL, :], preferred_element_type=F32))

    def norms(r, mixed):
        x1 = x_ref[r, :] + _rms(mixed, g_post_mix)
        return x1, _rms(x1, g_pre_ffn).astype(BF16)

    def gate_up(hf, c):
        cols = slice(c * FFN_CHUNK, (c + 1) * FFN_CHUNK)
        return (jnp.dot(hf, wg_ref[:, cols], preferred_element_type=F32),
                jnp.dot(hf, wu_ref[:, cols], preferred_element_type=F32))

    def down(gu, c):
        gate, up = gu
        act = (gate * jax.nn.sigmoid(gate) * up).astype(BF16)
        return jnp.dot(act, wd_ref[c * FFN_CHUNK:(c + 1) * FFN_CHUNK, :],
                       preferred_element_type=F32)

    def finish(r, x1, ff):
        o_ref[r, :] = x1 + _rms(ff, g_post_ffn)

    mixed = [mix(r) for r in groups]
    x1, hf, gu, ff = {}, {}, {}, {}
    for tick in range(n_chunks + len(groups) - 1):
        for i, r in enumerate(groups):
            c = tick - i
            if c == 0:
                x1[i], hf[i] = norms(r, mixed[i])
                gu[i] = gate_up(hf[i], 0)
                ff[i] = jnp.zeros((FFN_ROWS, D_MODEL), F32)
            if 0 <= c < n_chunks:
                nxt = gate_up(hf[i], c + 1) if c + 1 < n_chunks else None
                ff[i] = ff[i] + down(gu[i], c)
                gu[i] = nxt
                if c + 1 == n_chunks:
                    finish(r, x1[i], ff[i])


def _out_ffn(layer, x, attn, cp, w_out, g_post_mix, g_pre_ffn, g_post_ffn, w_gate, w_up, w_down):
    T = x.shape[0]
    tm = FFN_TM
    whole = lambda t: (0, 0)
    resident = functools.partial(pl.BlockSpec, index_map=whole, pipeline_mode=pl.Buffered(1))
    return pl.pallas_call(
        functools.partial(_out_ffn_kernel, layer),
        out_shape=jax.ShapeDtypeStruct((T, D_MODEL), F32),
        grid=(T // tm,),
        in_specs=[
            pl.BlockSpec((tm, D_MODEL), lambda t: (t, 0)),
            pl.BlockSpec((tm, ATTN_WIDTH), lambda t: (t, 0)),
            pl.BlockSpec((tm, CONV_WIDTH + POOL_WIDTH), lambda t: (t, 0)),
            resident((D_MODEL, D_MODEL)),
            pl.BlockSpec(g_post_mix.shape, whole),
            pl.BlockSpec(g_pre_ffn.shape, whole),
            pl.BlockSpec(g_post_ffn.shape, whole),
            resident((D_MODEL, D_FF)),
            resident((D_MODEL, D_FF)),
            resident((D_FF, D_MODEL)),
        ],
        out_specs=pl.BlockSpec((tm, D_MODEL), lambda t: (t, 0)),
        compiler_params=pltpu.CompilerParams(
            dimension_semantics=("arbitrary",),
            vmem_limit_bytes=VMEM_LIMIT),
        name="out_ffn",
    )(x, attn, cp, w_out, g_post_mix, g_pre_ffn, g_post_ffn, w_gate, w_up, w_down)


def _block_diag(pool_w):
    L, G, C, _ = pool_w.shape
    eye = jnp.eye(G, dtype=pool_w.dtype)
    return (eye[None, :, None, :, None] * pool_w[:, :, :, None, :]).reshape(L, G * C, G * C)


def kernel(x, w_in, w_out, conv_w, pool_w, pool_scale, g_pre_mix, g_post_mix, g_pre_ffn,
           g_post_ffn, w_gate, w_up, w_down):
    B, S, D = x.shape
    depth = w_in.shape[0]
    assert D == D_MODEL and S % MOBA_BLOCK == 0 and S % IN_TM == 0 and (B * S) % FFN_TM == 0
    qx, kx = _attn_aux(S)
    pool_bd = _block_diag(pool_w).astype(BF16)
    for l in range(depth):
        qkv, cp, wo, wg, wu, wd = _in_proj(l, x, g_pre_mix, w_in, conv_w, pool_bd, pool_scale,
                                           (w_out, w_gate, w_up, w_down))
        attn = _moba_attn(qkv, qx, kx)
        x = _out_ffn(l, x.reshape(B * S, D), attn.reshape(B * S, ATTN_WIDTH),
                     cp.reshape(B * S, CONV_WIDTH + POOL_WIDTH), wo,
                     g_post_mix, g_pre_ffn, g_post_ffn, wg, wu, wd).reshape(B, S, D)
    return x
```

```python
import functools

import jax
import jax.numpy as jnp
import numpy as np
from jax import lax
from jax.experimental import pallas as pl
from jax.experimental.pallas import tpu as pltpu

D_MODEL = 1024
HEAD_DIM = 64
ATTN_WIDTH = 512
ATTN_HEADS = 8
CONV_WIDTH = 256
CONV_K = 3
POOL_WIDTH = 256
POOL_WINDOWS = (2, 4, 8, 16)
POOL_GROUP_DIM = 64
IN_WIDTH = 2560
MOBA_BLOCK = 256
MOBA_TOPK = 3
D_FF = 2816
NORM_EPS = 1e-6

LANES = 128
HEAD_PAIRS = ATTN_WIDTH // LANES
HALO = 16
GATE_ROWS = 16
QKV_WIDTH = 3 * ATTN_WIDTH
CONV_OFF = QKV_WIDTH
POOL_OFF = CONV_OFF + 3 * CONV_WIDTH

IN_TM = 1024
IN_ROWS = 256
N_FFN_WEIGHTS = 4
FFN_TM = 1024
FFN_ROWS = 256
FFN_CHUNK = 256
VMEM_LIMIT = 56 * 1024 * 1024

BF16 = jnp.bfloat16
F32 = jnp.float32

LOG2E = float(np.log2(np.e))
Q_SCALE = HEAD_DIM ** -0.5 * LOG2E


def _rms(x, g):
    return x * lax.rsqrt(jnp.mean(x * x, axis=-1, keepdims=True) + NORM_EPS) * g


def _in_proj_kernel(layer, x_ref, g_ref, w32_ref, cw_ref, pw_ref, ps_ref, *rest):
    nw = N_FFN_WEIGHTS
    ffn32_refs, (qkv_ref, cp_ref), ffn16_refs = rest[:nw], rest[nw:nw + 2], rest[nw + 2:2 * nw + 2]
    w_ref, cbuf, pa, pb, pc, pd = rest[2 * nw + 2:]
    j = pl.program_id(1)
    tm = IN_ROWS
    gain = g_ref[layer:layer + 1, :]
    pool_scale = ps_ref[layer:layer + 1, :]

    @pl.when((pl.program_id(0) == 0) & (j == 0))
    def _():
        for c in range(0, IN_WIDTH, 2 * LANES):
            w_ref[:, c:c + 2 * LANES] = w32_ref[:, c:c + 2 * LANES].astype(BF16)

    @pl.when(j == 0)
    def _():
        cbuf[0:HALO, :] = jnp.zeros((HALO, CONV_WIDTH), F32)
        pa[0:2 * HALO, :] = jnp.zeros((2 * HALO, POOL_WIDTH), F32)
        pb[0:HALO, :] = jnp.zeros((HALO, POOL_WIDTH), F32)
        pc[0:HALO, :] = jnp.zeros((HALO, POOL_WIDTH), F32)
        pd[0:HALO, :] = jnp.zeros((HALO, POOL_WIDTH), F32)

    lane = lax.broadcasted_iota(jnp.int32, (1, POOL_WIDTH), 1)
    g0 = lane < POOL_GROUP_DIM
    g1 = lane < 2 * POOL_GROUP_DIM
    g2 = lane < 3 * POOL_GROUP_DIM
    w2, w4, w8, w16 = (F32(w) for w in POOL_WINDOWS)
    win = jnp.where(g0, w2, jnp.where(g1, w4, jnp.where(g2, w8, w16)))

    def normed(r):
        return _rms(x_ref[0, r, :], gain).astype(BF16)

    groups = [slice(r, r + tm) for r in range(0, x_ref.shape[1], tm)]
    h = normed(groups[0])
    for i, r in enumerate(groups):
        cv = jnp.dot(h, w_ref[:, CONV_OFF:POOL_OFF], preferred_element_type=F32)
        up = jnp.dot(h, w_ref[:, POOL_OFF:IN_WIDTH], preferred_element_type=F32)

        q = jnp.dot(h, w_ref[:, 0:ATTN_WIDTH], preferred_element_type=F32) * F32(Q_SCALE)
        qkv_ref[0, r, 0:ATTN_WIDTH] = q.astype(BF16)
        h_next = normed(groups[i + 1]) if i + 1 < len(groups) else None
        if i == 0:
            for src, dst in zip(ffn32_refs, ffn16_refs):
                dst[...] = src[...].astype(BF16)

        h_conv = cv[:, 0:CONV_WIDTH]
        b_gate = cv[:, CONV_WIDTH:2 * CONV_WIDTH]
        c_gate = cv[:, 2 * CONV_WIDTH:3 * CONV_WIDTH]
        u = c_gate * h_conv
        cbuf[HALO:HALO + tm, :] = u
        conv = (cbuf[HALO - 2:HALO - 2 + tm, :] * cw_ref[0:1, :]
                + cbuf[HALO - 1:HALO - 1 + tm, :] * cw_ref[1:2, :]
                + u * cw_ref[2:3, :])
        cp_ref[0, r, 0:CONV_WIDTH] = (b_gate * conv).astype(BF16)
        cbuf[0:HALO, :] = cbuf[tm:tm + HALO, :]

        qkv_ref[0, r, ATTN_WIDTH:QKV_WIDTH] = jnp.dot(
            h, w_ref[:, ATTN_WIDTH:QKV_WIDTH], preferred_element_type=F32).astype(BF16)

        pa[2 * HALO:2 * HALO + tm, :] = up
        n = tm + HALO
        pb[HALO:HALO + n, :] = pa[HALO:HALO + n, :] + pa[HALO - 1:HALO - 1 + n, :]
        pc[HALO:HALO + n, :] = pb[HALO:HALO + n, :] + pb[HALO - 2:HALO - 2 + n, :]
        pd[HALO:HALO + n, :] = pc[HALO:HALO + n, :] + pc[HALO - 4:HALO - 4 + n, :]
        s2 = pb[2 * HALO:2 * HALO + tm, :]
        s4 = pc[2 * HALO:2 * HALO + tm, :]
        s8 = pd[2 * HALO:2 * HALO + tm, :]
        s16 = s8 + pd[2 * HALO - 8:2 * HALO - 8 + tm, :]
        wsum = jnp.where(g0, s2, jnp.where(g1, s4, jnp.where(g2, s8, s16)))
        t1 = (j * x_ref.shape[1] + r.start + 1
              + lax.broadcasted_iota(jnp.int32, (tm, 1), 0)).astype(F32)
        cnt = jnp.minimum(t1, win)
        pooled = (wsum / cnt - up).astype(BF16)
        y = jnp.dot(pooled, pw_ref[...], preferred_element_type=F32) * pool_scale
        cp_ref[0, r, CONV_WIDTH:CONV_WIDTH + POOL_WIDTH] = y.astype(BF16)
        pa[HALO:2 * HALO, :] = pa[tm + HALO:tm + 2 * HALO, :]
        h = h_next


def _in_proj(layer, x, g, w_in, conv_w, pool_bd, pool_scale, ffn_weights):
    B, S, _ = x.shape
    tm = IN_TM
    n_steps = B * (S // tm)
    assert len(ffn_weights) == N_FFN_WEIGHTS
    assert all(w.shape[1] % (16 * n_steps) == 0 for w in ffn_weights)
    slice_spec = lambda w, idx: pl.BlockSpec((None,) * (w.ndim - 2) + (w.shape[-2] // n_steps, w.shape[-1]), idx)
    step = lambda b, j: b * (S // tm) + j
    const = lambda b, j: (layer, 0, 0)
    whole = lambda b, j: (0, 0)
    return pl.pallas_call(
        functools.partial(_in_proj_kernel, layer),
        out_shape=(jax.ShapeDtypeStruct((B, S, QKV_WIDTH), BF16),
                   jax.ShapeDtypeStruct((B, S, CONV_WIDTH + POOL_WIDTH), BF16),
                   *(jax.ShapeDtypeStruct(w.shape[1:], BF16) for w in ffn_weights)),
        grid=(B, S // tm),
        in_specs=[
            pl.BlockSpec((1, tm, D_MODEL), lambda b, j: (b, j, 0)),
            pl.BlockSpec(g.shape, whole),
            pl.BlockSpec((None, D_MODEL, IN_WIDTH), const, pipeline_mode=pl.Buffered(1)),
            pl.BlockSpec((None, CONV_K, CONV_WIDTH), const),
            pl.BlockSpec((None, POOL_WIDTH, POOL_WIDTH), const),
            pl.BlockSpec(pool_scale.shape, whole),
            *(slice_spec(w, lambda b, j: (layer, step(b, j), 0)) for w in ffn_weights),
        ],
        out_specs=(pl.BlockSpec((1, tm, QKV_WIDTH), lambda b, j: (b, j, 0)),
                   pl.BlockSpec((1, tm, CONV_WIDTH + POOL_WIDTH), lambda b, j: (b, j, 0)),
                   *(slice_spec(w[0], lambda b, j: (step(b, j), 0)) for w in ffn_weights)),
        scratch_shapes=[
            pltpu.VMEM((D_MODEL, IN_WIDTH), BF16),
            pltpu.VMEM((HALO + IN_ROWS, CONV_WIDTH), F32),
            pltpu.VMEM((2 * HALO + IN_ROWS, POOL_WIDTH), F32),
            pltpu.VMEM((2 * HALO + IN_ROWS, POOL_WIDTH), F32),
            pltpu.VMEM((2 * HALO + IN_ROWS, POOL_WIDTH), F32),
            pltpu.VMEM((2 * HALO + IN_ROWS, POOL_WIDTH), F32),
        ],
        compiler_params=pltpu.CompilerParams(
            dimension_semantics=("arbitrary", "arbitrary"),
            vmem_limit_bytes=VMEM_LIMIT),
        name="in_proj",
    )(x, g, w_in, conv_w, pool_bd, pool_scale, *ffn_weights)


NEG_BIG = -1e30
ALIBI_PARTS = 3
AUX_ALIBI_HI = 0
AUX_ALIBI_LO = ALIBI_PARTS
AUX_BLOCK0 = 8
VT_ROWS = HEAD_DIM + 16
EXP_LAG = 4
PV_LAG = 8
PAIRS_PER_STEP = 2
POS_SPLIT = 16


def _block_mask_t(g, q_blk):
    nb = g.shape[0]
    blk_idx = lax.broadcasted_iota(jnp.int32, g.shape, 0)
    elig = blk_idx < q_blk
    ranks = []
    for jb in range(nb):
        gj = g[jb:jb + 1, :]
        beats = ((g > gj) | ((g == gj) & (blk_idx < jb))) & elig
        ranks.append(jnp.sum(beats.astype(F32), axis=0, keepdims=True))
    rank = jnp.concatenate(ranks, axis=0)
    attend = ((rank < MOBA_TOPK) & elig) | (blk_idx == q_blk)
    return jnp.where(attend, F32(0.0), F32(NEG_BIG))


def _attn_kernel(q_ref, k_ref, v_ref, qx_ref, kx_ref, o_ref, qa_ref, ka_ref, vt_ref):
    S = k_ref.shape[1]
    blk = MOBA_BLOCK
    nb = S // blk
    n_heads = 2 * PAIRS_PER_STEP
    lane = lax.broadcasted_iota(jnp.int32, (1, LANES), 1)
    nt = (((1,), (1,)), ((), ()))
    in_heads = [(lane >= h * HEAD_DIM) & (lane < (h + 1) * HEAD_DIM) for h in range(2)]

    gated = min((MOBA_TOPK + 1) * blk, S)
    q_blk = (gated + lax.broadcasted_iota(jnp.int32, (nb, S - gated), 1)) // blk
    ones_tile = jnp.where(lax.broadcasted_iota(jnp.int32, (VT_ROWS - HEAD_DIM, S), 0) == 0,
                          F32(1.0), F32(0.0)).astype(BF16)
    for p in range(PAIRS_PER_STEP):
        cols = slice(p * LANES, (p + 1) * LANES)
        k = k_ref[0, :, cols]
        km = jnp.mean(k.astype(F32).reshape(nb, blk, LANES), axis=1)
        km = jnp.concatenate([km, jnp.zeros((GATE_ROWS - nb, LANES), F32)], axis=0)
        km_hi = km.astype(BF16)
        km_lo = (km - km_hi.astype(F32)).astype(BF16)
        q_free = q_ref[0, 0:gated, cols]
        q_gate = q_ref[0, gated:S, cols]
        v_t = v_ref[0, :, cols].T
        for h in range(2):
            ph = 2 * p + h
            alibi = qx_ref[p, h:h + 1, :]
            qa_ref[ph, 0:gated, :] = jnp.where(
                in_heads[h], q_free, jnp.broadcast_to(alibi.astype(BF16), q_free.shape))
            ka_ref[ph] = jnp.where(in_heads[h], k, kx_ref[h])
            vt_ref[ph] = jnp.concatenate([v_t[h * HEAD_DIM:(h + 1) * HEAD_DIM], ones_tile], axis=0)
            if gated < S:
                qh = jnp.where(in_heads[h], q_gate, jnp.zeros_like(q_gate))
                g = (lax.dot_general(km_hi, qh, nt, preferred_element_type=F32)
                     + lax.dot_general(km_lo, qh, nt, preferred_element_type=F32))[0:nb]
                aux = (1 - h) * HEAD_DIM + AUX_BLOCK0
                mask_t = jnp.concatenate([jnp.zeros((aux, S - gated), F32), _block_mask_t(g, q_blk),
                                          jnp.zeros((LANES - aux - nb, S - gated), F32)], axis=0)
                extra = mask_t.T + alibi
                qa_ref[ph, gated:S, :] = jnp.where(in_heads[h], q_gate, extra.astype(BF16))

    key = lax.broadcasted_iota(jnp.int32, (blk, blk), 0)
    qry = lax.broadcasted_iota(jnp.int32, (blk, blk), 1)
    causal = key <= qry

    def scores(c, h, j):
        s = lax.dot_general(ka_ref[h, j * blk:(j + 1) * blk, :], qa_ref[h, c * blk:(c + 1) * blk, :],
                            nt, preferred_element_type=F32)
        if j == c:
            s = jnp.where(causal, s, F32(NEG_BIG))
        return s, jnp.max(s, axis=0, keepdims=True)

    def probs(s, cm, m_old):
        m_new = cm if m_old is None else jnp.maximum(m_old, cm)
        alpha = None if m_old is None else jnp.exp2(m_old - m_new)
        return jnp.exp2(s - m_new).astype(BF16), m_new, alpha

    def accumulate(h, j, p, alpha, acc):
        pv = jnp.dot(vt_ref[h, :, j * blk:(j + 1) * blk], p, preferred_element_type=F32)
        return pv if acc is None else acc * alpha + pv

    steps = [(c, h, c) for c in range(nb) for h in range(n_heads)]
    steps += [(c, h, j) for j in range(nb) for c in range(j + 1, nb) for h in range(n_heads)]
    last_block = {c: (c - 1 if c else 0) for c in range(nb)}
    m_run, acc_run, outs = {}, {}, {}
    staged_s, staged_p = {}, {}
    for t in range(len(steps) + PV_LAG):
        if t < len(steps):
            staged_s[t] = scores(*steps[t])
        if 0 <= t - EXP_LAG < len(steps):
            c, h, j = steps[t - EXP_LAG]
            sc, cm = staged_s.pop(t - EXP_LAG)
            p, m_run[c, h], alpha = probs(sc, cm, m_run.get((c, h)))
            staged_p[t - EXP_LAG] = (p, alpha)
        if 0 <= t - PV_LAG < len(steps):
            c, h, j = steps[t - PV_LAG]
            p, alpha = staged_p.pop(t - PV_LAG)
            acc_run[c, h] = accumulate(h, j, p, alpha, acc_run.get((c, h)))
            if j == last_block[c]:
                acc = acc_run.pop((c, h))
                outs[c, h] = acc[0:HEAD_DIM] * (1.0 / acc[HEAD_DIM:HEAD_DIM + 1])
                pair = h // 2
                if (c, 2 * pair) in outs and (c, 2 * pair + 1) in outs:
                    o_ref[0, c * blk:(c + 1) * blk, pair * LANES:(pair + 1) * LANES] = jnp.concatenate(
                        [outs.pop((c, 2 * pair)), outs.pop((c, 2 * pair + 1))], axis=0).T.astype(BF16)


def _moba_attn(qkv, qx, kx):
    B, S, _ = qkv.shape
    width = PAIRS_PER_STEP * LANES
    n_groups = HEAD_PAIRS // PAIRS_PER_STEP
    return pl.pallas_call(
        _attn_kernel,
        out_shape=jax.ShapeDtypeStruct((B, S, ATTN_WIDTH), BF16),
        grid=(B, n_groups),
        in_specs=[
            pl.BlockSpec((1, S, width), lambda b, g: (b, 0, g)),
            pl.BlockSpec((1, S, width), lambda b, g: (b, 0, n_groups + g)),
            pl.BlockSpec((1, S, width), lambda b, g: (b, 0, 2 * n_groups + g)),
            pl.BlockSpec((PAIRS_PER_STEP, 2, LANES), lambda b, g: (g, 0, 0)),
            pl.BlockSpec((2, S, LANES), lambda b, g: (0, 0, 0)),
        ],
        out_specs=pl.BlockSpec((1, S, width), lambda b, g: (b, 0, g)),
        scratch_shapes=[
            pltpu.VMEM((2 * PAIRS_PER_STEP, S, LANES), BF16),
            pltpu.VMEM((2 * PAIRS_PER_STEP, S, LANES), BF16),
            pltpu.VMEM((2 * PAIRS_PER_STEP, VT_ROWS, S), BF16),
        ],
        compiler_params=pltpu.CompilerParams(
            dimension_semantics=("arbitrary", "arbitrary"),
            vmem_limit_bytes=VMEM_LIMIT),
        name="moba_attn",
    )(qkv, qkv, qkv, qx, kx)


def _attn_aux(S):
    nb = S // MOBA_BLOCK
    assert S // POS_SPLIT <= 128 and 2 * ALIBI_PARTS <= AUX_BLOCK0 and AUX_BLOCK0 + nb <= HEAD_DIM
    slopes = 2.0 ** (-8.0 * np.arange(1, ATTN_HEADS + 1) / ATTN_HEADS)
    rest = (slopes * LOG2E).astype(np.float32)
    pos = np.arange(S)
    qx = np.zeros((ATTN_HEADS, LANES), np.float32)
    kx = np.zeros((2, S, LANES), np.float32)
    for i in range(ALIBI_PARTS):
        part = rest.astype(BF16).astype(np.float32)
        rest = rest - part
        for h in range(2):
            aux = (1 - h) * HEAD_DIM
            qx[h::2, aux + AUX_ALIBI_HI + i] = part[h::2] * POS_SPLIT
            qx[h::2, aux + AUX_ALIBI_LO + i] = part[h::2]
            kx[h, :, aux + AUX_ALIBI_HI + i] = pos // POS_SPLIT
            kx[h, :, aux + AUX_ALIBI_LO + i] = pos % POS_SPLIT
    for h in range(2):
        aux = (1 - h) * HEAD_DIM
        kx[h, pos, aux + AUX_BLOCK0 + pos // MOBA_BLOCK] = 1.0
    return jnp.asarray(qx.reshape(HEAD_PAIRS, 2, LANES)), jnp.asarray(kx, dtype=BF16)


def _out_ffn_kernel(layer, x_ref, a_ref, cp_ref, wo_ref, gpm_ref, gpf_ref, gqf_ref,
                    wg_ref, wu_ref, wd_ref, o_ref):
    g_post_mix, g_pre_ffn, g_post_ffn = (g[layer:layer + 1, :] for g in (gpm_ref, gpf_ref, gqf_ref))
    groups = [slice(r, r + FFN_ROWS) for r in range(0, x_ref.shape[0], FFN_ROWS)]
    n_chunks = D_FF // FFN_CHUNK

    def mix(r):
        return (jnp.dot(a_ref[r, :], wo_ref[0:ATTN_WIDTH, :], preferred_element_type=F32)
                + jnp.dot(cp_ref[r, :], wo_ref[ATTN_WIDTH:D_MODEL, :], preferred_element_type=F32))

    def norms(r, mixed):
        x1 = x_ref[r, :] + _rms(mixed, g_post_mix)
        return x1, _rms(x1, g_pre_ffn).astype(BF16)

    def gate_up(hf, c):
        cols = slice(c * FFN_CHUNK, (c + 1) * FFN_CHUNK)
        return (jnp.dot(hf, wg_ref[:, cols], preferred_element_type=F32),
                jnp.dot(hf, wu_ref[:, cols], preferred_element_type=F32))

    def down(gu, c):
        gate, up = gu
        act = (gate * jax.nn.sigmoid(gate) * up).astype(BF16)
        return jnp.dot(act, wd_ref[c * FFN_CHUNK:(c + 1) * FFN_CHUNK, :],
                       preferred_element_type=F32)

    def finish(r, x1, ff):
        o_ref[r, :] = x1 + _rms(ff, g_post_ffn)

    mixed = [mix(r) for r in groups]
    x1, hf, gu, ff = {}, {}, {}, {}
    for tick in range(n_chunks + len(groups) - 1):
        for i, r in enumerate(groups):
            c = tick - i
            if c == 0:
                x1[i], hf[i] = norms(r, mixed[i])
                gu[i] = gate_up(hf[i], 0)
                ff[i] = jnp.zeros((FFN_ROWS, D_MODEL), F32)
            if 0 <= c < n_chunks:
                nxt = gate_up(hf[i], c + 1) if c + 1 < n_chunks else None
                ff[i] = ff[i] + down(gu[i], c)
                gu[i] = nxt
                if c + 1 == n_chunks:
                    finish(r, x1[i], ff[i])


def _out_ffn(layer, x, attn, cp, w_out, g_post_mix, g_pre_ffn, g_post_ffn, w_gate, w_up, w_down):
    T = x.shape[0]
    tm = FFN_TM
    whole = lambda t: (0, 0)
    resident = functools.partial(pl.BlockSpec, index_map=whole, pipeline_mode=pl.Buffered(1))
    return pl.pallas_call(
        functools.partial(_out_ffn_kernel, layer),
        out_shape=jax.ShapeDtypeStruct((T, D_MODEL), F32),
        grid=(T // tm,),
        in_specs=[
            pl.BlockSpec((tm, D_MODEL), lambda t: (t, 0)),
            pl.BlockSpec((tm, ATTN_WIDTH), lambda t: (t, 0)),
            pl.BlockSpec((tm, CONV_WIDTH + POOL_WIDTH), lambda t: (t, 0)),
            resident((D_MODEL, D_MODEL)),
            pl.BlockSpec(g_post_mix.shape, whole),
            pl.BlockSpec(g_pre_ffn.shape, whole),
            pl.BlockSpec(g_post_ffn.shape, whole),
            resident((D_MODEL, D_FF)),
            resident((D_MODEL, D_FF)),
            resident((D_FF, D_MODEL)),
        ],
        out_specs=pl.BlockSpec((tm, D_MODEL), lambda t: (t, 0)),
        compiler_params=pltpu.CompilerParams(
            dimension_semantics=("arbitrary",),
            vmem_limit_bytes=VMEM_LIMIT),
        name="out_ffn",
    )(x, attn, cp, w_out, g_post_mix, g_pre_ffn, g_post_ffn, w_gate, w_up, w_down)


def _block_diag(pool_w):
    L, G, C, _ = pool_w.shape
    eye = jnp.eye(G, dtype=pool_w.dtype)
    return (eye[None, :, None, :, None] * pool_w[:, :, :, None, :]).reshape(L, G * C, G * C)


def kernel(x, w_in, w_out, conv_w, pool_w, pool_scale, g_pre_mix, g_post_mix, g_pre_ffn,
           g_post_ffn, w_gate, w_up, w_down):
    B, S, D = x.shape
    depth = w_in.shape[0]
    assert D == D_MODEL and S % MOBA_BLOCK == 0 and S % IN_TM == 0 and (B * S) % FFN_TM == 0
    qx, kx = _attn_aux(S)
    pool_bd = _block_diag(pool_w).astype(BF16)
    for l in range(depth):
        qkv, cp, wo, wg, wu, wd = _in_proj(l, x, g_pre_mix, w_in, conv_w, pool_bd, pool_scale,
                                           (w_out, w_gate, w_up, w_down))
        attn = _moba_attn(qkv, qx, kx)
        x = _out_ffn(l, x.reshape(B * S, D), attn.reshape(B * S, ATTN_WIDTH),
                     cp.reshape(B * S, CONV_WIDTH + POOL_WIDTH), wo,
                     g_post_mix, g_pre_ffn, g_post_ffn, wg, wu, wd).reshape(B, S, D)
    return x
```

```python
import functools

import jax
import jax.numpy as jnp
import numpy as np
from jax import lax
from jax.experimental import pallas as pl
from jax.experimental.pallas import tpu as pltpu

D_MODEL = 1024
HEAD_DIM = 64
ATTN_WIDTH = 512
ATTN_HEADS = 8
CONV_WIDTH = 256
CONV_K = 3
POOL_WIDTH = 256
POOL_WINDOWS = (2, 4, 8, 16)
POOL_GROUP_DIM = 64
IN_WIDTH = 2560
MOBA_BLOCK = 256
MOBA_TOPK = 3
D_FF = 2816
NORM_EPS = 1e-6

LANES = 128
HEAD_PAIRS = ATTN_WIDTH // LANES
HALO = 16
GATE_ROWS = 16
QKV_WIDTH = 3 * ATTN_WIDTH
CONV_OFF = QKV_WIDTH
POOL_OFF = CONV_OFF + 3 * CONV_WIDTH

IN_TM = 1024
IN_ROWS = 256
N_FFN_WEIGHTS = 4
FFN_TM = 1024
FFN_ROWS = 256
FFN_CHUNK = 256
VMEM_LIMIT = 56 * 1024 * 1024

BF16 = jnp.bfloat16
F32 = jnp.float32

LOG2E = float(np.log2(np.e))
Q_SCALE = HEAD_DIM ** -0.5 * LOG2E


def _rms(x, g):
    return x * lax.rsqrt(jnp.mean(x * x, axis=-1, keepdims=True) + NORM_EPS) * g


def _in_proj_kernel(layer, x_ref, g_ref, w32_ref, cw_ref, pw_ref, ps_ref, *rest):
    nw = N_FFN_WEIGHTS
    ffn32_refs, (qkv_ref, cp_ref), ffn16_refs = rest[:nw], rest[nw:nw + 2], rest[nw + 2:2 * nw + 2]
    w_ref, cbuf, pa, pb, pc, pd = rest[2 * nw + 2:]
    j = pl.program_id(1)
    tm = IN_ROWS
    gain = g_ref[layer:layer + 1, :]
    pool_scale = ps_ref[layer:layer + 1, :]

    @pl.when((pl.program_id(0) == 0) & (j == 0))
    def _():
        for c in range(0, IN_WIDTH, 2 * LANES):
            w_ref[:, c:c + 2 * LANES] = w32_ref[:, c:c + 2 * LANES].astype(BF16)

    @pl.when(j == 0)
    def _():
        cbuf[0:HALO, :] = jnp.zeros((HALO, CONV_WIDTH), F32)
        pa[0:2 * HALO, :] = jnp.zeros((2 * HALO, POOL_WIDTH), F32)
        pb[0:HALO, :] = jnp.zeros((HALO, POOL_WIDTH), F32)
        pc[0:HALO, :] = jnp.zeros((HALO, POOL_WIDTH), F32)
        pd[0:HALO, :] = jnp.zeros((HALO, POOL_WIDTH), F32)

    lane = lax.broadcasted_iota(jnp.int32, (1, POOL_WIDTH), 1)
    g0 = lane < POOL_GROUP_DIM
    g1 = lane < 2 * POOL_GROUP_DIM
    g2 = lane < 3 * POOL_GROUP_DIM
    w2, w4, w8, w16 = (F32(w) for w in POOL_WINDOWS)
    win = jnp.where(g0, w2, jnp.where(g1, w4, jnp.where(g2, w8, w16)))

    def normed(r):
        return _rms(x_ref[0, r, :], gain).astype(BF16)

    groups = [slice(r, r + tm) for r in range(0, x_ref.shape[1], tm)]
    h = normed(groups[0])
    for i, r in enumerate(groups):
        cv = jnp.dot(h, w_ref[:, CONV_OFF:POOL_OFF], preferred_element_type=F32)
        up = jnp.dot(h, w_ref[:, POOL_OFF:IN_WIDTH], preferred_element_type=F32)

        q = jnp.dot(h, w_ref[:, 0:ATTN_WIDTH], preferred_element_type=F32) * F32(Q_SCALE)
        qkv_ref[0, r, 0:ATTN_WIDTH] = q.astype(BF16)
        h_next = normed(groups[i + 1]) if i + 1 < len(groups) else None
        if i == 0:
            for src, dst in zip(ffn32_refs, ffn16_refs):
                dst[...] = src[...].astype(BF16)

        h_conv = cv[:, 0:CONV_WIDTH]
        b_gate = cv[:, CONV_WIDTH:2 * CONV_WIDTH]
        c_gate = cv[:, 2 * CONV_WIDTH:3 * CONV_WIDTH]
        u = c_gate * h_conv
        cbuf[HALO:HALO + tm, :] = u
        conv = (cbuf[HALO - 2:HALO - 2 + tm, :] * cw_ref[0:1, :]
                + cbuf[HALO - 1:HALO - 1 + tm, :] * cw_ref[1:2, :]
                + u * cw_ref[2:3, :])
        cp_ref[0, r, 0:CONV_WIDTH] = (b_gate * conv).astype(BF16)
        cbuf[0:HALO, :] = cbuf[tm:tm + HALO, :]

        qkv_ref[0, r, ATTN_WIDTH:QKV_WIDTH] = jnp.dot(
            h, w_ref[:, ATTN_WIDTH:QKV_WIDTH], preferred_element_type=F32).astype(BF16)

        pa[2 * HALO:2 * HALO + tm, :] = up
        n = tm + HALO
        pb[HALO:HALO + n, :] = pa[HALO:HALO + n, :] + pa[HALO - 1:HALO - 1 + n, :]
        pc[HALO:HALO + n, :] = pb[HALO:HALO + n, :] + pb[HALO - 2:HALO - 2 + n, :]
        pd[HALO:HALO + n, :] = pc[HALO:HALO + n, :] + pc[HALO - 4:HALO - 4 + n, :]
        s2 = pb[2 * HALO:2 * HALO + tm, :]
        s4 = pc[2 * HALO:2 * HALO + tm, :]
        s8 = pd[2 * HALO:2 * HALO + tm, :]
        s16 = s8 + pd[2 * HALO - 8:2 * HALO - 8 + tm, :]
        wsum = jnp.where(g0, s2, jnp.where(g1, s4, jnp.where(g2, s8, s16)))
        t1 = (j * x_ref.shape[1] + r.start + 1
              + lax.broadcasted_iota(jnp.int32, (tm, 1), 0)).astype(F32)
        cnt = jnp.minimum(t1, win)
        pooled = (wsum / cnt - up).astype(BF16)
        y = jnp.dot(pooled, pw_ref[...], preferred_element_type=F32) * pool_scale
        cp_ref[0, r, CONV_WIDTH:CONV_WIDTH + POOL_WIDTH] = y.astype(BF16)
        pa[HALO:2 * HALO, :] = pa[tm + HALO:tm + 2 * HALO, :]
        h = h_next


def _in_proj(layer, x, g, w_in, conv_w, pool_bd, pool_scale, ffn_weights):
    B, S, _ = x.shape
    tm = IN_TM
    n_steps = B * (S // tm)
    assert len(ffn_weights) == N_FFN_WEIGHTS
    assert all(w.shape[1] % (16 * n_steps) == 0 for w in ffn_weights)
    slice_spec = lambda w, idx: pl.BlockSpec((None,) * (w.ndim - 2) + (w.shape[-2] // n_steps, w.shape[-1]), idx)
    step = lambda b, j: b * (S // tm) + j
    const = lambda b, j: (layer, 0, 0)
    whole = lambda b, j: (0, 0)
    return pl.pallas_call(
        functools.partial(_in_proj_kernel, layer),
        out_shape=(jax.ShapeDtypeStruct((B, S, QKV_WIDTH), BF16),
                   jax.ShapeDtypeStruct((B, S, CONV_WIDTH + POOL_WIDTH), BF16),
                   *(jax.ShapeDtypeStruct(w.shape[1:], BF16) for w in ffn_weights)),
        grid=(B, S // tm),
        in_specs=[
            pl.BlockSpec((1, tm, D_MODEL), lambda b, j: (b, j, 0)),
            pl.BlockSpec(g.shape, whole),
            pl.BlockSpec((None, D_MODEL, IN_WIDTH), const, pipeline_mode=pl.Buffered(1)),
            pl.BlockSpec((None, CONV_K, CONV_WIDTH), const),
            pl.BlockSpec((None, POOL_WIDTH, POOL_WIDTH), const),
            pl.BlockSpec(pool_scale.shape, whole),
            *(slice_spec(w, lambda b, j: (layer, step(b, j), 0)) for w in ffn_weights),
        ],
        out_specs=(pl.BlockSpec((1, tm, QKV_WIDTH), lambda b, j: (b, j, 0)),
                   pl.BlockSpec((1, tm, CONV_WIDTH + POOL_WIDTH), lambda b, j: (b, j, 0)),
                   *(slice_spec(w[0], lambda b, j: (step(b, j), 0)) for w in ffn_weights)),
        scratch_shapes=[
            pltpu.VMEM((D_MODEL, IN_WIDTH), BF16),
            pltpu.VMEM((HALO + IN_ROWS, CONV_WIDTH), F32),
            pltpu.VMEM((2 * HALO + IN_ROWS, POOL_WIDTH), F32),
            pltpu.VMEM((2 * HALO + IN_ROWS, POOL_WIDTH), F32),
            pltpu.VMEM((2 * HALO + IN_ROWS, POOL_WIDTH), F32),
            pltpu.VMEM((2 * HALO + IN_ROWS, POOL_WIDTH), F32),
        ],
        compiler_params=pltpu.CompilerParams(
            dimension_semantics=("arbitrary", "arbitrary"),
            vmem_limit_bytes=VMEM_LIMIT),
        name="in_proj",
    )(x, g, w_in, conv_w, pool_bd, pool_scale, *ffn_weights)


NEG_BIG = -1e30
ALIBI_PARTS = 3
AUX_ALIBI_HI = 0
AUX_ALIBI_LO = ALIBI_PARTS
AUX_BLOCK0 = 8
VT_ROWS = HEAD_DIM + 16
EXP_LAG = 3
PV_LAG = 6
PAIRS_PER_STEP = 2
POS_SPLIT = 16


def _block_mask_t(g, q_blk):
    nb = g.shape[0]
    blk_idx = lax.broadcasted_iota(jnp.int32, g.shape, 0)
    elig = blk_idx < q_blk
    ranks = []
    for jb in range(nb):
        gj = g[jb:jb + 1, :]
        beats = ((g > gj) | ((g == gj) & (blk_idx < jb))) & elig
        ranks.append(jnp.sum(beats.astype(F32), axis=0, keepdims=True))
    rank = jnp.concatenate(ranks, axis=0)
    attend = ((rank < MOBA_TOPK) & elig) | (blk_idx == q_blk)
    return jnp.where(attend, F32(0.0), F32(NEG_BIG))


def _attn_kernel(q_ref, k_ref, v_ref, qx_ref, kx_ref, o_ref, qa_ref, ka_ref, vt_ref):
    S = k_ref.shape[1]
    blk = MOBA_BLOCK
    nb = S // blk
    n_heads = 2 * PAIRS_PER_STEP
    lane = lax.broadcasted_iota(jnp.int32, (1, LANES), 1)
    nt = (((1,), (1,)), ((), ()))
    in_heads = [(lane >= h * HEAD_DIM) & (lane < (h + 1) * HEAD_DIM) for h in range(2)]

    gated = min((MOBA_TOPK + 1) * blk, S)
    q_blk = (gated + lax.broadcasted_iota(jnp.int32, (nb, S - gated), 1)) // blk
    ones_tile = jnp.where(lax.broadcasted_iota(jnp.int32, (VT_ROWS - HEAD_DIM, S), 0) == 0,
                          F32(1.0), F32(0.0)).astype(BF16)
    for p in range(PAIRS_PER_STEP):
        cols = slice(p * LANES, (p + 1) * LANES)
        k = k_ref[0, :, cols]
        km = jnp.mean(k.astype(F32).reshape(nb, blk, LANES), axis=1)
        km = jnp.concatenate([km, jnp.zeros((GATE_ROWS - nb, LANES), F32)], axis=0)
        km_hi = km.astype(BF16)
        km_lo = (km - km_hi.astype(F32)).astype(BF16)
        q_free = q_ref[0, 0:gated, cols]
        q_gate = q_ref[0, gated:S, cols]
        v_t = v_ref[0, :, cols].T
        for h in range(2):
            ph = 2 * p + h
            alibi = qx_ref[p, h:h + 1, :]
            qa_ref[ph, 0:gated, :] = jnp.where(
                in_heads[h], q_free, jnp.broadcast_to(alibi.astype(BF16), q_free.shape))
            ka_ref[ph] = jnp.where(in_heads[h], k, kx_ref[h])
            vt_ref[ph] = jnp.concatenate([v_t[h * HEAD_DIM:(h + 1) * HEAD_DIM], ones_tile], axis=0)
            if gated < S:
                qh = jnp.where(in_heads[h], q_gate, jnp.zeros_like(q_gate))
                g = (lax.dot_general(km_hi, qh, nt, preferred_element_type=F32)
                     + lax.dot_general(km_lo, qh, nt, preferred_element_type=F32))[0:nb]
                aux = (1 - h) * HEAD_DIM + AUX_BLOCK0
                mask_t = jnp.concatenate([jnp.zeros((aux, S - gated), F32), _block_mask_t(g, q_blk),
                                          jnp.zeros((LANES - aux - nb, S - gated), F32)], axis=0)
                extra = mask_t.T + alibi
                qa_ref[ph, gated:S, :] = jnp.where(in_heads[h], q_gate, extra.astype(BF16))

    key = lax.broadcasted_iota(jnp.int32, (blk, blk), 0)
    qry = lax.broadcasted_iota(jnp.int32, (blk, blk), 1)
    causal = key <= qry

    def scores(c, h, j):
        s = lax.dot_general(ka_ref[h, j * blk:(j + 1) * blk, :], qa_ref[h, c * blk:(c + 1) * blk, :],
                            nt, preferred_element_type=F32)
        if j == c:
            s = jnp.where(causal, s, F32(NEG_BIG))
        return s, jnp.max(s, axis=0, keepdims=True)

    def probs(s, cm, m_old):
        m_new = cm if m_old is None else jnp.maximum(m_old, cm)
        alpha = None if m_old is None else jnp.exp2(m_old - m_new)
        return jnp.exp2(s - m_new).astype(BF16), m_new, alpha

    def accumulate(h, j, p, alpha, acc):
        pv = jnp.dot(vt_ref[h, :, j * blk:(j + 1) * blk], p, preferred_element_type=F32)
        return pv if acc is None else acc * alpha + pv

    steps = [(c, h, c) for c in range(nb) for h in range(n_heads)]
    steps += [(c, h, j) for j in range(nb) for c in range(j + 1, nb) for h in range(n_heads)]
    last_block = {c: (c - 1 if c else 0) for c in range(nb)}
    m_run, acc_run, outs = {}, {}, {}
    staged_s, staged_p = {}, {}
    for t in range(len(steps) + PV_LAG):
        if t < len(steps):
            staged_s[t] = scores(*steps[t])
        if 0 <= t - EXP_LAG < len(steps):
            c, h, j = steps[t - EXP_LAG]
            sc, cm = staged_s.pop(t - EXP_LAG)
            p, m_run[c, h], alpha = probs(sc, cm, m_run.get((c, h)))
            staged_p[t - EXP_LAG] = (p, alpha)
        if 0 <= t - PV_LAG < len(steps):
            c, h, j = steps[t - PV_LAG]
            p, alpha = staged_p.pop(t - PV_LAG)
            acc_run[c, h] = accumulate(h, j, p, alpha, acc_run.get((c, h)))
            if j == last_block[c]:
                acc = acc_run.pop((c, h))
                outs[c, h] = acc[0:HEAD_DIM] * (1.0 / acc[HEAD_DIM:HEAD_DIM + 1])
                pair = h // 2
                if (c, 2 * pair) in outs and (c, 2 * pair + 1) in outs:
                    o_ref[0, c * blk:(c + 1) * blk, pair * LANES:(pair + 1) * LANES] = jnp.concatenate(
                        [outs.pop((c, 2 * pair)), outs.pop((c, 2 * pair + 1))], axis=0).T.astype(BF16)


def _moba_attn(qkv, qx, kx):
    B, S, _ = qkv.shape
    width = PAIRS_PER_STEP * LANES
    n_groups = HEAD_PAIRS // PAIRS_PER_STEP
    return pl.pallas_call(
        _attn_kernel,
        out_shape=jax.ShapeDtypeStruct((B, S, ATTN_WIDTH), BF16),
        grid=(B, n_groups),
        in_specs=[
            pl.BlockSpec((1, S, width), lambda b, g: (b, 0, g)),
            pl.BlockSpec((1, S, width), lambda b, g: (b, 0, n_groups + g)),
            pl.BlockSpec((1, S, width), lambda b, g: (b, 0, 2 * n_groups + g)),
            pl.BlockSpec((PAIRS_PER_STEP, 2, LANES), lambda b, g: (g, 0, 0)),
            pl.BlockSpec((2, S, LANES), lambda b, g: (0, 0, 0)),
        ],
        out_specs=pl.BlockSpec((1, S, width), lambda b, g: (b, 0, g)),
        scratch_shapes=[
            pltpu.VMEM((2 * PAIRS_PER_STEP, S, LANES), BF16),
            pltpu.VMEM((2 * PAIRS_PER_STEP, S, LANES), BF16),
            pltpu.VMEM((2 * PAIRS_PER_STEP, VT_ROWS, S), BF16),
        ],
        compiler_params=pltpu.CompilerParams(
            dimension_semantics=("arbitrary", "arbitrary"),
            vmem_limit_bytes=VMEM_LIMIT),
        name="moba_attn",
    )(qkv, qkv, qkv, qx, kx)


def _attn_aux(S):
    nb = S // MOBA_BLOCK
    assert S // POS_SPLIT <= 128 and 2 * ALIBI_PARTS <= AUX_BLOCK0 and AUX_BLOCK0 + nb <= HEAD_DIM
    slopes = 2.0 ** (-8.0 * np.arange(1, ATTN_HEADS + 1) / ATTN_HEADS)
    rest = (slopes * LOG2E).astype(np.float32)
    pos = np.arange(S)
    qx = np.zeros((ATTN_HEADS, LANES), np.float32)
    kx = np.zeros((2, S, LANES), np.float32)
    for i in range(ALIBI_PARTS):
        part = rest.astype(BF16).astype(np.float32)
        rest = rest - part
        for h in range(2):
            aux = (1 - h) * HEAD_DIM
            qx[h::2, aux + AUX_ALIBI_HI + i] = part[h::2] * POS_SPLIT
            qx[h::2, aux + AUX_ALIBI_LO + i] = part[h::2]
            kx[h, :, aux + AUX_ALIBI_HI + i] = pos // POS_SPLIT
            kx[h, :, aux + AUX_ALIBI_LO + i] = pos % POS_SPLIT
    for h in range(2):
        aux = (1 - h) * HEAD_DIM
        kx[h, pos, aux + AUX_BLOCK0 + pos // MOBA_BLOCK] = 1.0
    return jnp.asarray(qx.reshape(HEAD_PAIRS, 2, LANES)), jnp.asarray(kx, dtype=BF16)


def _out_ffn_kernel(layer, x_ref, a_ref, cp_ref, wo_ref, gpm_ref, gpf_ref, gqf_ref,
                    wg_ref, wu_ref, wd_ref, o_ref):
    g_post_mix, g_pre_ffn, g_post_ffn = (g[layer:layer + 1, :] for g in (gpm_ref, gpf_ref, gqf_ref))
    groups = [slice(r, r + FFN_ROWS) for r in range(0, x_ref.shape[0], FFN_ROWS)]
    n_chunks = D_FF // FFN_CHUNK

    def mix(r):
        return (jnp.dot(a_ref[r, :], wo_ref[0:ATTN_WIDTH, :], preferred_element_type=F32)
                + jnp.dot(cp_ref[r, :], wo_ref[ATTN_WIDTH:D_MODEL, :], preferred_element_type=F32))

    def norms(r, mixed):
        x1 = x_ref[r, :] + _rms(mixed, g_post_mix)
        return x1, _rms(x1, g_pre_ffn).astype(BF16)

    def gate_up(hf, c):
        cols = slice(c * FFN_CHUNK, (c + 1) * FFN_CHUNK)
        return (jnp.dot(hf, wg_ref[:, cols], preferred_element_type=F32),
                jnp.dot(hf, wu_ref[:, cols], preferred_element_type=F32))

    def down(gu, c):
        gate, up = gu
        act = (gate * jax.nn.sigmoid(gate) * up).astype(BF16)
        return jnp.dot(act, wd_ref[c * FFN_CHUNK:(c + 1) * FFN_CHUNK, :],
                       preferred_element_type=F32)

    def finish(r, x1, ff):
        o_ref[r, :] = x1 + _rms(ff, g_post_ffn)

    mixed = {0: mix(groups[0])}
    x1, hf, gu, ff = {}, {}, {}, {}
    for tick in range(n_chunks + len(groups) - 1):
        for i, r in enumerate(groups):
            c = tick - i
            if c == 0:
                if i + 1 < len(groups):
                    mixed[i + 1] = mix(groups[i + 1])
                x1[i], hf[i] = norms(r, mixed[i])
                gu[i] = gate_up(hf[i], 0)
                ff[i] = jnp.zeros((FFN_ROWS, D_MODEL), F32)
            if 0 <= c < n_chunks:
                nxt = gate_up(hf[i], c + 1) if c + 1 < n_chunks else None
                ff[i] = ff[i] + down(gu[i], c)
                gu[i] = nxt
                if c + 1 == n_chunks:
                    finish(r, x1[i], ff[i])


def _out_ffn(layer, x, attn, cp, w_out, g_post_mix, g_pre_ffn, g_post_ffn, w_gate, w_up, w_down):
    T = x.shape[0]
    tm = FFN_TM
    whole = lambda t: (0, 0)
    resident = functools.partial(pl.BlockSpec, index_map=whole, pipeline_mode=pl.Buffered(1))
    return pl.pallas_call(
        functools.partial(_out_ffn_kernel, layer),
        out_shape=jax.ShapeDtypeStruct((T, D_MODEL), F32),
        grid=(T // tm,),
        in_specs=[
            pl.BlockSpec((tm, D_MODEL), lambda t: (t, 0)),
            pl.BlockSpec((tm, ATTN_WIDTH), lambda t: (t, 0)),
            pl.BlockSpec((tm, CONV_WIDTH + POOL_WIDTH), lambda t: (t, 0)),
            resident((D_MODEL, D_MODEL)),
            pl.BlockSpec(g_post_mix.shape, whole),
            pl.BlockSpec(g_pre_ffn.shape, whole),
            pl.BlockSpec(g_post_ffn.shape, whole),
            resident((D_MODEL, D_FF)),
            resident((D_MODEL, D_FF)),
            resident((D_FF, D_MODEL)),
        ],
        out_specs=pl.BlockSpec((tm, D_MODEL), lambda t: (t, 0)),
        compiler_params=pltpu.CompilerParams(
            dimension_semantics=("arbitrary",),
            vmem_limit_bytes=VMEM_LIMIT),
        name="out_ffn",
    )(x, attn, cp, w_out, g_post_mix, g_pre_ffn, g_post_ffn, w_gate, w_up, w_down)


def _block_diag(pool_w):
    L, G, C, _ = pool_w.shape
    eye = jnp.eye(G, dtype=pool_w.dtype)
    return (eye[None, :, None, :, None] * pool_w[:, :, :, None, :]).reshape(L, G * C, G * C)


def kernel(x, w_in, w_out, conv_w, pool_w, pool_scale, g_pre_mix, g_post_mix, g_pre_ffn,
           g_post_ffn, w_gate, w_up, w_down):
    B, S, D = x.shape
    depth = w_in.shape[0]
    assert D == D_MODEL and S % MOBA_BLOCK == 0 and S % IN_TM == 0 and (B * S) % FFN_TM == 0
    qx, kx = _attn_aux(S)
    pool_bd = _block_diag(pool_w).astype(BF16)
    for l in range(depth):
        qkv, cp, wo, wg, wu, wd = _in_proj(l, x, g_pre_mix, w_in, conv_w, pool_bd, pool_scale,
                                           (w_out, w_gate, w_up, w_down))
        attn = _moba_attn(qkv, qx, kx)
        x = _out_ffn(l, x.reshape(B * S, D), attn.reshape(B * S, ATTN_WIDTH),
                     cp.reshape(B * S, CONV_WIDTH + POOL_WIDTH), wo,
                     g_post_mix, g_pre_ffn, g_post_ffn, wg, wu, wd).reshape(B, S, D)
    return x
```

```python
import functools

import jax
import jax.numpy as jnp
import numpy as np
from jax import lax
from jax.experimental import pallas as pl
from jax.experimental.pallas import tpu as pltpu

D_MODEL = 1024
HEAD_DIM = 64
ATTN_WIDTH = 512
ATTN_HEADS = 8
CONV_WIDTH = 256
CONV_K = 3
POOL_WIDTH = 256
POOL_WINDOWS = (2, 4, 8, 16)
POOL_GROUP_DIM = 64
IN_WIDTH = 2560
MOBA_BLOCK = 256
MOBA_TOPK = 3
D_FF = 2816
NORM_EPS = 1e-6

LANES = 128
HEAD_PAIRS = ATTN_WIDTH // LANES
HALO = 16
GATE_ROWS = 16
QKV_WIDTH = 3 * ATTN_WIDTH
CONV_OFF = QKV_WIDTH
POOL_OFF = CONV_OFF + 3 * CONV_WIDTH

IN_TM = 1024
IN_ROWS = 256
N_FFN_WEIGHTS = 4
FFN_TM = 1024
FFN_ROWS = 256
FFN_CHUNK = 256
VMEM_LIMIT = 56 * 1024 * 1024

BF16 = jnp.bfloat16
F32 = jnp.float32

LOG2E = float(np.log2(np.e))
Q_SCALE = HEAD_DIM ** -0.5 * LOG2E


def _rms(x, g):
    return x * lax.rsqrt(jnp.mean(x * x, axis=-1, keepdims=True) + NORM_EPS) * g


def _in_proj_kernel(layer, x_ref, g_ref, w32_ref, cw_ref, pw_ref, ps_ref, *rest):
    nw = N_FFN_WEIGHTS
    ffn32_refs, (qkv_ref, cp_ref, km_ref) = rest[:nw], rest[nw:nw + 3]
    ffn16_refs = rest[nw + 3:2 * nw + 3]
    w_ref, cbuf, pa, pb, pc, pd = rest[2 * nw + 3:]
    j = pl.program_id(1)
    tm = IN_ROWS
    gain = g_ref[layer:layer + 1, :]
    pool_scale = ps_ref[layer:layer + 1, :]

    @pl.when((pl.program_id(0) == 0) & (j == 0))
    def _():
        for c in range(0, IN_WIDTH, 2 * LANES):
            w_ref[:, c:c + 2 * LANES] = w32_ref[:, c:c + 2 * LANES].astype(BF16)

    @pl.when(j == 0)
    def _():
        cbuf[0:HALO, :] = jnp.zeros((HALO, CONV_WIDTH), F32)
        pa[0:2 * HALO, :] = jnp.zeros((2 * HALO, POOL_WIDTH), F32)
        pb[0:HALO, :] = jnp.zeros((HALO, POOL_WIDTH), F32)
        pc[0:HALO, :] = jnp.zeros((HALO, POOL_WIDTH), F32)
        pd[0:HALO, :] = jnp.zeros((HALO, POOL_WIDTH), F32)

    lane = lax.broadcasted_iota(jnp.int32, (1, POOL_WIDTH), 1)
    g0 = lane < POOL_GROUP_DIM
    g1 = lane < 2 * POOL_GROUP_DIM
    g2 = lane < 3 * POOL_GROUP_DIM
    w2, w4, w8, w16 = (F32(w) for w in POOL_WINDOWS)
    win = jnp.where(g0, w2, jnp.where(g1, w4, jnp.where(g2, w8, w16)))

    def normed(r):
        return _rms(x_ref[0, r, :], gain).astype(BF16)

    groups = [slice(r, r + tm) for r in range(0, x_ref.shape[1], tm)]
    h = normed(groups[0])
    for i, r in enumerate(groups):
        cv = jnp.dot(h, w_ref[:, CONV_OFF:POOL_OFF], preferred_element_type=F32)
        up = jnp.dot(h, w_ref[:, POOL_OFF:IN_WIDTH], preferred_element_type=F32)

        q = jnp.dot(h, w_ref[:, 0:ATTN_WIDTH], preferred_element_type=F32) * F32(Q_SCALE)
        qkv_ref[0, r, 0:ATTN_WIDTH] = q.astype(BF16)
        h_next = normed(groups[i + 1]) if i + 1 < len(groups) else None
        if i == 0:
            for src, dst in zip(ffn32_refs, ffn16_refs):
                dst[...] = src[...].astype(BF16)

        h_conv = cv[:, 0:CONV_WIDTH]
        b_gate = cv[:, CONV_WIDTH:2 * CONV_WIDTH]
        c_gate = cv[:, 2 * CONV_WIDTH:3 * CONV_WIDTH]
        u = c_gate * h_conv
        cbuf[HALO:HALO + tm, :] = u
        conv = (cbuf[HALO - 2:HALO - 2 + tm, :] * cw_ref[0:1, :]
                + cbuf[HALO - 1:HALO - 1 + tm, :] * cw_ref[1:2, :]
                + u * cw_ref[2:3, :])
        cp_ref[0, r, 0:CONV_WIDTH] = (b_gate * conv).astype(BF16)
        cbuf[0:HALO, :] = cbuf[tm:tm + HALO, :]

        kv = jnp.dot(h, w_ref[:, ATTN_WIDTH:QKV_WIDTH], preferred_element_type=F32)
        qkv_ref[0, r, ATTN_WIDTH:QKV_WIDTH] = kv.astype(BF16)
        km_ref[0, pl.ds(j * len(groups) + i, 1), :] = jnp.mean(
            kv[:, 0:ATTN_WIDTH], axis=0, keepdims=True)

        pa[2 * HALO:2 * HALO + tm, :] = up
        n = tm + HALO
        pb[HALO:HALO + n, :] = pa[HALO:HALO + n, :] + pa[HALO - 1:HALO - 1 + n, :]
        pc[HALO:HALO + n, :] = pb[HALO:HALO + n, :] + pb[HALO - 2:HALO - 2 + n, :]
        pd[HALO:HALO + n, :] = pc[HALO:HALO + n, :] + pc[HALO - 4:HALO - 4 + n, :]
        s2 = pb[2 * HALO:2 * HALO + tm, :]
        s4 = pc[2 * HALO:2 * HALO + tm, :]
        s8 = pd[2 * HALO:2 * HALO + tm, :]
        s16 = s8 + pd[2 * HALO - 8:2 * HALO - 8 + tm, :]
        wsum = jnp.where(g0, s2, jnp.where(g1, s4, jnp.where(g2, s8, s16)))
        t1 = (j * x_ref.shape[1] + r.start + 1
              + lax.broadcasted_iota(jnp.int32, (tm, 1), 0)).astype(F32)
        cnt = jnp.minimum(t1, win)
        pooled = (wsum / cnt - up).astype(BF16)
        y = jnp.dot(pooled, pw_ref[...], preferred_element_type=F32) * pool_scale
        cp_ref[0, r, CONV_WIDTH:CONV_WIDTH + POOL_WIDTH] = y.astype(BF16)
        pa[HALO:2 * HALO, :] = pa[tm + HALO:tm + 2 * HALO, :]
        h = h_next


def _in_proj(layer, x, g, w_in, conv_w, pool_bd, pool_scale, ffn_weights):
    B, S, _ = x.shape
    tm = IN_TM
    n_steps = B * (S // tm)
    assert len(ffn_weights) == N_FFN_WEIGHTS
    assert all(w.shape[1] % (16 * n_steps) == 0 for w in ffn_weights)
    slice_spec = lambda w, idx: pl.BlockSpec((None,) * (w.ndim - 2) + (w.shape[-2] // n_steps, w.shape[-1]), idx)
    step = lambda b, j: b * (S // tm) + j
    const = lambda b, j: (layer, 0, 0)
    whole = lambda b, j: (0, 0)
    return pl.pallas_call(
        functools.partial(_in_proj_kernel, layer),
        out_shape=(jax.ShapeDtypeStruct((B, S, QKV_WIDTH), BF16),
                   jax.ShapeDtypeStruct((B, S, CONV_WIDTH + POOL_WIDTH), BF16),
                   jax.ShapeDtypeStruct((B, S // IN_ROWS, ATTN_WIDTH), F32),
                   *(jax.ShapeDtypeStruct(w.shape[1:], BF16) for w in ffn_weights)),
        grid=(B, S // tm),
        in_specs=[
            pl.BlockSpec((1, tm, D_MODEL), lambda b, j: (b, j, 0)),
            pl.BlockSpec(g.shape, whole),
            pl.BlockSpec((None, D_MODEL, IN_WIDTH), const, pipeline_mode=pl.Buffered(1)),
            pl.BlockSpec((None, CONV_K, CONV_WIDTH), const),
            pl.BlockSpec((None, POOL_WIDTH, POOL_WIDTH), const),
            pl.BlockSpec(pool_scale.shape, whole),
            *(slice_spec(w, lambda b, j: (layer, step(b, j), 0)) for w in ffn_weights),
        ],
        out_specs=(pl.BlockSpec((1, tm, QKV_WIDTH), lambda b, j: (b, j, 0)),
                   pl.BlockSpec((1, tm, CONV_WIDTH + POOL_WIDTH), lambda b, j: (b, j, 0)),
                   pl.BlockSpec((1, S // IN_ROWS, ATTN_WIDTH), lambda b, j: (b, 0, 0)),
                   *(slice_spec(w[0], lambda b, j: (step(b, j), 0)) for w in ffn_weights)),
        scratch_shapes=[
            pltpu.VMEM((D_MODEL, IN_WIDTH), BF16),
            pltpu.VMEM((HALO + IN_ROWS, CONV_WIDTH), F32),
            pltpu.VMEM((2 * HALO + IN_ROWS, POOL_WIDTH), F32),
            pltpu.VMEM((2 * HALO + IN_ROWS, POOL_WIDTH), F32),
            pltpu.VMEM((2 * HALO + IN_ROWS, POOL_WIDTH), F32),
            pltpu.VMEM((2 * HALO + IN_ROWS, POOL_WIDTH), F32),
        ],
        compiler_params=pltpu.CompilerParams(
            dimension_semantics=("arbitrary", "arbitrary"),
            vmem_limit_bytes=VMEM_LIMIT),
        name="in_proj",
    )(x, g, w_in, conv_w, pool_bd, pool_scale, *ffn_weights)


NEG_BIG = -1e30
ALIBI_PARTS = 3
AUX_ALIBI_HI = 0
AUX_ALIBI_LO = ALIBI_PARTS
AUX_BLOCK0 = 8
VT_ROWS = HEAD_DIM + 16
EXP_LAG = 3
PV_LAG = 6
PAIRS_PER_STEP = 2
POS_SPLIT = 16


def _block_mask_t(g, q_blk):
    nb = g.shape[0]
    blk_idx = lax.broadcasted_iota(jnp.int32, g.shape, 0)
    elig = blk_idx < q_blk
    ranks = []
    for jb in range(nb):
        gj = g[jb:jb + 1, :]
        beats = ((g > gj) | ((g == gj) & (blk_idx < jb))) & elig
        ranks.append(jnp.sum(beats.astype(F32), axis=0, keepdims=True))
    rank = jnp.concatenate(ranks, axis=0)
    attend = ((rank < MOBA_TOPK) & elig) | (blk_idx == q_blk)
    return jnp.where(attend, F32(0.0), F32(NEG_BIG))


def _attn_kernel(q_ref, k_ref, v_ref, km_ref, qx_ref, kx_ref, o_ref, qa_ref, ka_ref, vt_ref):
    S = k_ref.shape[1]
    blk = MOBA_BLOCK
    nb = S // blk
    n_heads = 2 * PAIRS_PER_STEP
    lane = lax.broadcasted_iota(jnp.int32, (1, LANES), 1)
    nt = (((1,), (1,)), ((), ()))
    in_heads = [(lane >= h * HEAD_DIM) & (lane < (h + 1) * HEAD_DIM) for h in range(2)]

    gated = min((MOBA_TOPK + 1) * blk, S)
    q_blk = (gated + lax.broadcasted_iota(jnp.int32, (nb, S - gated), 1)) // blk
    ones_tile = jnp.where(lax.broadcasted_iota(jnp.int32, (VT_ROWS - HEAD_DIM, S), 0) == 0,
                          F32(1.0), F32(0.0)).astype(BF16)
    for p in range(PAIRS_PER_STEP):
        cols = slice(p * LANES, (p + 1) * LANES)
        k = k_ref[0, :, cols]
        km = jnp.concatenate([km_ref[0, :, cols], jnp.zeros((GATE_ROWS - nb, LANES), F32)], axis=0)
        km_hi = km.astype(BF16)
        km_lo = (km - km_hi.astype(F32)).astype(BF16)
        q_free = q_ref[0, 0:gated, cols]
        q_gate = q_ref[0, gated:S, cols]
        v_t = v_ref[0, :, cols].T
        for h in range(2):
            ph = 2 * p + h
            alibi = qx_ref[p, h:h + 1, :]
            qa_ref[ph, 0:gated, :] = jnp.where(
                in_heads[h], q_free, jnp.broadcast_to(alibi.astype(BF16), q_free.shape))
            ka_ref[ph] = jnp.where(in_heads[h], k, kx_ref[h])
            vt_ref[ph] = jnp.concatenate([v_t[h * HEAD_DIM:(h + 1) * HEAD_DIM], ones_tile], axis=0)
            if gated < S:
                qh = jnp.where(in_heads[h], q_gate, jnp.zeros_like(q_gate))
                g = (lax.dot_general(km_hi, qh, nt, preferred_element_type=F32)
                     + lax.dot_general(km_lo, qh, nt, preferred_element_type=F32))[0:nb]
                aux = (1 - h) * HEAD_DIM + AUX_BLOCK0
                mask_t = jnp.concatenate([jnp.zeros((aux, S - gated), F32), _block_mask_t(g, q_blk),
                                          jnp.zeros((LANES - aux - nb, S - gated), F32)], axis=0)
                extra = mask_t.T + alibi
                qa_ref[ph, gated:S, :] = jnp.where(in_heads[h], q_gate, extra.astype(BF16))

    key = lax.broadcasted_iota(jnp.int32, (blk, blk), 0)
    qry = lax.broadcasted_iota(jnp.int32, (blk, blk), 1)
    causal = key <= qry

    def scores(c, h, j):
        s = lax.dot_general(ka_ref[h, j * blk:(j + 1) * blk, :], qa_ref[h, c * blk:(c + 1) * blk, :],
                            nt, preferred_element_type=F32)
        if j == c:
            s = jnp.where(causal, s, F32(NEG_BIG))
        return s, jnp.max(s, axis=0, keepdims=True)

    def probs(s, cm, m_old):
        m_new = cm if m_old is None else jnp.maximum(m_old, cm)
        alpha = None if m_old is None else jnp.exp2(m_old - m_new)
        return jnp.exp2(s - m_new).astype(BF16), m_new, alpha

    def accumulate(h, j, p, alpha, acc):
        pv = jnp.dot(vt_ref[h, :, j * blk:(j + 1) * blk], p, preferred_element_type=F32)
        return pv if acc is None else acc * alpha + pv

    steps = [(c, h, c) for c in range(nb) for h in range(n_heads)]
    steps += [(c, h, j) for j in range(nb) for c in range(j + 1, nb) for h in range(n_heads)]
    last_block = {c: (c - 1 if c else 0) for c in range(nb)}
    m_run, acc_run, outs = {}, {}, {}
    staged_s, staged_p = {}, {}
    for t in range(len(steps) + PV_LAG):
        if t < len(steps):
            staged_s[t] = scores(*steps[t])
        if 0 <= t - EXP_LAG < len(steps):
            c, h, j = steps[t - EXP_LAG]
            sc, cm = staged_s.pop(t - EXP_LAG)
            p, m_run[c, h], alpha = probs(sc, cm, m_run.get((c, h)))
            staged_p[t - EXP_LAG] = (p, alpha)
        if 0 <= t - PV_LAG < len(steps):
            c, h, j = steps[t - PV_LAG]
            p, alpha = staged_p.pop(t - PV_LAG)
            acc_run[c, h] = accumulate(h, j, p, alpha, acc_run.get((c, h)))
            if j == last_block[c]:
                acc = acc_run.pop((c, h))
                outs[c, h] = acc[0:HEAD_DIM] * (1.0 / acc[HEAD_DIM:HEAD_DIM + 1])
                pair = h // 2
                if (c, 2 * pair) in outs and (c, 2 * pair + 1) in outs:
                    o_ref[0, c * blk:(c + 1) * blk, pair * LANES:(pair + 1) * LANES] = jnp.concatenate(
                        [outs.pop((c, 2 * pair)), outs.pop((c, 2 * pair + 1))], axis=0).T.astype(BF16)


def _moba_attn(qkv, km, qx, kx):
    B, S, _ = qkv.shape
    width = PAIRS_PER_STEP * LANES
    n_groups = HEAD_PAIRS // PAIRS_PER_STEP
    return pl.pallas_call(
        _attn_kernel,
        out_shape=jax.ShapeDtypeStruct((B, S, ATTN_WIDTH), BF16),
        grid=(B, n_groups),
        in_specs=[
            pl.BlockSpec((1, S, width), lambda b, g: (b, 0, g)),
            pl.BlockSpec((1, S, width), lambda b, g: (b, 0, n_groups + g)),
            pl.BlockSpec((1, S, width), lambda b, g: (b, 0, 2 * n_groups + g)),
            pl.BlockSpec((1, S // MOBA_BLOCK, width), lambda b, g: (b, 0, g)),
            pl.BlockSpec((PAIRS_PER_STEP, 2, LANES), lambda b, g: (g, 0, 0)),
            pl.BlockSpec((2, S, LANES), lambda b, g: (0, 0, 0)),
        ],
        out_specs=pl.BlockSpec((1, S, width), lambda b, g: (b, 0, g)),
        scratch_shapes=[
            pltpu.VMEM((2 * PAIRS_PER_STEP, S, LANES), BF16),
            pltpu.VMEM((2 * PAIRS_PER_STEP, S, LANES), BF16),
            pltpu.VMEM((2 * PAIRS_PER_STEP, VT_ROWS, S), BF16),
        ],
        compiler_params=pltpu.CompilerParams(
            dimension_semantics=("arbitrary", "arbitrary"),
            vmem_limit_bytes=VMEM_LIMIT),
        name="moba_attn",
    )(qkv, qkv, qkv, km, qx, kx)


def _attn_aux(S):
    nb = S // MOBA_BLOCK
    assert S // POS_SPLIT <= 128 and 2 * ALIBI_PARTS <= AUX_BLOCK0 and AUX_BLOCK0 + nb <= HEAD_DIM
    slopes = 2.0 ** (-8.0 * np.arange(1, ATTN_HEADS + 1) / ATTN_HEADS)
    rest = (slopes * LOG2E).astype(np.float32)
    pos = np.arange(S)
    qx = np.zeros((ATTN_HEADS, LANES), np.float32)
    kx = np.zeros((2, S, LANES), np.float32)
    for i in range(ALIBI_PARTS):
        part = rest.astype(BF16).astype(np.float32)
        rest = rest - part
        for h in range(2):
            aux = (1 - h) * HEAD_DIM
            qx[h::2, aux + AUX_ALIBI_HI + i] = part[h::2] * POS_SPLIT
            qx[h::2, aux + AUX_ALIBI_LO + i] = part[h::2]
            kx[h, :, aux + AUX_ALIBI_HI + i] = pos // POS_SPLIT
            kx[h, :, aux + AUX_ALIBI_LO + i] = pos % POS_SPLIT
    for h in range(2):
        aux = (1 - h) * HEAD_DIM
        kx[h, pos, aux + AUX_BLOCK0 + pos // MOBA_BLOCK] = 1.0
    return jnp.asarray(qx.reshape(HEAD_PAIRS, 2, LANES)), jnp.asarray(kx, dtype=BF16)


def _out_ffn_kernel(layer, x_ref, a_ref, cp_ref, wo_ref, gpm_ref, gpf_ref, gqf_ref,
                    wg_ref, wu_ref, wd_ref, o_ref):
    g_post_mix, g_pre_ffn, g_post_ffn = (g[layer:layer + 1, :] for g in (gpm_ref, gpf_ref, gqf_ref))
    groups = [slice(r, r + FFN_ROWS) for r in range(0, x_ref.shape[0], FFN_ROWS)]
    n_chunks = D_FF // FFN_CHUNK

    def mix(r):
        return (jnp.dot(a_ref[r, :], wo_ref[0:ATTN_WIDTH, :], preferred_element_type=F32)
                + jnp.dot(cp_ref[r, :], wo_ref[ATTN_WIDTH:D_MODEL, :], preferred_element_type=F32))

    def norms(r, mixed):
        x1 = x_ref[r, :] + _rms(mixed, g_post_mix)
        return x1, _rms(x1, g_pre_ffn).astype(BF16)

    def gate_up(hf, c):
        cols = slice(c * FFN_CHUNK, (c + 1) * FFN_CHUNK)
        return (jnp.dot(hf, wg_ref[:, cols], preferred_element_type=F32),
                jnp.dot(hf, wu_ref[:, cols], preferred_element_type=F32))

    def down(gu, c):
        gate, up = gu
        act = (gate * jax.nn.sigmoid(gate) * up).astype(BF16)
        return jnp.dot(act, wd_ref[c * FFN_CHUNK:(c + 1) * FFN_CHUNK, :],
                       preferred_element_type=F32)

    def finish(r, x1, ff):
        o_ref[r, :] = x1 + _rms(ff, g_post_ffn)

    mixed = {0: mix(groups[0])}
    x1, hf, gu, ff = {}, {}, {}, {}
    for tick in range(n_chunks + len(groups) - 1):
        for i, r in enumerate(groups):
            c = tick - i
            if c == 0:
                if i + 1 < len(groups):
                    mixed[i + 1] = mix(groups[i + 1])
                x1[i], hf[i] = norms(r, mixed[i])
                gu[i] = gate_up(hf[i], 0)
                ff[i] = jnp.zeros((FFN_ROWS, D_MODEL), F32)
            if 0 <= c < n_chunks:
                nxt = gate_up(hf[i], c + 1) if c + 1 < n_chunks else None
                ff[i] = ff[i] + down(gu[i], c)
                gu[i] = nxt
                if c + 1 == n_chunks:
                    finish(r, x1[i], ff[i])


def _out_ffn(layer, x, attn, cp, w_out, g_post_mix, g_pre_ffn, g_post_ffn, w_gate, w_up, w_down):
    T = x.shape[0]
    tm = FFN_TM
    whole = lambda t: (0, 0)
    resident = functools.partial(pl.BlockSpec, index_map=whole, pipeline_mode=pl.Buffered(1))
    return pl.pallas_call(
        functools.partial(_out_ffn_kernel, layer),
        out_shape=jax.ShapeDtypeStruct((T, D_MODEL), F32),
        grid=(T // tm,),
        in_specs=[
            pl.BlockSpec((tm, D_MODEL), lambda t: (t, 0)),
            pl.BlockSpec((tm, ATTN_WIDTH), lambda t: (t, 0)),
            pl.BlockSpec((tm, CONV_WIDTH + POOL_WIDTH), lambda t: (t, 0)),
            resident((D_MODEL, D_MODEL)),
            pl.BlockSpec(g_post_mix.shape, whole),
            pl.BlockSpec(g_pre_ffn.shape, whole),
            pl.BlockSpec(g_post_ffn.shape, whole),
            resident((D_MODEL, D_FF)),
            resident((D_MODEL, D_FF)),
            resident((D_FF, D_MODEL)),
        ],
        out_specs=pl.BlockSpec((tm, D_MODEL), lambda t: (t, 0)),
        compiler_params=pltpu.CompilerParams(
            dimension_semantics=("arbitrary",),
            vmem_limit_bytes=VMEM_LIMIT),
        name="out_ffn",
    )(x, attn, cp, w_out, g_post_mix, g_pre_ffn, g_post_ffn, w_gate, w_up, w_down)


def _block_diag(pool_w):
    L, G, C, _ = pool_w.shape
    eye = jnp.eye(G, dtype=pool_w.dtype)
    return (eye[None, :, None, :, None] * pool_w[:, :, :, None, :]).reshape(L, G * C, G * C)


def kernel(x, w_in, w_out, conv_w, pool_w, pool_scale, g_pre_mix, g_post_mix, g_pre_ffn,
           g_post_ffn, w_gate, w_up, w_down):
    B, S, D = x.shape
    depth = w_in.shape[0]
    assert D == D_MODEL and IN_ROWS == MOBA_BLOCK and S % MOBA_BLOCK == 0 and S % IN_TM == 0 and (B * S) % FFN_TM == 0
    qx, kx = _attn_aux(S)
    pool_bd = _block_diag(pool_w).astype(BF16)
    for l in range(depth):
        qkv, cp, km, wo, wg, wu, wd = _in_proj(l, x, g_pre_mix, w_in, conv_w, pool_bd, pool_scale,
                                           (w_out, w_gate, w_up, w_down))
        attn = _moba_attn(qkv, km, qx, kx)
        x = _out_ffn(l, x.reshape(B * S, D), attn.reshape(B * S, ATTN_WIDTH),
                     cp.reshape(B * S, CONV_WIDTH + POOL_WIDTH), wo,
                     g_post_mix, g_pre_ffn, g_post_ffn, wg, wu, wd).reshape(B, S, D)
    return x
```
